```python
import math
import jax, jax.numpy as jnp
from jax import lax
import numpy as np

D_MODEL = 1024
BATCH = 2
SEQ = 8192
DEPTH = 2
DEC_BATCH = 8
DEC_SEQ = 32
PAST_LEN = 4096

CHUNK = 64
S5_WIDTH = 512
S5_GROUP = 16
S5_GROUPS = S5_WIDTH // S5_GROUP
S5_STATE = 64
RW_WIDTH = 512
RW_HEAD = 64
RW_HEADS = RW_WIDTH // RW_HEAD
RW_DECAY_RANK = 64
RW_A_RANK = 64
RW_V_RANK = 32
RW_G_RANK = 128
RW_COLS = 3 * RW_WIDTH + RW_DECAY_RANK + RW_A_RANK + RW_G_RANK
SHIFT_WIDTH = 2
N_BRANCHES = 2
N_IN = S5_WIDTH + RW_COLS + N_BRANCHES * D_MODEL
MOE_GROUPS = 4
EXPERTS_PER_GROUP = 4
N_EXPERTS = MOE_GROUPS * EXPERTS_PER_GROUP
TOP_K_INNER = 2
D_EXPERT = 512
PLE_DIM = 256
ALPHA = (2 * DEPTH) ** 0.25
BETA = (8 * DEPTH) ** -0.25
LN_EPS = 1e-5
GN_EPS = 64e-5

kernel_name = 'hybrid_s5_rwkv7_hmoe_stream_step'


def _layer_norm(x, g, b):
    xf = x.astype(jnp.float32)
    mu = jnp.mean(xf, -1, keepdims=True)
    var = jnp.mean(jnp.square(xf - mu), -1, keepdims=True)
    return ((xf - mu) * lax.rsqrt(var + LN_EPS) * g + b).astype(x.dtype)


def _complex_affine_combine(e1, e2):
    a1r, a1i, b1r, b1i = e1
    a2r, a2i, b2r, b2i = e2
    return (a2r * a1r - a2i * a1i,
            a2r * a1i + a2i * a1r,
            a2r * b1r - a2i * b1i + b2r,
            a2r * b1i + a2i * b1r + b2i)


def _s5_mixer(u, h0, a_re, a_im, log_dt, b_re, b_im, c_re, c_im, d, w_glu, b_glu):
    bsz, L, _ = u.shape
    f32 = jnp.float32
    uf = u.astype(f32).reshape(bsz, L, S5_GROUPS, S5_GROUP)
    a_re, a_im = a_re.astype(f32), a_im.astype(f32)
    dt = jnp.exp(log_dt.astype(f32))[:, None]
    mag = jnp.exp(a_re * dt)
    ab_re = mag * jnp.cos(a_im * dt)
    ab_im = mag * jnp.sin(a_im * dt)
    den = a_re * a_re + a_im * a_im
    n_re = ab_re - 1.0
    k_re = (n_re * a_re + ab_im * a_im) / den
    k_im = (ab_im * a_re - n_re * a_im) / den
    bb_re = k_re[..., None] * b_re - k_im[..., None] * b_im
    bb_im = k_re[..., None] * b_im + k_im[..., None] * b_re
    bu_re = jnp.einsum('gpc,blgc->blgp', bb_re, uf)
    bu_im = jnp.einsum('gpc,blgc->blgp', bb_im, uf)
    h0f = h0.astype(f32)
    h0_re, h0_im = h0f[..., 0], h0f[..., 1]
    bu_re = bu_re.at[:, 0].add(ab_re * h0_re - ab_im * h0_im)
    bu_im = bu_im.at[:, 0].add(ab_re * h0_im + ab_im * h0_re)
    shp = bu_re.shape
    elems = (jnp.broadcast_to(ab_re, shp), jnp.broadcast_to(ab_im, shp), bu_re, bu_im)
    _, _, h_re, h_im = lax.associative_scan(_complex_affine_combine, elems, axis=1)
    y = (jnp.einsum('gcp,blgp->blgc', c_re, h_re) - jnp.einsum('gcp,blgp->blgc', c_im, h_im)
         + d.reshape(S5_GROUPS, S5_GROUP) * uf)
    y = jax.nn.gelu(y.reshape(bsz, L, S5_WIDTH))
    y = y * jax.nn.sigmoid(y @ w_glu + b_glu)
    h_last = jnp.stack([h_re[:, -1], h_im[:, -1]], axis=-1)
    return y.astype(u.dtype), h_last.astype(h0.dtype)


def _wkv_scan(r, w, k, v, a, b, s0):
    def step(s, inp):
        r_t, w_t, k_t, v_t, a_t, b_t = inp
        sa = jnp.einsum('bhvk,bhk->bhv', s, a_t)
        s = (s * w_t[:, :, None, :] + sa[..., None] * b_t[:, :, None, :]
             + v_t[..., None] * k_t[:, :, None, :])
        return s, jnp.einsum('bhvk,bhk->bhv', s, r_t)
    xs = tuple(jnp.swapaxes(t, 0, 1) for t in (r, w, k, v, a, b))
    s_last, y = lax.scan(step, s0, xs)
    return jnp.swapaxes(y, 0, 1), s_last


def _rwkv7_mixer(zr, shift0, s0, v_first, vmix, mu, w0, w2, a0, a2, g2, k_k, k_a, r_k, ln_g, ln_b):
    bsz, L, _ = zr.shape
    f32 = jnp.float32
    prev = jnp.concatenate([shift0.astype(zr.dtype), zr[:, :-1]], axis=1)
    zs = zr + mu * (prev - zr)
    new_shift = zr[:, -(SHIFT_WIDTH - 1):]
    W = RW_WIDTH
    r, k, v, zw, za, zg = jnp.split(
        zs, [W, 2 * W, 3 * W, 3 * W + RW_DECAY_RANK, 3 * W + RW_DECAY_RANK + RW_A_RANK], axis=-1)
    w_log = -jax.nn.softplus(-(w0 + jnp.tanh(zw) @ w2)) - 0.5
    decay = jnp.exp(-jnp.exp(w_log.astype(f32)))
    a = jax.nn.sigmoid(a0 + za @ a2)
    g = jax.nn.sigmoid(zg) @ g2
    if vmix is None:
        v_first = v
    else:
        v0, v1, v2 = vmix
        v = v + (v_first - v) * jax.nn.sigmoid(v0 + (v @ v1) @ v2)
    heads = lambda t: t.astype(f32).reshape(bsz, L, RW_HEADS, RW_HEAD)
    kk = heads(k * k_k)
    kk = kk / jnp.maximum(jnp.sqrt(jnp.sum(kk * kk, -1, keepdims=True)), 1e-12)
    k = k * (1.0 + (a - 1.0) * k_a)
    rh, kh, vh, ah, wh = heads(r), heads(k), heads(v), heads(a), heads(decay)
    y, s_last = _wkv_scan(rh, wh, kh, vh, -kk, kk * ah, s0.astype(f32))
    ym = jnp.mean(y, -1, keepdims=True)
    yv = jnp.mean(jnp.square(y - ym), -1, keepdims=True)
    y = ((y - ym) * lax.rsqrt(yv + GN_EPS)).reshape(bsz, L, W) * ln_g + ln_b
    y = y + (jnp.sum(rh * kh * r_k, -1, keepdims=True) * vh).reshape(bsz, L, W)
    y = (y * g).astype(zr.dtype)
    return y, new_shift, s_last.astype(s0.dtype), v_first


def _hier_moe(h, w_rg, b_rg, w_re, b_re, w_gate, w_up, w_down):
    shp = h.shape
    t = h.reshape(-1, D_MODEL)
    f32 = jnp.float32
    g_logits = (t @ w_rg + b_rg).astype(f32)
    g_prob = jax.nn.softmax(g_logits, -1)
    g_sel = jnp.argmax(g_logits, -1)
    g_w = jnp.max(g_prob, -1, keepdims=True)
    g_onehot = jax.nn.one_hot(g_sel, MOE_GROUPS, dtype=f32)
    e_all = (jnp.einsum('td,dge->tge', t, w_re) + b_re).astype(f32)
    e_logits = jnp.einsum('tge,tg->te', e_all, g_onehot)
    top_v, top_i = lax.top_k(e_logits, TOP_K_INNER)
    top_w = jax.nn.softmax(top_v, -1) * g_w
    expert_id = g_sel[:, None] * EXPERTS_PER_GROUP + top_i
    combine = jnp.sum(jax.nn.one_hot(expert_id, N_EXPERTS, dtype=f32) * top_w[..., None], axis=1)
    hg = jnp.einsum('td,edf->tef', t, w_gate)
    hu = jnp.einsum('td,edf->tef', t, w_up)
    act = jax.nn.silu(hg) * hu * combine[..., None].astype(t.dtype)
    out = jnp.einsum('tef,efd->td', act, w_down)
    return out.reshape(shp).astype(h.dtype)


def _layer(x, p, shift0, wkv0, s50, v_first, vmix, lw):
    z = x @ lw['w_in']
    u, zr, zg = jnp.split(z, [S5_WIDTH, S5_WIDTH + RW_COLS], axis=-1)
    gate_a, gate_b = jnp.split(jax.nn.sigmoid(zg + lw['b_gate']), N_BRANCHES, axis=-1)
    y_a, s5_new = _s5_mixer(u, s50, lw['s5_a_re'], lw['s5_a_im'], lw['s5_log_dt'], lw['s5_b_re'],
                            lw['s5_b_im'], lw['s5_c_re'], lw['s5_c_im'], lw['s5_d'],
                            lw['s5_w_glu'], lw['s5_b_glu'])
    y_b, shift_new, wkv_new, v_first = _rwkv7_mixer(
        zr, shift0, wkv0, v_first, vmix, lw['rw_mu'], lw['rw_w0'], lw['rw_w2'], lw['rw_a0'],
        lw['rw_a2'], lw['rw_g2'], lw['rw_k_k'], lw['rw_k_a'], lw['rw_r_k'], lw['rw_ln_g'], lw['rw_ln_b'])
    merged = gate_a * (y_a @ lw['w_pa']) + gate_b * (y_b @ lw['w_pb'])
    x = _layer_norm(ALPHA * x + merged @ lw['w_o'], lw['ln1_g'], lw['ln1_b'])
    ple = jax.nn.sigmoid(x @ lw['w_pg'] + lw['b_pg']) * (p @ lw['w_ple'])
    ffn = _hier_moe(x, lw['w_rg'], lw['b_rg'], lw['w_re'], lw['b_re'],
                    lw['w_gate'], lw['w_up'], lw['w_down'])
    x = _layer_norm(ALPHA * x + ffn + ple, lw['ln2_g'], lw['ln2_b'])
    return x, shift_new, wkv_new, s5_new, v_first


def _trunk(x, p, shift0, wkv0, s50, prm, vmix):
    v_first = None
    shifts, wkvs, s5s = [], [], []
    for i in range(DEPTH):
        lw = {name: arr[i] for name, arr in prm.items()}
        vm = None if i == 0 else tuple(t[i - 1] for t in vmix)
        x, sh, wk, s5, v_first = _layer(x, p[i], shift0[i], wkv0[i], s50[i], v_first, vm, lw)
        shifts.append(sh)
        wkvs.append(wk)
        s5s.append(s5)
    return x, jnp.stack(shifts), jnp.stack(wkvs), jnp.stack(s5s)


def setup_inputs(seed: int = 0) -> dict:
    key = jax.random.key(seed)
    ks = iter(jax.random.split(key, 64))
    f32 = jnp.float32

    def nrm(shape, scale):
        return scale * jax.random.normal(next(ks), shape, f32)

    def unif(shape, lo, hi):
        return jax.random.uniform(next(ks), shape, f32, lo, hi)

    L, D, G, P, C, W = DEPTH, D_MODEL, S5_GROUPS, S5_STATE, S5_GROUP, RW_WIDTH
    n = jnp.arange(P, dtype=f32)
    return {
        'x_prompt': nrm((BATCH, SEQ, D), 1.0),
        'x_sample': nrm((DEC_BATCH, DEC_SEQ, D), 1.0),
        'state_shift': nrm((L, DEC_BATCH, SHIFT_WIDTH - 1, RW_COLS), 1.0),
        'state_wkv': nrm((L, DEC_BATCH, RW_HEADS, RW_HEAD, RW_HEAD), 1.0),
        'state_s5': nrm((L, DEC_BATCH, G, P, 2), 0.1),
        'p_prompt': nrm((L, BATCH, SEQ, PLE_DIM), 1.0),
        'p_sample': nrm((L, DEC_BATCH, DEC_SEQ, PLE_DIM), 1.0),
        'w_in': nrm((L, D, N_IN), D ** -0.5),
        'b_gate': nrm((L, N_BRANCHES * D), 0.1),
        's5_a_re': -0.5 + nrm((L, G, P), 0.01),
        's5_a_im': math.pi * n + nrm((L, G, P), 0.01),
        's5_log_dt': unif((L, G), math.log(1e-3), math.log(1e-1)),
        's5_b_re': nrm((L, G, P, C), (2 * C) ** -0.5),
        's5_b_im': nrm((L, G, P, C), (2 * C) ** -0.5),
        's5_c_re': nrm((L, G, C, P), 0.5 ** 0.5),
        's5_c_im': nrm((L, G, C, P), 0.5 ** 0.5),
        's5_d': nrm((L, S5_WIDTH), 1.0),
        's5_w_glu': nrm((L, S5_WIDTH, S5_WIDTH), S5_WIDTH ** -0.5),
        's5_b_glu': nrm((L, S5_WIDTH), 0.02),
        'rw_mu': unif((L, RW_COLS), 0.0, 1.0),
        'rw_w0': unif((L, W), -5.0, 0.0),
        'rw_w2': nrm((L, RW_DECAY_RANK, W), 0.5 * RW_DECAY_RANK ** -0.5),
        'rw_a0': nrm((L, W), 0.5),
        'rw_a2': nrm((L, RW_A_RANK, W), 0.5 * RW_A_RANK ** -0.5),
        'rw_g2': nrm((L, RW_G_RANK, W), RW_G_RANK ** -0.5),
        'rw_v0': nrm((L - 1, W), 0.5),
        'rw_v1': nrm((L - 1, W, RW_V_RANK), W ** -0.5),
        'rw_v2': nrm((L - 1, RW_V_RANK, W), 0.5 * RW_V_RANK ** -0.5),
        'rw_k_k': 0.85 + nrm((L, W), 0.05),
        'rw_k_a': 1.0 + nrm((L, W), 0.05),
        'rw_r_k': nrm((L, RW_HEADS, RW_HEAD), 0.1),
        'rw_ln_g': 1.0 + nrm((L, W), 0.05),
        'rw_ln_b': nrm((L, W), 0.02),
        'w_pa': nrm((L, S5_WIDTH, D), S5_WIDTH ** -0.5),
        'w_pb': nrm((L, W, D), W ** -0.5),
        'w_o': nrm((L, D, D), BETA * D ** -0.5),
        'ln1_g': 1.0 + nrm((L, D), 0.05),
        'ln1_b': nrm((L, D), 0.02),
        'w_rg': nrm((L, D, MOE_GROUPS), D ** -0.5),
        'b_rg': nrm((L, MOE_GROUPS), 0.01),
        'w_re': nrm((L, D, MOE_GROUPS, EXPERTS_PER_GROUP), D ** -0.5),
        'b_re': nrm((L, MOE_GROUPS, EXPERTS_PER_GROUP), 0.01),
        'w_gate': nrm((L, N_EXPERTS, D, D_EXPERT), D ** -0.5),
        'w_up': nrm((L, N_EXPERTS, D, D_EXPERT), D ** -0.5),
        'w_down': nrm((L, N_EXPERTS, D_EXPERT, D), BETA * D_EXPERT ** -0.5),
        'w_ple': nrm((L, PLE_DIM, D), BETA * PLE_DIM ** -0.5),
        'w_pg': nrm((L, D, D), D ** -0.5),
        'b_pg': nrm((L, D), 0.02),
        'ln2_g': 1.0 + nrm((L, D), 0.05),
        'ln2_b': nrm((L, D), 0.02),
    }


def reference(x_prompt, x_sample, state_shift, state_wkv, state_s5, p_prompt, p_sample,
              w_in, b_gate, s5_a_re, s5_a_im, s5_log_dt, s5_b_re, s5_b_im, s5_c_re, s5_c_im,
              s5_d, s5_w_glu, s5_b_glu, rw_mu, rw_w0, rw_w2, rw_a0, rw_a2, rw_g2, rw_v0, rw_v1,
              rw_v2, rw_k_k, rw_k_a, rw_r_k, rw_ln_g, rw_ln_b, w_pa, w_pb, w_o, ln1_g, ln1_b,
              w_rg, b_rg, w_re, b_re, w_gate, w_up, w_down, w_ple, w_pg, b_pg, ln2_g, ln2_b):
    prm = {
        'w_in': w_in, 'b_gate': b_gate,
        's5_a_re': s5_a_re, 's5_a_im': s5_a_im, 's5_log_dt': s5_log_dt,
        's5_b_re': s5_b_re, 's5_b_im': s5_b_im, 's5_c_re': s5_c_re, 's5_c_im': s5_c_im,
        's5_d': s5_d, 's5_w_glu': s5_w_glu, 's5_b_glu': s5_b_glu,
        'rw_mu': rw_mu, 'rw_w0': rw_w0, 'rw_w2': rw_w2, 'rw_a0': rw_a0, 'rw_a2': rw_a2,
        'rw_g2': rw_g2, 'rw_k_k': rw_k_k, 'rw_k_a': rw_k_a, 'rw_r_k': rw_r_k,
        'rw_ln_g': rw_ln_g, 'rw_ln_b': rw_ln_b,
        'w_pa': w_pa, 'w_pb': w_pb, 'w_o': w_o, 'ln1_g': ln1_g, 'ln1_b': ln1_b,
        'w_rg': w_rg, 'b_rg': b_rg, 'w_re': w_re, 'b_re': b_re,
        'w_gate': w_gate, 'w_up': w_up, 'w_down': w_down,
        'w_ple': w_ple, 'w_pg': w_pg, 'b_pg': b_pg, 'ln2_g': ln2_g, 'ln2_b': ln2_b,
    }
    vmix = (rw_v0, rw_v1, rw_v2)
    bp = x_prompt.shape[0]
    dtype = x_prompt.dtype
    shift0 = jnp.zeros((DEPTH, bp, SHIFT_WIDTH - 1, RW_COLS), dtype)
    wkv0 = jnp.zeros((DEPTH, bp, RW_HEADS, RW_HEAD, RW_HEAD), dtype)
    s50 = jnp.zeros((DEPTH, bp, S5_GROUPS, S5_STATE, 2), dtype)
    y_prompt, shift_p, wkv_p, s5_p = _trunk(x_prompt, p_prompt, shift0, wkv0, s50, prm, vmix)
    y_sample, shift_s, wkv_s, s5_s = _trunk(x_sample, p_sample, state_shift, state_wkv, state_s5,
                                            prm, vmix)
    return (y_prompt, y_sample, shift_p, wkv_p, s5_p, shift_s, wkv_s, s5_s)
```

```python
import functools
import math

import jax
import jax.numpy as jnp
from jax import lax
from jax.experimental import pallas as pl
from jax.experimental.pallas import tpu as pltpu

F32 = jnp.float32
BF16 = jnp.bfloat16
HIGHEST = lax.Precision.HIGHEST

DEPTH = 2
ALPHA = (2 * DEPTH) ** 0.25
LN_EPS = 1e-5
GN_EPS = 64e-5

S5_GROUP = 16
S5_STATE = 64
S5_GROUPS_PER_BLOCK = 8
RW_HEAD = 64
SHIFT_ROWS = 8
ROUTER_LANES = 128
WKV_CHUNK = 64
VMEM_LIMIT = 56 * 1024 * 1024


def _bdot(a, b):
    return jnp.dot(a.astype(BF16), b.astype(BF16), preferred_element_type=F32)


def _fdot(a, b):
    return jnp.dot(a, b, precision=HIGHEST, preferred_element_type=F32)


def _fdot_nt(a, b):
    return lax.dot_general(a, b, (((1,), (1,)), ((), ())), precision=HIGHEST,
                           preferred_element_type=F32)


def _fdot_tn(a, b):
    return lax.dot_general(a, b, (((0,), (0,)), ((), ())), precision=HIGHEST,
                           preferred_element_type=F32)


def _sigmoid(x):
    return 1.0 / (1.0 + jnp.exp(-x))


def _layer_norm(h, g, b):
    mu = jnp.mean(h, axis=-1, keepdims=True)
    d = h - mu
    var = jnp.mean(d * d, axis=-1, keepdims=True)
    return d * lax.rsqrt(var + LN_EPS) * g + b


def _params(sem):
    return pltpu.CompilerParams(dimension_semantics=sem, vmem_limit_bytes=VMEM_LIMIT)


def _full(shape):
    n = len(shape)
    return pl.BlockSpec(shape, lambda *_: (0,) * n)


def _inproj_body(x_ref, w_ref, bg_ref, u_ref, zr_ref, gate_ref, *, s5w, rwc):
    xb = x_ref[...].astype(BF16)
    u_ref[...] = jnp.dot(xb, w_ref[:, :s5w], preferred_element_type=F32)
    zr_ref[...] = jnp.dot(xb, w_ref[:, s5w:s5w + rwc], preferred_element_type=F32)
    zg = jnp.dot(xb, w_ref[:, s5w + rwc:], preferred_element_type=F32)
    gate_ref[...] = _sigmoid(zg + bg_ref[...])


def _inproj(x, w_in, b_gate, s5w, rwc, tm):
    t, d = x.shape
    n_in = w_in.shape[1]
    ng = n_in - s5w - rwc
    row = lambda n: pl.BlockSpec((tm, n), lambda i: (i, 0))
    return pl.pallas_call(
        functools.partial(_inproj_body, s5w=s5w, rwc=rwc),
        grid=(t // tm,),
        in_specs=[row(d), _full((d, n_in)), _full((1, ng))],
        out_specs=[row(s5w), row(rwc), row(ng)],
        out_shape=[jax.ShapeDtypeStruct((t, s5w), F32), jax.ShapeDtypeStruct((t, rwc), F32),
                   jax.ShapeDtypeStruct((t, ng), F32)],
        compiler_params=_params(("parallel",)),
        name="inproj",
    )(x, w_in, b_gate)


def _s5_body(u_ref, h0r_ref, h0i_ref, abr_ref, abi_ref, wbu_ref, wy_ref, d_ref, wglu_ref, bglu_ref,
             y_ref, hlr_ref, hli_ref, bur, bui, hr, hi, str_, sti, *, tl, nblk, cb, sb):
    i = pl.program_id(1)

    @pl.when(i == 0)
    def _():
        str_[...] = h0r_ref[...]
        sti[...] = h0i_ref[...]

    u = u_ref[...]
    for j in range(nblk):
        bu = _bdot(u[:, cb * j:cb * (j + 1)], wbu_ref[j])
        bur[:, sb * j:sb * (j + 1)] = bu[:, :sb]
        bui[:, sb * j:sb * (j + 1)] = bu[:, sb:]

    ar = abr_ref[...]
    ai = abi_ref[...]

    def step(t, carry):
        pr, pi = carry
        nr = ar * pr - ai * pi + bur[pl.ds(t, 1), :]
        ni = ar * pi + ai * pr + bui[pl.ds(t, 1), :]
        hr[pl.ds(t, 1), :] = nr
        hi[pl.ds(t, 1), :] = ni
        return nr, ni

    lr, li = lax.fori_loop(0, tl, step, (str_[...], sti[...]), unroll=8)
    str_[...] = lr
    sti[...] = li
    hlr_ref[...] = lr
    hli_ref[...] = li

    ys = []
    for j in range(nblk):
        ys.append(_bdot(hr[:, sb * j:sb * (j + 1)], wy_ref[j, :sb, :])
                  + _bdot(hi[:, sb * j:sb * (j + 1)], wy_ref[j, sb:, :]))
    y = jnp.concatenate(ys, axis=1) + d_ref[...] * u
    y = jax.nn.gelu(y)
    y_ref[...] = y * _sigmoid(_bdot(y, wglu_ref[...]) + bglu_ref[...])


def _s5_mixer(u, h0r, h0i, prm, tl):
    b, l, w = u.shape
    ns = h0r.shape[-1]
    nblk = w // (S5_GROUP * S5_GROUPS_PER_BLOCK)
    cb = w // nblk
    sb = ns // nblk
    seq = lambda n: pl.BlockSpec((None, tl, n), lambda bi, i: (bi, i, 0))
    st = pl.BlockSpec((None, 1, ns), lambda bi, i: (bi, 0, 0))
    return pl.pallas_call(
        functools.partial(_s5_body, tl=tl, nblk=nblk, cb=cb, sb=sb),
        grid=(b, l // tl),
        in_specs=[seq(w), st, st, _full((1, ns)), _full((1, ns)), _full(prm["wbu"].shape),
                  _full(prm["wy"].shape), _full((1, w)), _full((w, w)), _full((1, w))],
        out_specs=[seq(w), st, st],
        out_shape=[jax.ShapeDtypeStruct((b, l, w), F32), jax.ShapeDtypeStruct((b, 1, ns), F32),
                   jax.ShapeDtypeStruct((b, 1, ns), F32)],
        scratch_shapes=[pltpu.VMEM((tl, ns), F32)] * 4 + [pltpu.VMEM((1, ns), F32)] * 2,
        compiler_params=_params(("parallel", "arbitrary")),
        name="s5_mixer",
    )(u, h0r, h0i, prm["abr"], prm["abi"], prm["wbu"], prm["wy"], prm["d"], prm["wglu"], prm["bglu"])


def _s5_discretize(a_re, a_im, log_dt, b_re, b_im, c_re, c_im):
    g, p = a_re.shape
    c = b_re.shape[-1]
    gb = S5_GROUPS_PER_BLOCK
    nblk = g // gb
    dt = jnp.exp(log_dt)[:, None]
    mag = jnp.exp(a_re * dt)
    ab_re = mag * jnp.cos(a_im * dt)
    ab_im = mag * jnp.sin(a_im * dt)
    den = a_re * a_re + a_im * a_im
    n_re = ab_re - 1.0
    k_re = (n_re * a_re + ab_im * a_im) / den
    k_im = (ab_im * a_re - n_re * a_im) / den
    bb_re = k_re[..., None] * b_re - k_im[..., None] * b_im
    bb_im = k_re[..., None] * b_im + k_im[..., None] * b_re
    eye = jnp.eye(gb, dtype=F32)

    def pack_in(bb):
        return jnp.einsum("jgpc,gh->jgchp", bb.reshape(nblk, gb, p, c), eye).reshape(nblk, gb * c, gb * p)

    def pack_out(cc):
        return jnp.einsum("jgcp,gh->jgphc", cc.reshape(nblk, gb, c, p), eye).reshape(nblk, gb * p, gb * c)

    wbu = jnp.concatenate([pack_in(bb_re), pack_in(bb_im)], axis=-1).astype(BF16)
    wy = jnp.concatenate([pack_out(c_re), -pack_out(c_im)], axis=1).astype(BF16)
    return ab_re.reshape(1, g * p), ab_im.reshape(1, g * p), wbu, wy


def _rw_pre_body(*refs, w, dr, ar_, has_vmix):
    (zr_ref, zp_ref, sh0_ref, mu_ref, w0_ref, w2_ref, a0_ref, a2_ref, g2_ref, kk_ref, ka_ref) = refs[:11]
    refs = refs[11:]
    if has_vmix:
        vf_ref, v0_ref, v1_ref, v2_ref = refs[:4]
        refs = refs[4:]
    r_ref, k_ref, v_ref, lw_ref, kkraw_ref, a_ref, g_ref = refs

    i = pl.program_id(1)
    z = zr_ref[...]
    last_prev = jnp.where(i == 0, sh0_ref[...], zp_ref[SHIFT_ROWS - 1:SHIFT_ROWS, :])
    row = lax.broadcasted_iota(jnp.int32, z.shape, 0)
    prev = jnp.where(row == 0, last_prev, pltpu.roll(z, 1, axis=0))
    zs = z + mu_ref[...] * (prev - z)

    r = zs[:, :w]
    k = zs[:, w:2 * w]
    v = zs[:, 2 * w:3 * w]
    zw = zs[:, 3 * w:3 * w + dr]
    za = zs[:, 3 * w + dr:3 * w + dr + ar_]
    zg = zs[:, 3 * w + dr + ar_:]

    wl = w0_ref[...] + _bdot(jnp.tanh(zw), w2_ref[...])
    nwl = -wl
    softplus = jnp.maximum(nwl, 0.0) + jnp.log(1.0 + jnp.exp(-jnp.abs(nwl)))
    lw_ref[...] = -jnp.exp(-softplus - 0.5)
    a = _sigmoid(a0_ref[...] + _bdot(za, a2_ref[...]))
    g_ref[...] = _bdot(_sigmoid(zg), g2_ref[...])
    if has_vmix:
        mix = _sigmoid(v0_ref[...] + _bdot(_bdot(v, v1_ref[...]), v2_ref[...]))
        v = v + (vf_ref[...] - v) * mix
    r_ref[...] = r
    v_ref[...] = v
    a_ref[...] = a
    kkraw_ref[...] = k * kk_ref[...]
    k_ref[...] = k * (1.0 + (a - 1.0) * ka_ref[...])


def _rw_pre(zr, shift0, prm, v_first, tl):
    b, l, cols = zr.shape
    w = prm["w0"].shape[-1]
    dr = prm["w2"].shape[0]
    ar_ = prm["a2"].shape[0]
    has_vmix = v_first is not None
    nsub = tl // SHIFT_ROWS
    seq = lambda n: pl.BlockSpec((None, tl, n), lambda bi, i: (bi, i, 0))
    prev_spec = pl.BlockSpec((None, SHIFT_ROWS, cols),
                             lambda bi, i: (bi, jnp.maximum(i * nsub - 1, 0), 0))
    vec = lambda n: _full((1, n))
    ins = [zr, zr, shift0, prm["mu"], prm["w0"], prm["w2"], prm["a0"], prm["a2"], prm["g2"],
           prm["k_k"], prm["k_a"]]
    specs = [seq(cols), prev_spec, pl.BlockSpec((None, 1, cols), lambda bi, i: (bi, 0, 0)), vec(cols),
             vec(w), _full(prm["w2"].shape), vec(w), _full(prm["a2"].shape), _full(prm["g2"].shape),
             vec(w), vec(w)]
    if has_vmix:
        ins += [v_first, prm["v0"], prm["v1"], prm["v2"]]
        specs += [seq(w), vec(w), _full(prm["v1"].shape), _full(prm["v2"].shape)]
    return pl.pallas_call(
        functools.partial(_rw_pre_body, w=w, dr=dr, ar_=ar_, has_vmix=has_vmix),
        grid=(b, l // tl),
        in_specs=specs,
        out_specs=[seq(w)] * 7,
        out_shape=[jax.ShapeDtypeStruct((b, l, w), F32)] * 7,
        compiler_params=_params(("parallel", "parallel")),
        name="rwkv_pre",
    )(*ins)


def _wkv_body(r_ref, k_ref, v_ref, lw_ref, kk_ref, a_ref, g_ref, rk_ref, lng_ref, lnb_ref, s0_ref,
              y_ref, sl_ref, s_scr, *, tl, c, nh, hd):
    i = pl.program_id(1)

    @pl.when(i == 0)
    def _():
        s_scr[...] = s0_ref[...]

    rowi = lax.broadcasted_iota(jnp.int32, (c, c), 0)
    coli = lax.broadcasted_iota(jnp.int32, (c, c), 1)
    tri_incl = (coli <= rowi).astype(F32)
    lower = coli <= rowi
    strict = coli < rowi
    levels = int(math.log2(c))

    def chunk(ci, carry):
        r0 = pl.multiple_of(ci * c, c)
        rows = pl.ds(r0, c)
        for h in range(nh):
            hs = slice(h * hd, (h + 1) * hd)
            rh = r_ref[rows, hs]
            kh = k_ref[rows, hs]
            vh = v_ref[rows, hs]
            lwh = lw_ref[rows, hs]
            kkh = kk_ref[rows, hs]
            ah = a_ref[rows, hs]
            s_prev = s_scr[h]

            nrm = jnp.sqrt(jnp.sum(kkh * kkh, axis=-1, keepdims=True))
            kkn = kkh / jnp.maximum(nrm, 1e-12)
            av = -kkn
            bv = kkn * ah

            cl = _fdot(tri_incl, lwh)
            clp = cl - lwh
            cl_end = cl[c - 1:c, :]
            einv = jnp.exp(-cl)
            at = av * jnp.exp(clp)
            rt = rh * jnp.exp(cl)
            kt = kh * einv
            bt = bv * einv
            to_end = jnp.exp(cl_end - cl)
            kd = kh * to_end
            bd = bv * to_end

            n_ab = jnp.where(strict, _fdot_nt(at, bt), 0.0)
            a_ak = jnp.where(strict, _fdot_nt(at, kt), 0.0)
            a_rb = jnp.where(lower, _fdot_nt(rt, bt), 0.0)
            a_rk = jnp.where(lower, _fdot_nt(rt, kt), 0.0)

            x = _fdot_nt(at, s_prev) + _fdot(a_ak, vh)
            pw = n_ab
            for lv in range(levels):
                x = x + _fdot(pw, x)
                if lv + 1 < levels:
                    pw = _fdot(pw, pw)
            u = x

            y = _fdot_nt(rt, s_prev) + _fdot(a_rb, u) + _fdot(a_rk, vh)
            s_scr[h] = s_prev * jnp.exp(cl_end) + _fdot_tn(u, bd) + _fdot_tn(vh, kd)

            ym = jnp.mean(y, axis=-1, keepdims=True)
            yc = y - ym
            yv = jnp.mean(yc * yc, axis=-1, keepdims=True)
            yn = yc * lax.rsqrt(yv + GN_EPS) * lng_ref[:, hs] + lnb_ref[:, hs]
            bonus = jnp.sum(rh * kh * rk_ref[:, hs], axis=-1, keepdims=True) * vh
            y_ref[rows, hs] = (yn + bonus) * g_ref[rows, hs]
        return carry

    lax.fori_loop(0, tl // c, chunk, 0)

    @pl.when(i == pl.num_programs(1) - 1)
    def _():
        sl_ref[...] = s_scr[...]


def _wkv(r, k, v, lw, kk, a, g, prm, s0, tl, c):
    b, l, w = r.shape
    nh, hd = s0.shape[1], s0.shape[2]
    seq = pl.BlockSpec((None, tl, w), lambda bi, i: (bi, i, 0))
    st = pl.BlockSpec((None, nh, hd, hd), lambda bi, i: (bi, 0, 0, 0))
    vec = _full((1, w))
    return pl.pallas_call(
        functools.partial(_wkv_body, tl=tl, c=c, nh=nh, hd=hd),
        grid=(b, l // tl),
        in_specs=[seq] * 7 + [vec, vec, vec, st],
        out_specs=[seq, st],
        out_shape=[jax.ShapeDtypeStruct((b, l, w), F32), jax.ShapeDtypeStruct(s0.shape, F32)],
        scratch_shapes=[pltpu.VMEM((nh, hd, hd), F32)],
        compiler_params=_params(("parallel", "arbitrary")),
        name="wkv",
    )(r, k, v, lw, kk, a, g, prm["r_k"], prm["ln_g"], prm["ln_b"], s0)


def _merge_body(x_ref, ya_ref, yb_ref, gate_ref, p_ref, wpa_ref, wpb_ref, wo_ref, g1_ref, b1_ref,
                wpg_ref, bpg_ref, wple_ref, wr_ref, br_ref, x1_ref, base_ref, cmb_ref,
                *, d, ngroups, nper):
    gate = gate_ref[...]
    merged = gate[:, :d] * _bdot(ya_ref[...], wpa_ref[...]) + gate[:, d:] * _bdot(yb_ref[...], wpb_ref[...])
    x1 = _layer_norm(ALPHA * x_ref[...] + _bdot(merged, wo_ref[...]), g1_ref[...], b1_ref[...])
    x1_ref[...] = x1.astype(BF16)
    ple = _sigmoid(_bdot(x1, wpg_ref[...]) + bpg_ref[...]) * _bdot(p_ref[...], wple_ref[...])
    base_ref[...] = ALPHA * x1 + ple

    logits = _fdot(x1, wr_ref[...]) + br_ref[...]
    lane = lax.broadcasted_iota(jnp.int32, logits.shape, 1)
    neg = -jnp.inf
    big = logits.shape[1]
    first_of = lambda hit: jnp.min(jnp.where(hit, lane, big), axis=-1, keepdims=True)
    gl = jnp.where(lane < ngroups, logits, neg)
    gmax = jnp.max(gl, axis=-1, keepdims=True)
    g_sel = first_of(gl == gmax)
    g_w = 1.0 / jnp.sum(jnp.exp(gl - gmax), axis=-1, keepdims=True)
    lo = ngroups + nper * g_sel
    el = jnp.where((lane >= lo) & (lane < lo + nper), logits, neg)
    m1 = jnp.max(el, axis=-1, keepdims=True)
    i1 = first_of(el == m1)
    el2 = jnp.where(lane == i1, neg, el)
    m2 = jnp.max(el2, axis=-1, keepdims=True)
    i2 = first_of(el2 == m2)
    e2 = jnp.exp(m2 - m1)
    w1 = g_w / (1.0 + e2)
    w2 = g_w * e2 / (1.0 + e2)
    cmb_ref[...] = jnp.where(lane == i1, w1, 0.0) + jnp.where(lane == i2, w2, 0.0)


def _merge(x, ya, yb, gate, p, prm, tm):
    t, d = x.shape
    row = lambda n: pl.BlockSpec((tm, n), lambda i: (i, 0))
    ws = [prm["w_pa"], prm["w_pb"], prm["w_o"], prm["ln1_g"], prm["ln1_b"], prm["w_pg"], prm["b_pg"],
          prm["w_ple"], prm["w_router"], prm["b_router"]]
    return pl.pallas_call(
        functools.partial(_merge_body, d=d, ngroups=prm["ngroups"], nper=prm["nper"]),
        grid=(t // tm,),
        in_specs=[row(d), row(ya.shape[1]), row(yb.shape[1]), row(gate.shape[1]), row(p.shape[1])]
                 + [_full(a.shape) for a in ws],
        out_specs=[row(d), row(d), row(ROUTER_LANES)],
        out_shape=[jax.ShapeDtypeStruct((t, d), BF16), jax.ShapeDtypeStruct((t, d), F32),
                   jax.ShapeDtypeStruct((t, ROUTER_LANES), F32)],
        compiler_params=_params(("parallel",)),
        name="merge_ln1_router",
    )(x, ya, yb, gate, p, *ws)


def _moe_body(x1_ref, base_ref, cmb_ref, wg_ref, wu_ref, wd_ref, g2_ref, b2_ref, o_ref, acc, *, lane0):
    e = pl.program_id(1)

    @pl.when(e == 0)
    def _():
        acc[...] = jnp.zeros_like(acc)

    xb = x1_ref[...]
    hg = jnp.dot(xb, wg_ref[...], preferred_element_type=F32)
    hu = jnp.dot(xb, wu_ref[...], preferred_element_type=F32)
    cmb = cmb_ref[...]
    lane = lax.broadcasted_iota(jnp.int32, cmb.shape, 1)
    ce = jnp.sum(jnp.where(lane == e + lane0, cmb, 0.0), axis=-1, keepdims=True)
    act = hg * _sigmoid(hg) * hu * ce
    acc[...] += _bdot(act, wd_ref[...])

    @pl.when(e == pl.num_programs(1) - 1)
    def _():
        o_ref[...] = _layer_norm(base_ref[...] + acc[...], g2_ref[...], b2_ref[...])


def _moe(x1, base, cmb, prm, tm):
    t, d = base.shape
    ne, _, f = prm["w_gate"].shape
    row = lambda n: pl.BlockSpec((tm, n), lambda i, e: (i, 0))
    return pl.pallas_call(
        functools.partial(_moe_body, lane0=prm["ngroups"]),
        grid=(t // tm, ne),
        in_specs=[row(d), row(d), row(ROUTER_LANES),
                  pl.BlockSpec((None, d, f), lambda i, e: (e, 0, 0)),
                  pl.BlockSpec((None, d, f), lambda i, e: (e, 0, 0)),
                  pl.BlockSpec((None, f, d), lambda i, e: (e, 0, 0)),
                  pl.BlockSpec((1, d), lambda i, e: (0, 0)), pl.BlockSpec((1, d), lambda i, e: (0, 0))],
        out_specs=row(d),
        out_shape=jax.ShapeDtypeStruct((t, d), F32),
        scratch_shapes=[pltpu.VMEM((tm, d), F32)],
        compiler_params=_params(("parallel", "arbitrary")),
        name="moe_ln2",
    )(x1, base, cmb, prm["w_gate"], prm["w_up"], prm["w_down"], prm["ln2_g"], prm["ln2_b"])


def _tile(n, target):
    t = min(n, target)
    assert n % t == 0, (n, t)
    return t


def _layer(x, p, shift0, wkv0, s5r0, s5i0, v_first, lp):
    b, l, d = x.shape
    t = b * l
    s5w = lp["s5"]["d"].shape[-1]
    rwc = lp["rw"]["mu"].shape[-1]
    u, zr, gate = _inproj(x.reshape(t, d), lp["w_in"], lp["b_gate"], s5w, rwc, _tile(t, 256))
    ya, s5r, s5i = _s5_mixer(u.reshape(b, l, s5w), s5r0, s5i0, lp["s5"], _tile(l, 256))
    zr3 = zr.reshape(b, l, rwc)
    tl = _tile(l, 256)
    r, k, v, lw, kk, a, g = _rw_pre(zr3, shift0, lp["rw"], v_first, tl)
    yb, wkv_new = _wkv(r, k, v, lw, kk, a, g, lp["rw"], wkv0, tl, min(WKV_CHUNK, l))
    x1, base, cmb = _merge(x.reshape(t, d), ya.reshape(t, s5w), yb.reshape(t, -1), gate,
                           p.reshape(t, -1), lp["mrg"], _tile(t, 256))
    out = _moe(x1, base, cmb, lp["moe"], _tile(t, 512))
    shift_new = zr3[:, l - 1:, :]
    return out.reshape(b, l, d), shift_new, wkv_new, s5r, s5i, (v if v_first is None else v_first)


def _trunk(x, p, shift0, wkv0, s50, layers):
    b = x.shape[0]
    shifts, wkvs, s5s = [], [], []
    v_first = None
    for i, lp in enumerate(layers):
        ns = s50.shape[2] * s50.shape[3]
        s5r0 = s50[i, ..., 0].reshape(b, 1, ns)
        s5i0 = s50[i, ..., 1].reshape(b, 1, ns)
        x, sh, wk, s5r, s5i, v_first = _layer(x, p[i], shift0[i], wkv0[i], s5r0, s5i0, v_first, lp)
        shifts.append(sh)
        wkvs.append(wk)
        s5s.append(jnp.stack([s5r.reshape(s50.shape[1:4]), s5i.reshape(s50.shape[1:4])], axis=-1))
    return x, jnp.stack(shifts), jnp.stack(wkvs), jnp.stack(s5s)


def kernel(x_prompt, x_sample, state_shift, state_wkv, state_s5, p_prompt, p_sample, w_in, b_gate, s5_a_re, s5_a_im, s5_log_dt, s5_b_re, s5_b_im, s5_c_re, s5_c_im, s5_d, s5_w_glu, s5_b_glu, rw_mu, rw_w0, rw_w2, rw_a0, rw_a2, rw_g2, rw_v0, rw_v1, rw_v2, rw_k_k, rw_k_a, rw_r_k, rw_ln_g, rw_ln_b, w_pa, w_pb, w_o, ln1_g, ln1_b, w_rg, b_rg, w_re, b_re, w_gate, w_up, w_down, w_ple, w_pg, b_pg, ln2_g, ln2_b):
    depth, d = w_in.shape[0], w_in.shape[1]
    ngroups = w_rg.shape[-1]
    nper = w_re.shape[-1]
    nexp = ngroups * nper
    layers = []
    for i in range(depth):
        abr, abi, wbu, wy = _s5_discretize(s5_a_re[i], s5_a_im[i], s5_log_dt[i], s5_b_re[i], s5_b_im[i],
                                           s5_c_re[i], s5_c_im[i])
        s5 = dict(abr=abr, abi=abi, wbu=wbu, wy=wy, d=s5_d[i][None], wglu=s5_w_glu[i].astype(BF16),
                  bglu=s5_b_glu[i][None])
        rw = dict(mu=rw_mu[i][None], w0=rw_w0[i][None], w2=rw_w2[i].astype(BF16), a0=rw_a0[i][None],
                  a2=rw_a2[i].astype(BF16), g2=rw_g2[i].astype(BF16), k_k=rw_k_k[i][None],
                  k_a=rw_k_a[i][None], r_k=rw_r_k[i].reshape(1, -1), ln_g=rw_ln_g[i][None],
                  ln_b=rw_ln_b[i][None])
        if i > 0:
            rw.update(v0=rw_v0[i - 1][None], v1=rw_v1[i - 1].astype(BF16), v2=rw_v2[i - 1].astype(BF16))
        pad = ROUTER_LANES - ngroups - nexp
        w_router = jnp.concatenate([w_rg[i], w_re[i].reshape(d, nexp), jnp.zeros((d, pad), F32)], axis=1)
        b_router = jnp.concatenate([b_rg[i], b_re[i].reshape(nexp), jnp.zeros((pad,), F32)])[None]
        mrg = dict(w_pa=w_pa[i].astype(BF16), w_pb=w_pb[i].astype(BF16), w_o=w_o[i].astype(BF16),
                   ln1_g=ln1_g[i][None], ln1_b=ln1_b[i][None], w_pg=w_pg[i].astype(BF16),
                   b_pg=b_pg[i][None], w_ple=w_ple[i].astype(BF16), w_router=w_router,
                   b_router=b_router, ngroups=ngroups, nper=nper)
        moe = dict(w_gate=w_gate[i].astype(BF16), w_up=w_up[i].astype(BF16),
                   w_down=w_down[i].astype(BF16), ln2_g=ln2_g[i][None], ln2_b=ln2_b[i][None],
                   ngroups=ngroups)
        layers.append(dict(w_in=w_in[i].astype(BF16), b_gate=b_gate[i][None], s5=s5, rw=rw, mrg=mrg,
                           moe=moe))

    bp = x_prompt.shape[0]
    dt = x_prompt.dtype
    zero_shift = jnp.zeros((depth, bp) + state_shift.shape[2:], dt)
    zero_wkv = jnp.zeros((depth, bp) + state_wkv.shape[2:], dt)
    zero_s5 = jnp.zeros((depth, bp) + state_s5.shape[2:], dt)
    y_p, sh_p, wkv_p, s5_p = _trunk(x_prompt, p_prompt, zero_shift, zero_wkv, zero_s5, layers)
    y_s, sh_s, wkv_s, s5_s = _trunk(x_sample, p_sample, state_shift, state_wkv, state_s5, layers)
    return (y_p, y_s, sh_p, wkv_p, s5_p, sh_s, wkv_s, s5_s)
```

```python
import functools
import math

import jax
import jax.numpy as jnp
from jax import lax
from jax.experimental import pallas as pl
from jax.experimental.pallas import tpu as pltpu

F32 = jnp.float32
BF16 = jnp.bfloat16
HIGHEST = lax.Precision.HIGHEST

DEPTH = 2
ALPHA = (2 * DEPTH) ** 0.25
LN_EPS = 1e-5
GN_EPS = 64e-5

S5_GROUP = 16
S5_STATE = 64
S5_GROUPS_PER_BLOCK = 8
RW_HEAD = 64
SHIFT_ROWS = 8
ROUTER_LANES = 128
WKV_CHUNK = 64
VMEM_LIMIT = 56 * 1024 * 1024
WKV_TERMS = {"nt": 1, "inv": 1, "out": 1, "state": 1}


def _bdot(a, b):
    return jnp.dot(a.astype(BF16), b.astype(BF16), preferred_element_type=F32)


def _fdot(a, b):
    return jnp.dot(a, b, precision=HIGHEST, preferred_element_type=F32)


def _fdot_nt(a, b):
    return lax.dot_general(a, b, (((1,), (1,)), ((), ())), precision=HIGHEST,
                           preferred_element_type=F32)


def _fdot_tn(a, b):
    return lax.dot_general(a, b, (((0,), (0,)), ((), ())), precision=HIGHEST,
                           preferred_element_type=F32)


def _sigmoid(x):
    return 1.0 / (1.0 + jnp.exp(-x))


def _layer_norm(h, g, b):
    mu = jnp.mean(h, axis=-1, keepdims=True)
    d = h - mu
    var = jnp.mean(d * d, axis=-1, keepdims=True)
    return d * lax.rsqrt(var + LN_EPS) * g + b


def _params(sem):
    return pltpu.CompilerParams(dimension_semantics=sem, vmem_limit_bytes=VMEM_LIMIT)


def _full(shape):
    n = len(shape)
    return pl.BlockSpec(shape, lambda *_: (0,) * n)


def _inproj_body(x_ref, w_ref, bg_ref, u_ref, zr_ref, gate_ref, *, s5w, rwc):
    xb = x_ref[...].astype(BF16)
    u_ref[...] = jnp.dot(xb, w_ref[:, :s5w], preferred_element_type=F32)
    zr_ref[...] = jnp.dot(xb, w_ref[:, s5w:s5w + rwc], preferred_element_type=F32)
    zg = jnp.dot(xb, w_ref[:, s5w + rwc:], preferred_element_type=F32)
    gate_ref[...] = _sigmoid(zg + bg_ref[...])


def _inproj(x, w_in, b_gate, s5w, rwc, tm):
    t, d = x.shape
    n_in = w_in.shape[1]
    ng = n_in - s5w - rwc
    row = lambda n: pl.BlockSpec((tm, n), lambda i: (i, 0))
    return pl.pallas_call(
        functools.partial(_inproj_body, s5w=s5w, rwc=rwc),
        grid=(t // tm,),
        in_specs=[row(d), _full((d, n_in)), _full((1, ng))],
        out_specs=[row(s5w), row(rwc), row(ng)],
        out_shape=[jax.ShapeDtypeStruct((t, s5w), F32), jax.ShapeDtypeStruct((t, rwc), F32),
                   jax.ShapeDtypeStruct((t, ng), F32)],
        compiler_params=_params(("parallel",)),
        name="inproj",
    )(x, w_in, b_gate)


def _s5_body(u_ref, h0r_ref, h0i_ref, abr_ref, abi_ref, wbu_ref, wy_ref, d_ref, wglu_ref, bglu_ref,
             y_ref, hlr_ref, hli_ref, bur, bui, hr, hi, str_, sti, *, tl, nblk, cb, sb):
    i = pl.program_id(1)

    @pl.when(i == 0)
    def _():
        str_[...] = h0r_ref[...]
        sti[...] = h0i_ref[...]

    u = u_ref[...]
    for j in range(nblk):
        bu = _bdot(u[:, cb * j:cb * (j + 1)], wbu_ref[j])
        bur[:, sb * j:sb * (j + 1)] = bu[:, :sb]
        bui[:, sb * j:sb * (j + 1)] = bu[:, sb:]

    ar = abr_ref[...]
    ai = abi_ref[...]

    def step(t, carry):
        pr, pi = carry
        nr = ar * pr - ai * pi + bur[pl.ds(t, 1), :]
        ni = ar * pi + ai * pr + bui[pl.ds(t, 1), :]
        hr[pl.ds(t, 1), :] = nr
        hi[pl.ds(t, 1), :] = ni
        return nr, ni

    lr, li = lax.fori_loop(0, tl, step, (str_[...], sti[...]), unroll=8)
    str_[...] = lr
    sti[...] = li
    hlr_ref[...] = lr
    hli_ref[...] = li

    ys = []
    for j in range(nblk):
        ys.append(_bdot(hr[:, sb * j:sb * (j + 1)], wy_ref[j, :sb, :])
                  + _bdot(hi[:, sb * j:sb * (j + 1)], wy_ref[j, sb:, :]))
    y = jnp.concatenate(ys, axis=1) + d_ref[...] * u
    y = jax.nn.gelu(y)
    y_ref[...] = y * _sigmoid(_bdot(y, wglu_ref[...]) + bglu_ref[...])


def _s5_mixer(u, h0r, h0i, prm, tl):
    b, l, w = u.shape
    ns = h0r.shape[-1]
    nblk = w // (S5_GROUP * S5_GROUPS_PER_BLOCK)
    cb = w // nblk
    sb = ns // nblk
    seq = lambda n: pl.BlockSpec((None, tl, n), lambda bi, i: (bi, i, 0))
    st = pl.BlockSpec((None, 1, ns), lambda bi, i: (bi, 0, 0))
    return pl.pallas_call(
        functools.partial(_s5_body, tl=tl, nblk=nblk, cb=cb, sb=sb),
        grid=(b, l // tl),
        in_specs=[seq(w), st, st, _full((1, ns)), _full((1, ns)), _full(prm["wbu"].shape),
                  _full(prm["wy"].shape), _full((1, w)), _full((w, w)), _full((1, w))],
        out_specs=[seq(w), st, st],
        out_shape=[jax.ShapeDtypeStruct((b, l, w), F32), jax.ShapeDtypeStruct((b, 1, ns), F32),
                   jax.ShapeDtypeStruct((b, 1, ns), F32)],
        scratch_shapes=[pltpu.VMEM((tl, ns), F32)] * 4 + [pltpu.VMEM((1, ns), F32)] * 2,
        compiler_params=_params(("parallel", "arbitrary")),
        name="s5_mixer",
    )(u, h0r, h0i, prm["abr"], prm["abi"], prm["wbu"], prm["wy"], prm["d"], prm["wglu"], prm["bglu"])


def _s5_discretize(a_re, a_im, log_dt, b_re, b_im, c_re, c_im):
    g, p = a_re.shape
    c = b_re.shape[-1]
    gb = S5_GROUPS_PER_BLOCK
    nblk = g // gb
    dt = jnp.exp(log_dt)[:, None]
    mag = jnp.exp(a_re * dt)
    ab_re = mag * jnp.cos(a_im * dt)
    ab_im = mag * jnp.sin(a_im * dt)
    den = a_re * a_re + a_im * a_im
    n_re = ab_re - 1.0
    k_re = (n_re * a_re + ab_im * a_im) / den
    k_im = (ab_im * a_re - n_re * a_im) / den
    bb_re = k_re[..., None] * b_re - k_im[..., None] * b_im
    bb_im = k_re[..., None] * b_im + k_im[..., None] * b_re
    eye = jnp.eye(gb, dtype=F32)

    def pack_in(bb):
        return jnp.einsum("jgpc,gh->jgchp", bb.reshape(nblk, gb, p, c), eye).reshape(nblk, gb * c, gb * p)

    def pack_out(cc):
        return jnp.einsum("jgcp,gh->jgphc", cc.reshape(nblk, gb, c, p), eye).reshape(nblk, gb * p, gb * c)

    wbu = jnp.concatenate([pack_in(bb_re), pack_in(bb_im)], axis=-1).astype(BF16)
    wy = jnp.concatenate([pack_out(c_re), -pack_out(c_im)], axis=1).astype(BF16)
    return ab_re.reshape(1, g * p), ab_im.reshape(1, g * p), wbu, wy


def _split(x, n):
    parts = []
    rem = x
    for j in range(n):
        p = rem.astype(BF16)
        parts.append(p)
        if j + 1 < n:
            rem = rem - p.astype(F32)
    return parts


_NN = (((1,), (0,)), ((), ()))
_NT = (((1,), (1,)), ((), ()))
_TN = (((0,), (0,)), ((), ()))


def _mdot(a_parts, b_parts, dims=_NN):
    order = max(len(a_parts), len(b_parts))
    acc = None
    for ia, pa in enumerate(a_parts):
        for ib, pb in enumerate(b_parts):
            if ia + ib < order:
                t = lax.dot_general(pa, pb, dims, preferred_element_type=F32)
                acc = t if acc is None else acc + t
    return acc


def _head_sum(x, hd):
    n = x.shape[-1]
    same = (lax.broadcasted_iota(jnp.int32, (n, n), 0) // hd
            == lax.broadcasted_iota(jnp.int32, (n, n), 1) // hd)
    return _mdot(_split(x, 2), [same.astype(BF16)])


def _rw_pre_body(*refs, w, dr, ar_, hd, has_vmix):
    (zr_ref, zp_ref, sh0_ref, mu_ref, w0_ref, w2_ref, a0_ref, a2_ref, g2_ref, kk_ref, ka_ref) = refs[:11]
    refs = refs[11:]
    if has_vmix:
        vf_ref, v0_ref, v1_ref, v2_ref = refs[:4]
        refs = refs[4:]
    r_ref, k_ref, v_ref, lw_ref, kkn_ref, bv_ref, g_ref = refs

    i = pl.program_id(1)
    z = zr_ref[...]
    last_prev = jnp.where(i == 0, sh0_ref[...], zp_ref[SHIFT_ROWS - 1:SHIFT_ROWS, :])
    row = lax.broadcasted_iota(jnp.int32, z.shape, 0)
    prev = jnp.where(row == 0, last_prev, pltpu.roll(z, 1, axis=0))
    zs = z + mu_ref[...] * (prev - z)

    r = zs[:, :w]
    k = zs[:, w:2 * w]
    v = zs[:, 2 * w:3 * w]
    zw = zs[:, 3 * w:3 * w + dr]
    za = zs[:, 3 * w + dr:3 * w + dr + ar_]
    zg = zs[:, 3 * w + dr + ar_:]

    wl = w0_ref[...] + _bdot(jnp.tanh(zw), w2_ref[...])
    nwl = -wl
    softplus = jnp.maximum(nwl, 0.0) + jnp.log(1.0 + jnp.exp(-jnp.abs(nwl)))
    lw_ref[...] = -jnp.exp(-softplus - 0.5)
    a = _sigmoid(a0_ref[...] + _bdot(za, a2_ref[...]))
    g_ref[...] = _bdot(_sigmoid(zg), g2_ref[...])
    if has_vmix:
        mix = _sigmoid(v0_ref[...] + _bdot(_bdot(v, v1_ref[...]), v2_ref[...]))
        v = v + (vf_ref[...] - v) * mix
    r_ref[...] = r
    v_ref[...] = v
    kk = k * kk_ref[...]
    nrm = jnp.sqrt(_head_sum(kk * kk, hd))
    kkn = kk / jnp.maximum(nrm, 1e-12)
    kkn_ref[...] = kkn
    bv_ref[...] = kkn * a
    k_ref[...] = k * (1.0 + (a - 1.0) * ka_ref[...])


def _rw_pre(zr, shift0, prm, v_first, tl):
    b, l, cols = zr.shape
    w = prm["w0"].shape[-1]
    dr = prm["w2"].shape[0]
    ar_ = prm["a2"].shape[0]
    has_vmix = v_first is not None
    nsub = tl // SHIFT_ROWS
    seq = lambda n: pl.BlockSpec((None, tl, n), lambda bi, i: (bi, i, 0))
    prev_spec = pl.BlockSpec((None, SHIFT_ROWS, cols),
                             lambda bi, i: (bi, jnp.maximum(i * nsub - 1, 0), 0))
    vec = lambda n: _full((1, n))
    ins = [zr, zr, shift0, prm["mu"], prm["w0"], prm["w2"], prm["a0"], prm["a2"], prm["g2"],
           prm["k_k"], prm["k_a"]]
    specs = [seq(cols), prev_spec, pl.BlockSpec((None, 1, cols), lambda bi, i: (bi, 0, 0)), vec(cols),
             vec(w), _full(prm["w2"].shape), vec(w), _full(prm["a2"].shape), _full(prm["g2"].shape),
             vec(w), vec(w)]
    if has_vmix:
        ins += [v_first, prm["v0"], prm["v1"], prm["v2"]]
        specs += [seq(w), vec(w), _full(prm["v1"].shape), _full(prm["v2"].shape)]
    return pl.pallas_call(
        functools.partial(_rw_pre_body, w=w, dr=dr, ar_=ar_, hd=RW_HEAD, has_vmix=has_vmix),
        grid=(b, l // tl),
        in_specs=specs,
        out_specs=[seq(w)] * 7,
        out_shape=[jax.ShapeDtypeStruct((b, l, w), F32)] * 7,
        compiler_params=_params(("parallel", "parallel")),
        name="rwkv_pre",
    )(*ins)


def _wkv_body(r_ref, k_ref, v_ref, lw_ref, kkn_ref, bv_ref, g_ref, rk_ref, lng_ref, lnb_ref, s0_ref,
              y_ref, sl_ref, s_scr, at_scr, rt_scr, kt_scr, bt_scr, kd_scr, bd_scr, ce_scr, y_scr,
              *, tl, c, nh, hd):
    i = pl.program_id(1)

    @pl.when(i == 0)
    def _():
        s_scr[...] = s0_ref[...]

    rowi = lax.broadcasted_iota(jnp.int32, (tl, tl), 0)
    coli = lax.broadcasted_iota(jnp.int32, (tl, tl), 1)
    same_chunk = (rowi // c) == (coli // c)
    lw = lw_ref[...]
    lw_parts = _split(lw, 3)
    cl = _mdot([(same_chunk & (coli <= rowi)).astype(BF16)], lw_parts)
    ce = _mdot([same_chunk.astype(BF16)], lw_parts)
    kkn = kkn_ref[...]
    bv = bv_ref[...]
    k = k_ref[...]
    einv = jnp.exp(-cl)
    to_end = jnp.exp(ce - cl)
    at_scr[...] = -kkn * jnp.exp(cl - lw)
    rt_scr[...] = r_ref[...] * jnp.exp(cl)
    kt_scr[...] = k * einv
    bt_scr[...] = bv * einv
    kd_scr[...] = k * to_end
    bd_scr[...] = bv * to_end
    ce_scr[...] = ce

    ri = lax.broadcasted_iota(jnp.int32, (c, c), 0)
    ci_ = lax.broadcasted_iota(jnp.int32, (c, c), 1)
    lower = ci_ <= ri
    strict = ci_ < ri
    levels = int(math.log2(c))
    sp = lambda x, cls: _split(x, WKV_TERMS[cls])

    def chunk(ci, carry):
        r0 = pl.multiple_of(ci * c, c)
        rows = pl.ds(r0, c)
        g_end = jnp.exp(ce_scr[pl.ds(r0, 1), :])
        heads = range(nh)
        hsl = [slice(h * hd, (h + 1) * hd) for h in heads]
        at = [at_scr[rows, s] for s in hsl]
        rt = [rt_scr[rows, s] for s in hsl]
        lhs = [sp(jnp.concatenate([at[h], rt[h]], axis=0), "nt") for h in heads]
        rhs = [sp(jnp.concatenate([bt_scr[rows, s], kt_scr[rows, s]], axis=0), "nt") for s in hsl]
        pm = [_mdot(lhs[h], rhs[h], _NT) for h in heads]
        n_ab = [jnp.where(strict, pm[h][:c, :c], 0.0) for h in heads]
        a_ak = [sp(jnp.where(strict, pm[h][:c, c:], 0.0), "out") for h in heads]
        a_rb = [sp(jnp.where(lower, pm[h][c:, :c], 0.0), "out") for h in heads]
        a_rk = [sp(jnp.where(lower, pm[h][c:, c:], 0.0), "out") for h in heads]
        v_p = [sp(v_ref[rows, s], "out") for s in hsl]

        x = [jnp.concatenate([at[h], _mdot(a_ak[h], v_p[h])], axis=1) for h in heads]
        rkv = [_mdot(a_rk[h], v_p[h]) for h in heads]
        pw = n_ab
        for lv in range(levels):
            pw_p = [sp(pw[h], "inv") for h in heads]
            x = [x[h] + _mdot(pw_p[h], sp(x[h], "inv")) for h in heads]
            if lv + 1 < levels:
                pw = [_mdot(pw_p[h], pw_p[h]) for h in heads]
        x_p = [sp(x[h], "out") for h in heads]
        qy = [jnp.concatenate([rt[h], rkv[h]], axis=1) + _mdot(a_rb[h], x_p[h]) for h in heads]
        mz = [_mdot(x_p[h], sp(bd_scr[rows, hsl[h]], "out"), _TN) for h in heads]
        zt = [mz[h][hd:, :] + _mdot(v_p[h], sp(kd_scr[rows, hsl[h]], "out"), _TN) for h in heads]

        s_prev = [s_scr[h] for h in heads]
        s_p = [sp(s_prev[h], "state") for h in heads]
        for h in heads:
            y_scr[rows, hsl[h]] = _mdot(sp(qy[h][:, :hd], "state"), s_p[h], _NT) + qy[h][:, hd:]
        for h in heads:
            s_scr[h] = s_prev[h] * g_end[:, hsl[h]] + _mdot(s_p[h], sp(mz[h][:hd, :], "state")) + zt[h]
        return carry

    lax.fori_loop(0, tl // c, chunk, 0)

    @pl.when(i == pl.num_programs(1) - 1)
    def _():
        sl_ref[...] = s_scr[...]

    y = y_scr[...]
    yc = y - _head_sum(y, hd) * (1.0 / hd)
    yv = _head_sum(yc * yc, hd) * (1.0 / hd)
    yn = yc * lax.rsqrt(yv + GN_EPS) * lng_ref[...] + lnb_ref[...]
    bonus = _head_sum(r_ref[...] * k_ref[...] * rk_ref[...], hd) * v_ref[...]
    y_ref[...] = (yn + bonus) * g_ref[...]


def _wkv(r, k, v, lw, kkn, bv, g, prm, s0, tl, c):
    b, l, w = r.shape
    nh, hd = s0.shape[1], s0.shape[2]
    seq = pl.BlockSpec((None, tl, w), lambda bi, i: (bi, i, 0))
    st = pl.BlockSpec((None, nh, hd, hd), lambda bi, i: (bi, 0, 0, 0))
    vec = _full((1, w))
    return pl.pallas_call(
        functools.partial(_wkv_body, tl=tl, c=c, nh=nh, hd=hd),
        grid=(b, l // tl),
        in_specs=[seq] * 7 + [vec, vec, vec, st],
        out_specs=[seq, st],
        out_shape=[jax.ShapeDtypeStruct((b, l, w), F32), jax.ShapeDtypeStruct(s0.shape, F32)],
        scratch_shapes=[pltpu.VMEM((nh, hd, hd), F32)] + [pltpu.VMEM((tl, w), F32)] * 8,
        compiler_params=_params(("parallel", "arbitrary")),
        name="wkv",
    )(r, k, v, lw, kkn, bv, g, prm["r_k"], prm["ln_g"], prm["ln_b"], s0)


def _merge_body(x_ref, ya_ref, yb_ref, gate_ref, p_ref, wpa_ref, wpb_ref, wo_ref, g1_ref, b1_ref,
                wpg_ref, bpg_ref, wple_ref, wr_ref, br_ref, x1_ref, base_ref, cmb_ref,
                *, d, ngroups, nper):
    gate = gate_ref[...]
    merged = gate[:, :d] * _bdot(ya_ref[...], wpa_ref[...]) + gate[:, d:] * _bdot(yb_ref[...], wpb_ref[...])
    x1 = _layer_norm(ALPHA * x_ref[...] + _bdot(merged, wo_ref[...]), g1_ref[...], b1_ref[...])
    x1_ref[...] = x1.astype(BF16)
    ple = _sigmoid(_bdot(x1, wpg_ref[...]) + bpg_ref[...]) * _bdot(p_ref[...], wple_ref[...])
    base_ref[...] = ALPHA * x1 + ple

    logits = _fdot(x1, wr_ref[...]) + br_ref[...]
    lane = lax.broadcasted_iota(jnp.int32, logits.shape, 1)
    neg = -jnp.inf
    big = logits.shape[1]
    first_of = lambda hit: jnp.min(jnp.where(hit, lane, big), axis=-1, keepdims=True)
    gl = jnp.where(lane < ngroups, logits, neg)
    gmax = jnp.max(gl, axis=-1, keepdims=True)
    g_sel = first_of(gl == gmax)
    g_w = 1.0 / jnp.sum(jnp.exp(gl - gmax), axis=-1, keepdims=True)
    lo = ngroups + nper * g_sel
    el = jnp.where((lane >= lo) & (lane < lo + nper), logits, neg)
    m1 = jnp.max(el, axis=-1, keepdims=True)
    i1 = first_of(el == m1)
    el2 = jnp.where(lane == i1, neg, el)
    m2 = jnp.max(el2, axis=-1, keepdims=True)
    i2 = first_of(el2 == m2)
    e2 = jnp.exp(m2 - m1)
    w1 = g_w / (1.0 + e2)
    w2 = g_w * e2 / (1.0 + e2)
    cmb_ref[...] = jnp.where(lane == i1, w1, 0.0) + jnp.where(lane == i2, w2, 0.0)


def _merge(x, ya, yb, gate, p, prm, tm):
    t, d = x.shape
    row = lambda n: pl.BlockSpec((tm, n), lambda i: (i, 0))
    ws = [prm["w_pa"], prm["w_pb"], prm["w_o"], prm["ln1_g"], prm["ln1_b"], prm["w_pg"], prm["b_pg"],
          prm["w_ple"], prm["w_router"], prm["b_router"]]
    return pl.pallas_call(
        functools.partial(_merge_body, d=d, ngroups=prm["ngroups"], nper=prm["nper"]),
        grid=(t // tm,),
        in_specs=[row(d), row(ya.shape[1]), row(yb.shape[1]), row(gate.shape[1]), row(p.shape[1])]
                 + [_full(a.shape) for a in ws],
        out_specs=[row(d), row(d), row(ROUTER_LANES)],
        out_shape=[jax.ShapeDtypeStruct((t, d), BF16), jax.ShapeDtypeStruct((t, d), F32),
                   jax.ShapeDtypeStruct((t, ROUTER_LANES), F32)],
        compiler_params=_params(("parallel",)),
        name="merge_ln1_router",
    )(x, ya, yb, gate, p, *ws)


def _moe_body(x1_ref, base_ref, cmb_ref, wg_ref, wu_ref, wd_ref, g2_ref, b2_ref, o_ref, acc, *, lane0):
    e = pl.program_id(1)

    @pl.when(e == 0)
    def _():
        acc[...] = jnp.zeros_like(acc)

    xb = x1_ref[...]
    hg = jnp.dot(xb, wg_ref[...], preferred_element_type=F32)
    hu = jnp.dot(xb, wu_ref[...], preferred_element_type=F32)
    cmb = cmb_ref[...]
    lane = lax.broadcasted_iota(jnp.int32, cmb.shape, 1)
    ce = jnp.sum(jnp.where(lane == e + lane0, cmb, 0.0), axis=-1, keepdims=True)
    act = hg * _sigmoid(hg) * hu * ce
    acc[...] += _bdot(act, wd_ref[...])

    @pl.when(e == pl.num_programs(1) - 1)
    def _():
        o_ref[...] = _layer_norm(base_ref[...] + acc[...], g2_ref[...], b2_ref[...])


def _moe(x1, base, cmb, prm, tm):
    t, d = base.shape
    ne, _, f = prm["w_gate"].shape
    row = lambda n: pl.BlockSpec((tm, n), lambda i, e: (i, 0))
    return pl.pallas_call(
        functools.partial(_moe_body, lane0=prm["ngroups"]),
        grid=(t // tm, ne),
        in_specs=[row(d), row(d), row(ROUTER_LANES),
                  pl.BlockSpec((None, d, f), lambda i, e: (e, 0, 0)),
                  pl.BlockSpec((None, d, f), lambda i, e: (e, 0, 0)),
                  pl.BlockSpec((None, f, d), lambda i, e: (e, 0, 0)),
                  pl.BlockSpec((1, d), lambda i, e: (0, 0)), pl.BlockSpec((1, d), lambda i, e: (0, 0))],
        out_specs=row(d),
        out_shape=jax.ShapeDtypeStruct((t, d), F32),
        scratch_shapes=[pltpu.VMEM((tm, d), F32)],
        compiler_params=_params(("parallel", "arbitrary")),
        name="moe_ln2",
    )(x1, base, cmb, prm["w_gate"], prm["w_up"], prm["w_down"], prm["ln2_g"], prm["ln2_b"])


def _tile(n, target):
    t = min(n, target)
    assert n % t == 0, (n, t)
    return t


def _layer(x, p, shift0, wkv0, s5r0, s5i0, v_first, lp):
    b, l, d = x.shape
    t = b * l
    s5w = lp["s5"]["d"].shape[-1]
    rwc = lp["rw"]["mu"].shape[-1]
    u, zr, gate = _inproj(x.reshape(t, d), lp["w_in"], lp["b_gate"], s5w, rwc, _tile(t, 256))
    ya, s5r, s5i = _s5_mixer(u.reshape(b, l, s5w), s5r0, s5i0, lp["s5"], _tile(l, 256))
    zr3 = zr.reshape(b, l, rwc)
    tl = _tile(l, 256)
    r, k, v, lw, kkn, bv, g = _rw_pre(zr3, shift0, lp["rw"], v_first, tl)
    yb, wkv_new = _wkv(r, k, v, lw, kkn, bv, g, lp["rw"], wkv0, tl, min(WKV_CHUNK, l))
    x1, base, cmb = _merge(x.reshape(t, d), ya.reshape(t, s5w), yb.reshape(t, -1), gate,
                           p.reshape(t, -1), lp["mrg"], _tile(t, 256))
    out = _moe(x1, base, cmb, lp["moe"], _tile(t, 512))
    shift_new = zr3[:, l - 1:, :]
    return out.reshape(b, l, d), shift_new, wkv_new, s5r, s5i, (v if v_first is None else v_first)


def _trunk(x, p, shift0, wkv0, s50, layers):
    b = x.shape[0]
    shifts, wkvs, s5s = [], [], []
    v_first = None
    for i, lp in enumerate(layers):
        ns = s50.shape[2] * s50.shape[3]
        s5r0 = s50[i, ..., 0].reshape(b, 1, ns)
        s5i0 = s50[i, ..., 1].reshape(b, 1, ns)
        x, sh, wk, s5r, s5i, v_first = _layer(x, p[i], shift0[i], wkv0[i], s5r0, s5i0, v_first, lp)
        shifts.append(sh)
        wkvs.append(wk)
        s5s.append(jnp.stack([s5r.reshape(s50.shape[1:4]), s5i.reshape(s50.shape[1:4])], axis=-1))
    return x, jnp.stack(shifts), jnp.stack(wkvs), jnp.stack(s5s)


def kernel(x_prompt, x_sample, state_shift, state_wkv, state_s5, p_prompt, p_sample, w_in, b_gate, s5_a_re, s5_a_im, s5_log_dt, s5_b_re, s5_b_im, s5_c_re, s5_c_im, s5_d, s5_w_glu, s5_b_glu, rw_mu, rw_w0, rw_w2, rw_a0, rw_a2, rw_g2, rw_v0, rw_v1, rw_v2, rw_k_k, rw_k_a, rw_r_k, rw_ln_g, rw_ln_b, w_pa, w_pb, w_o, ln1_g, ln1_b, w_rg, b_rg, w_re, b_re, w_gate, w_up, w_down, w_ple, w_pg, b_pg, ln2_g, ln2_b):
    depth, d = w_in.shape[0], w_in.shape[1]
    ngroups = w_rg.shape[-1]
    nper = w_re.shape[-1]
    nexp = ngroups * nper
    layers = []
    for i in range(depth):
        abr, abi, wbu, wy = _s5_discretize(s5_a_re[i], s5_a_im[i], s5_log_dt[i], s5_b_re[i], s5_b_im[i],
                                           s5_c_re[i], s5_c_im[i])
        s5 = dict(abr=abr, abi=abi, wbu=wbu, wy=wy, d=s5_d[i][None], wglu=s5_w_glu[i].astype(BF16),
                  bglu=s5_b_glu[i][None])
        rw = dict(mu=rw_mu[i][None], w0=rw_w0[i][None], w2=rw_w2[i].astype(BF16), a0=rw_a0[i][None],
                  a2=rw_a2[i].astype(BF16), g2=rw_g2[i].astype(BF16), k_k=rw_k_k[i][None],
                  k_a=rw_k_a[i][None], r_k=rw_r_k[i].reshape(1, -1), ln_g=rw_ln_g[i][None],
                  ln_b=rw_ln_b[i][None])
        if i > 0:
            rw.update(v0=rw_v0[i - 1][None], v1=rw_v1[i - 1].astype(BF16), v2=rw_v2[i - 1].astype(BF16))
        pad = ROUTER_LANES - ngroups - nexp
        w_router = jnp.concatenate([w_rg[i], w_re[i].reshape(d, nexp), jnp.zeros((d, pad), F32)], axis=1)
        b_router = jnp.concatenate([b_rg[i], b_re[i].reshape(nexp), jnp.zeros((pad,), F32)])[None]
        mrg = dict(w_pa=w_pa[i].astype(BF16), w_pb=w_pb[i].astype(BF16), w_o=w_o[i].astype(BF16),
                   ln1_g=ln1_g[i][None], ln1_b=ln1_b[i][None], w_pg=w_pg[i].astype(BF16),
                   b_pg=b_pg[i][None], w_ple=w_ple[i].astype(BF16), w_router=w_router,
                   b_router=b_router, ngroups=ngroups, nper=nper)
        moe = dict(w_gate=w_gate[i].astype(BF16), w_up=w_up[i].astype(BF16),
                   w_down=w_down[i].astype(BF16), ln2_g=ln2_g[i][None], ln2_b=ln2_b[i][None],
                   ngroups=ngroups)
        layers.append(dict(w_in=w_in[i].astype(BF16), b_gate=b_gate[i][None], s5=s5, rw=rw, mrg=mrg,
                           moe=moe))

    bp = x_prompt.shape[0]
    dt = x_prompt.dtype
    zero_shift = jnp.zeros((depth, bp) + state_shift.shape[2:], dt)
    zero_wkv = jnp.zeros((depth, bp) + state_wkv.shape[2:], dt)
    zero_s5 = jnp.zeros((depth, bp) + state_s5.shape[2:], dt)
    y_p, sh_p, wkv_p, s5_p = _trunk(x_prompt, p_prompt, zero_shift, zero_wkv, zero_s5, layers)
    y_s, sh_s, wkv_s, s5_s = _trunk(x_sample, p_sample, state_shift, state_wkv, state_s5, layers)
    return (y_p, y_s, sh_p, wkv_p, s5_p, sh_s, wkv_s, s5_s)
```

```python
import functools
import math

import jax
import jax.numpy as jnp
from jax import lax
from jax.experimental import pallas as pl
from jax.experimental.pallas import tpu as pltpu

F32 = jnp.float32
BF16 = jnp.bfloat16
HIGHEST = lax.Precision.HIGHEST

DEPTH = 2
ALPHA = (2 * DEPTH) ** 0.25
LN_EPS = 1e-5
GN_EPS = 64e-5

S5_GROUP = 16
S5_STATE = 64
S5_GROUPS_PER_BLOCK = 8
RW_HEAD = 64
SHIFT_ROWS = 8
ROUTER_LANES = 128
WKV_CHUNK = 64
VMEM_LIMIT = 56 * 1024 * 1024
WKV_TERMS = {"nt": 1, "inv": 1, "out": 1, "state": 1}
WKV_CHUNKS_PER_ITER = 4
MOE_F_SPLIT = 2
MOE_EXPERTS_PER_STEP = 2


def _bdot(a, b):
    return jnp.dot(a.astype(BF16), b.astype(BF16), preferred_element_type=F32)


def _fdot(a, b):
    return jnp.dot(a, b, precision=HIGHEST, preferred_element_type=F32)


def _fdot_nt(a, b):
    return lax.dot_general(a, b, (((1,), (1,)), ((), ())), precision=HIGHEST,
                           preferred_element_type=F32)


def _fdot_tn(a, b):
    return lax.dot_general(a, b, (((0,), (0,)), ((), ())), precision=HIGHEST,
                           preferred_element_type=F32)


def _sigmoid(x):
    return 1.0 / (1.0 + jnp.exp(-x))


def _layer_norm(h, g, b):
    mu = jnp.mean(h, axis=-1, keepdims=True)
    d = h - mu
    var = jnp.mean(d * d, axis=-1, keepdims=True)
    return d * lax.rsqrt(var + LN_EPS) * g + b


def _params(sem):
    return pltpu.CompilerParams(dimension_semantics=sem, vmem_limit_bytes=VMEM_LIMIT)


def _full(shape):
    n = len(shape)
    return pl.BlockSpec(shape, lambda *_: (0,) * n)


def _inproj_body(x_ref, w_ref, bg_ref, u_ref, zr_ref, gate_ref, *, s5w, rwc):
    xb = x_ref[...].astype(BF16)
    u_ref[...] = jnp.dot(xb, w_ref[:, :s5w], preferred_element_type=F32)
    zr_ref[...] = jnp.dot(xb, w_ref[:, s5w:s5w + rwc], preferred_element_type=F32)
    zg = jnp.dot(xb, w_ref[:, s5w + rwc:], preferred_element_type=F32)
    gate_ref[...] = _sigmoid(zg + bg_ref[...])


def _inproj(x, w_in, b_gate, s5w, rwc, tm):
    t, d = x.shape
    n_in = w_in.shape[1]
    ng = n_in - s5w - rwc
    row = lambda n: pl.BlockSpec((tm, n), lambda i: (i, 0))
    return pl.pallas_call(
        functools.partial(_inproj_body, s5w=s5w, rwc=rwc),
        grid=(t // tm,),
        in_specs=[row(d), _full((d, n_in)), _full((1, ng))],
        out_specs=[row(s5w), row(rwc), row(ng)],
        out_shape=[jax.ShapeDtypeStruct((t, s5w), F32), jax.ShapeDtypeStruct((t, rwc), F32),
                   jax.ShapeDtypeStruct((t, ng), F32)],
        compiler_params=_params(("parallel",)),
        name="inproj",
    )(x, w_in, b_gate)


def _s5_body(u_ref, h0r_ref, h0i_ref, abr_ref, abi_ref, wbu_ref, wy_ref, d_ref, wglu_ref, bglu_ref,
             y_ref, hlr_ref, hli_ref, bur, bui, hr, hi, str_, sti, *, tl, nblk, cb, sb):
    i = pl.program_id(1)

    @pl.when(i == 0)
    def _():
        str_[...] = h0r_ref[...]
        sti[...] = h0i_ref[...]

    u = u_ref[...]
    for j in range(nblk):
        bu = _bdot(u[:, cb * j:cb * (j + 1)], wbu_ref[j])
        bur[:, sb * j:sb * (j + 1)] = bu[:, :sb]
        bui[:, sb * j:sb * (j + 1)] = bu[:, sb:]

    ar = abr_ref[...]
    ai = abi_ref[...]

    def step(t, carry):
        pr, pi = carry
        nr = ar * pr - ai * pi + bur[pl.ds(t, 1), :]
        ni = ar * pi + ai * pr + bui[pl.ds(t, 1), :]
        hr[pl.ds(t, 1), :] = nr
        hi[pl.ds(t, 1), :] = ni
        return nr, ni

    lr, li = lax.fori_loop(0, tl, step, (str_[...], sti[...]), unroll=8)
    str_[...] = lr
    sti[...] = li
    hlr_ref[...] = lr
    hli_ref[...] = li

    ys = []
    for j in range(nblk):
        ys.append(_bdot(hr[:, sb * j:sb * (j + 1)], wy_ref[j, :sb, :])
                  + _bdot(hi[:, sb * j:sb * (j + 1)], wy_ref[j, sb:, :]))
    y = jnp.concatenate(ys, axis=1) + d_ref[...] * u
    y = jax.nn.gelu(y)
    y_ref[...] = y * _sigmoid(_bdot(y, wglu_ref[...]) + bglu_ref[...])


def _s5_mixer(u, h0r, h0i, prm, tl):
    b, l, w = u.shape
    ns = h0r.shape[-1]
    nblk = w // (S5_GROUP * S5_GROUPS_PER_BLOCK)
    cb = w // nblk
    sb = ns // nblk
    seq = lambda n: pl.BlockSpec((None, tl, n), lambda bi, i: (bi, i, 0))
    st = pl.BlockSpec((None, 1, ns), lambda bi, i: (bi, 0, 0))
    return pl.pallas_call(
        functools.partial(_s5_body, tl=tl, nblk=nblk, cb=cb, sb=sb),
        grid=(b, l // tl),
        in_specs=[seq(w), st, st, _full((1, ns)), _full((1, ns)), _full(prm["wbu"].shape),
                  _full(prm["wy"].shape), _full((1, w)), _full((w, w)), _full((1, w))],
        out_specs=[seq(w), st, st],
        out_shape=[jax.ShapeDtypeStruct((b, l, w), F32), jax.ShapeDtypeStruct((b, 1, ns), F32),
                   jax.ShapeDtypeStruct((b, 1, ns), F32)],
        scratch_shapes=[pltpu.VMEM((tl, ns), F32)] * 4 + [pltpu.VMEM((1, ns), F32)] * 2,
        compiler_params=_params(("parallel", "arbitrary")),
        name="s5_mixer",
    )(u, h0r, h0i, prm["abr"], prm["abi"], prm["wbu"], prm["wy"], prm["d"], prm["wglu"], prm["bglu"])


def _s5_discretize(a_re, a_im, log_dt, b_re, b_im, c_re, c_im):
    g, p = a_re.shape
    c = b_re.shape[-1]
    gb = S5_GROUPS_PER_BLOCK
    nblk = g // gb
    dt = jnp.exp(log_dt)[:, None]
    mag = jnp.exp(a_re * dt)
    ab_re = mag * jnp.cos(a_im * dt)
    ab_im = mag * jnp.sin(a_im * dt)
    den = a_re * a_re + a_im * a_im
    n_re = ab_re - 1.0
    k_re = (n_re * a_re + ab_im * a_im) / den
    k_im = (ab_im * a_re - n_re * a_im) / den
    bb_re = k_re[..., None] * b_re - k_im[..., None] * b_im
    bb_im = k_re[..., None] * b_im + k_im[..., None] * b_re
    eye = jnp.eye(gb, dtype=F32)

    def pack_in(bb):
        return jnp.einsum("jgpc,gh->jgchp", bb.reshape(nblk, gb, p, c), eye).reshape(nblk, gb * c, gb * p)

    def pack_out(cc):
        return jnp.einsum("jgcp,gh->jgphc", cc.reshape(nblk, gb, c, p), eye).reshape(nblk, gb * p, gb * c)

    wbu = jnp.concatenate([pack_in(bb_re), pack_in(bb_im)], axis=-1).astype(BF16)
    wy = jnp.concatenate([pack_out(c_re), -pack_out(c_im)], axis=1).astype(BF16)
    return ab_re.reshape(1, g * p), ab_im.reshape(1, g * p), wbu, wy


def _split(x, n):
    parts = []
    rem = x
    for j in range(n):
        p = rem.astype(BF16)
        parts.append(p)
        if j + 1 < n:
            rem = rem - p.astype(F32)
    return parts


_NN = (((1,), (0,)), ((), ()))
_NT = (((1,), (1,)), ((), ()))
_TN = (((0,), (0,)), ((), ()))


def _mdot(a_parts, b_parts, dims=_NN):
    order = max(len(a_parts), len(b_parts))
    acc = None
    for ia, pa in enumerate(a_parts):
        for ib, pb in enumerate(b_parts):
            if ia + ib < order:
                t = lax.dot_general(pa, pb, dims, preferred_element_type=F32)
                acc = t if acc is None else acc + t
    return acc


def _head_sum(x, hd, terms=1):
    n = x.shape[-1]
    same = (lax.broadcasted_iota(jnp.int32, (n, n), 0) // hd
            == lax.broadcasted_iota(jnp.int32, (n, n), 1) // hd)
    return _mdot(_split(x, terms), [same.astype(BF16)])


def _rw_pre_body(*refs, w, dr, ar_, hd, has_vmix):
    (zr_ref, zp_ref, sh0_ref, mu_ref, w0_ref, w2_ref, a0_ref, a2_ref, g2_ref, kk_ref, ka_ref) = refs[:11]
    refs = refs[11:]
    if has_vmix:
        vf_ref, v0_ref, v1_ref, v2_ref = refs[:4]
        refs = refs[4:]
    r_ref, k_ref, v_ref, lw_ref, kkn_ref, bv_ref, g_ref = refs

    i = pl.program_id(1)
    z = zr_ref[...]
    last_prev = jnp.where(i == 0, sh0_ref[...], zp_ref[SHIFT_ROWS - 1:SHIFT_ROWS, :])
    row = lax.broadcasted_iota(jnp.int32, z.shape, 0)
    prev = jnp.where(row == 0, last_prev, pltpu.roll(z, 1, axis=0))
    zs = z + mu_ref[...] * (prev - z)

    r = zs[:, :w]
    k = zs[:, w:2 * w]
    v = zs[:, 2 * w:3 * w]
    zw = zs[:, 3 * w:3 * w + dr]
    za = zs[:, 3 * w + dr:3 * w + dr + ar_]
    zg = zs[:, 3 * w + dr + ar_:]

    wl = w0_ref[...] + _bdot(jnp.tanh(zw), w2_ref[...])
    nwl = -wl
    softplus = jnp.maximum(nwl, 0.0) + jnp.log(1.0 + jnp.exp(-jnp.abs(nwl)))
    lw_ref[...] = -jnp.exp(-softplus - 0.5)
    a = _sigmoid(a0_ref[...] + _bdot(za, a2_ref[...]))
    g_ref[...] = _bdot(_sigmoid(zg), g2_ref[...])
    if has_vmix:
        mix = _sigmoid(v0_ref[...] + _bdot(_bdot(v, v1_ref[...]), v2_ref[...]))
        v = v + (vf_ref[...] - v) * mix
    r_ref[...] = r
    v_ref[...] = v
    kk = k * kk_ref[...]
    nrm = jnp.sqrt(_head_sum(kk * kk, hd))
    kkn = kk / jnp.maximum(nrm, 1e-12)
    kkn_ref[...] = kkn
    bv_ref[...] = kkn * a
    k_ref[...] = k * (1.0 + (a - 1.0) * ka_ref[...])


def _rw_pre(zr, shift0, prm, v_first, tl):
    b, l, cols = zr.shape
    w = prm["w0"].shape[-1]
    dr = prm["w2"].shape[0]
    ar_ = prm["a2"].shape[0]
    has_vmix = v_first is not None
    nsub = tl // SHIFT_ROWS
    seq = lambda n: pl.BlockSpec((None, tl, n), lambda bi, i: (bi, i, 0))
    prev_spec = pl.BlockSpec((None, SHIFT_ROWS, cols),
                             lambda bi, i: (bi, jnp.maximum(i * nsub - 1, 0), 0))
    vec = lambda n: _full((1, n))
    ins = [zr, zr, shift0, prm["mu"], prm["w0"], prm["w2"], prm["a0"], prm["a2"], prm["g2"],
           prm["k_k"], prm["k_a"]]
    specs = [seq(cols), prev_spec, pl.BlockSpec((None, 1, cols), lambda bi, i: (bi, 0, 0)), vec(cols),
             vec(w), _full(prm["w2"].shape), vec(w), _full(prm["a2"].shape), _full(prm["g2"].shape),
             vec(w), vec(w)]
    if has_vmix:
        ins += [v_first, prm["v0"], prm["v1"], prm["v2"]]
        specs += [seq(w), vec(w), _full(prm["v1"].shape), _full(prm["v2"].shape)]
    return pl.pallas_call(
        functools.partial(_rw_pre_body, w=w, dr=dr, ar_=ar_, hd=RW_HEAD, has_vmix=has_vmix),
        grid=(b, l // tl),
        in_specs=specs,
        out_specs=[seq(w)] * 7,
        out_shape=[jax.ShapeDtypeStruct((b, l, w), F32)] * 7,
        compiler_params=_params(("parallel", "parallel")),
        name="rwkv_pre",
    )(*ins)


def _wkv_body(r_ref, k_ref, v_ref, lw_ref, kkn_ref, bv_ref, g_ref, rk_ref, lng_ref, lnb_ref, s0_ref,
              y_ref, sl_ref, s_scr, at_scr, rt_scr, kt_scr, bt_scr, kd_scr, bd_scr, ce_scr, y_scr,
              *, tl, c, cpi, nh, hd):
    i = pl.program_id(1)

    @pl.when(i == 0)
    def _():
        s_scr[...] = s0_ref[...]

    rowi = lax.broadcasted_iota(jnp.int32, (tl, tl), 0)
    coli = lax.broadcasted_iota(jnp.int32, (tl, tl), 1)
    same_chunk = (rowi // c) == (coli // c)
    lw = lw_ref[...]
    cl = _mdot([(same_chunk & (coli <= rowi)).astype(BF16)], _split(lw, 3))
    ce = jnp.concatenate([jnp.broadcast_to(cl[j * c + c - 1:(j + 1) * c, :], (c, cl.shape[1]))
                          for j in range(tl // c)], axis=0)
    kkn = kkn_ref[...]
    bv = bv_ref[...]
    k = k_ref[...]
    einv = jnp.exp(-cl)
    to_end = jnp.exp(ce - cl)
    at_scr[...] = -kkn * jnp.exp(cl - lw)
    rt_scr[...] = r_ref[...] * jnp.exp(cl)
    kt_scr[...] = k * einv
    bt_scr[...] = bv * einv
    kd_scr[...] = k * to_end
    bd_scr[...] = bv * to_end
    ce_scr[...] = ce

    ri = lax.broadcasted_iota(jnp.int32, (c, c), 0)
    ci_ = lax.broadcasted_iota(jnp.int32, (c, c), 1)
    lower = ci_ <= ri
    strict = ci_ < ri
    levels = int(math.log2(c))
    sp = lambda x, cls: _split(x, WKV_TERMS[cls])

    def chunk_group(gi, carry):
        base = pl.multiple_of(gi * (cpi * c), cpi * c)
        units = [(cc, h) for cc in range(cpi) for h in range(nh)]
        rows = [pl.ds(base + cc * c, c) for cc, _ in units]
        hsl = [slice(h * hd, (h + 1) * hd) for _, h in units]
        un = range(len(units))
        at = [at_scr[rows[u], hsl[u]] for u in un]
        rt = [rt_scr[rows[u], hsl[u]] for u in un]
        lhs = [sp(jnp.concatenate([at[u], rt[u]], axis=0), "nt") for u in un]
        rhs = [sp(jnp.concatenate([bt_scr[rows[u], hsl[u]], kt_scr[rows[u], hsl[u]]], axis=0), "nt")
               for u in un]
        pm = [_mdot(lhs[u], rhs[u], _NT) for u in un]
        n_ab = [jnp.where(strict, pm[u][:c, :c], 0.0) for u in un]
        a_ak = [sp(jnp.where(strict, pm[u][:c, c:], 0.0), "out") for u in un]
        a_rb = [sp(jnp.where(lower, pm[u][c:, :c], 0.0), "out") for u in un]
        a_rk = [sp(jnp.where(lower, pm[u][c:, c:], 0.0), "out") for u in un]
        v_p = [sp(v_ref[rows[u], hsl[u]], "out") for u in un]

        x = [jnp.concatenate([at[u], _mdot(a_ak[u], v_p[u])], axis=1) for u in un]
        rkv = [_mdot(a_rk[u], v_p[u]) for u in un]
        pw = n_ab
        for lv in range(levels):
            pw_p = [sp(pw[u], "inv") for u in un]
            x = [x[u] + _mdot(pw_p[u], sp(x[u], "inv")) for u in un]
            if lv + 1 < levels:
                pw = [_mdot(pw_p[u], pw_p[u]) for u in un]
        x_p = [sp(x[u], "out") for u in un]
        qy = [jnp.concatenate([rt[u], rkv[u]], axis=1) + _mdot(a_rb[u], x_p[u]) for u in un]
        mz = [_mdot(x_p[u], sp(bd_scr[rows[u], hsl[u]], "out"), _TN) for u in un]
        zt = [mz[u][hd:, :] + _mdot(v_p[u], sp(kd_scr[rows[u], hsl[u]], "out"), _TN) for u in un]

        s_cur = [s_scr[h] for h in range(nh)]
        for cc in range(cpi):
            g_end = jnp.exp(ce_scr[pl.ds(base + cc * c, 1), :])
            s_p = [sp(s_cur[h], "state") for h in range(nh)]
            for h in range(nh):
                u = cc * nh + h
                y_scr[rows[u], hsl[u]] = _mdot(sp(qy[u][:, :hd], "state"), s_p[h], _NT) + qy[u][:, hd:]
            s_cur = [s_cur[h] * g_end[:, hsl[h]] + _mdot(s_p[h], sp(mz[cc * nh + h][:hd, :], "state"))
                     + zt[cc * nh + h] for h in range(nh)]
        for h in range(nh):
            s_scr[h] = s_cur[h]
        return carry

    lax.fori_loop(0, tl // (cpi * c), chunk_group, 0)

    @pl.when(i == pl.num_programs(1) - 1)
    def _():
        sl_ref[...] = s_scr[...]

    y = y_scr[...]
    yc = y - _head_sum(y, hd) * (1.0 / hd)
    yv = _head_sum(yc * yc, hd) * (1.0 / hd)
    yn = yc * lax.rsqrt(yv + GN_EPS) * lng_ref[...] + lnb_ref[...]
    bonus = _head_sum(r_ref[...] * k_ref[...] * rk_ref[...], hd) * v_ref[...]
    y_ref[...] = (yn + bonus) * g_ref[...]


def _wkv(r, k, v, lw, kkn, bv, g, prm, s0, tl, c):
    b, l, w = r.shape
    nh, hd = s0.shape[1], s0.shape[2]
    seq = pl.BlockSpec((None, tl, w), lambda bi, i: (bi, i, 0))
    st = pl.BlockSpec((None, nh, hd, hd), lambda bi, i: (bi, 0, 0, 0))
    vec = _full((1, w))
    return pl.pallas_call(
        functools.partial(_wkv_body, tl=tl, c=c, cpi=min(WKV_CHUNKS_PER_ITER, tl // c), nh=nh, hd=hd),
        grid=(b, l // tl),
        in_specs=[seq] * 7 + [vec, vec, vec, st],
        out_specs=[seq, st],
        out_shape=[jax.ShapeDtypeStruct((b, l, w), F32), jax.ShapeDtypeStruct(s0.shape, F32)],
        scratch_shapes=[pltpu.VMEM((nh, hd, hd), F32)] + [pltpu.VMEM((tl, w), F32)] * 8,
        compiler_params=_params(("parallel", "arbitrary")),
        name="wkv",
    )(r, k, v, lw, kkn, bv, g, prm["r_k"], prm["ln_g"], prm["ln_b"], s0)


def _merge_body(x_ref, ya_ref, yb_ref, gate_ref, p_ref, wpa_ref, wpb_ref, wo_ref, g1_ref, b1_ref,
                wpg_ref, bpg_ref, wple_ref, wr_ref, br_ref, x1_ref, base_ref, cmb_ref,
                *, d, ngroups, nper):
    gate = gate_ref[...]
    merged = gate[:, :d] * _bdot(ya_ref[...], wpa_ref[...]) + gate[:, d:] * _bdot(yb_ref[...], wpb_ref[...])
    x1 = _layer_norm(ALPHA * x_ref[...] + _bdot(merged, wo_ref[...]), g1_ref[...], b1_ref[...])
    x1_ref[...] = x1.astype(BF16)
    ple = _sigmoid(_bdot(x1, wpg_ref[...]) + bpg_ref[...]) * _bdot(p_ref[...], wple_ref[...])
    base_ref[...] = ALPHA * x1 + ple

    logits = _mdot(_split(x1, 2), _split(wr_ref[...], 2)) + br_ref[...]
    lane = lax.broadcasted_iota(jnp.int32, logits.shape, 1)
    neg = -jnp.inf
    big = logits.shape[1]
    first_of = lambda hit: jnp.min(jnp.where(hit, lane, big), axis=-1, keepdims=True)
    gl = jnp.where(lane < ngroups, logits, neg)
    gmax = jnp.max(gl, axis=-1, keepdims=True)
    g_sel = first_of(gl == gmax)
    g_w = 1.0 / jnp.sum(jnp.exp(gl - gmax), axis=-1, keepdims=True)
    lo = ngroups + nper * g_sel
    el = jnp.where((lane >= lo) & (lane < lo + nper), logits, neg)
    m1 = jnp.max(el, axis=-1, keepdims=True)
    i1 = first_of(el == m1)
    el2 = jnp.where(lane == i1, neg, el)
    m2 = jnp.max(el2, axis=-1, keepdims=True)
    i2 = first_of(el2 == m2)
    e2 = jnp.exp(m2 - m1)
    w1 = g_w / (1.0 + e2)
    w2 = g_w * e2 / (1.0 + e2)
    cmb_ref[...] = jnp.where(lane == i1, w1, 0.0) + jnp.where(lane == i2, w2, 0.0)


def _merge(x, ya, yb, gate, p, prm, tm):
    t, d = x.shape
    row = lambda n: pl.BlockSpec((tm, n), lambda i: (i, 0))
    ws = [prm["w_pa"], prm["w_pb"], prm["w_o"], prm["ln1_g"], prm["ln1_b"], prm["w_pg"], prm["b_pg"],
          prm["w_ple"], prm["w_router"], prm["b_router"]]
    return pl.pallas_call(
        functools.partial(_merge_body, d=d, ngroups=prm["ngroups"], nper=prm["nper"]),
        grid=(t // tm,),
        in_specs=[row(d), row(ya.shape[1]), row(yb.shape[1]), row(gate.shape[1]), row(p.shape[1])]
                 + [_full(a.shape) for a in ws],
        out_specs=[row(d), row(d), row(ROUTER_LANES)],
        out_shape=[jax.ShapeDtypeStruct((t, d), BF16), jax.ShapeDtypeStruct((t, d), F32),
                   jax.ShapeDtypeStruct((t, ROUTER_LANES), F32)],
        compiler_params=_params(("parallel",)),
        name="merge_ln1_router",
    )(x, ya, yb, gate, p, *ws)


def _moe_body(x1_ref, base_ref, cmb_ref, wg_ref, wu_ref, wd_ref, g2_ref, b2_ref, o_ref, acc, *, lane0):
    e = pl.program_id(1)

    @pl.when(e == 0)
    def _():
        acc[...] = jnp.zeros_like(acc)

    xb = x1_ref[...]
    cmb = cmb_ref[...]
    lane = lax.broadcasted_iota(jnp.int32, cmb.shape, 1)
    eps, _, f = wg_ref.shape
    fs = f // MOE_F_SPLIT
    part = None
    for ee in range(eps):
        ce = jnp.sum(jnp.where(lane == e * eps + ee + lane0, cmb, 0.0), axis=-1, keepdims=True)
        for j in range(MOE_F_SPLIT):
            cols = slice(j * fs, (j + 1) * fs)
            hg = jnp.dot(xb, wg_ref[ee, :, cols], preferred_element_type=F32)
            hu = jnp.dot(xb, wu_ref[ee, :, cols], preferred_element_type=F32)
            act = hg * _sigmoid(hg) * hu * ce
            t = _bdot(act, wd_ref[ee, cols, :])
            part = t if part is None else part + t
    acc[...] += part

    @pl.when(e == pl.num_programs(1) - 1)
    def _():
        o_ref[...] = _layer_norm(base_ref[...] + acc[...], g2_ref[...], b2_ref[...])


def _moe(x1, base, cmb, prm, tm):
    t, d = base.shape
    ne, _, f = prm["w_gate"].shape
    row = lambda n: pl.BlockSpec((tm, n), lambda i, e: (i, 0))
    eps = MOE_EXPERTS_PER_STEP
    return pl.pallas_call(
        functools.partial(_moe_body, lane0=prm["ngroups"]),
        grid=(t // tm, ne // eps),
        in_specs=[row(d), row(d), row(ROUTER_LANES),
                  pl.BlockSpec((eps, d, f), lambda i, e: (e, 0, 0)),
                  pl.BlockSpec((eps, d, f), lambda i, e: (e, 0, 0)),
                  pl.BlockSpec((eps, f, d), lambda i, e: (e, 0, 0)),
                  pl.BlockSpec((1, d), lambda i, e: (0, 0)), pl.BlockSpec((1, d), lambda i, e: (0, 0))],
        out_specs=row(d),
        out_shape=jax.ShapeDtypeStruct((t, d), F32),
        scratch_shapes=[pltpu.VMEM((tm, d), F32)],
        compiler_params=_params(("parallel", "arbitrary")),
        name="moe_ln2",
    )(x1, base, cmb, prm["w_gate"], prm["w_up"], prm["w_down"], prm["ln2_g"], prm["ln2_b"])


def _tile(n, target):
    t = min(n, target)
    assert n % t == 0, (n, t)
    return t


def _layer(x, p, shift0, wkv0, s5r0, s5i0, v_first, lp):
    b, l, d = x.shape
    t = b * l
    s5w = lp["s5"]["d"].shape[-1]
    rwc = lp["rw"]["mu"].shape[-1]
    u, zr, gate = _inproj(x.reshape(t, d), lp["w_in"], lp["b_gate"], s5w, rwc, _tile(t, 256))
    ya, s5r, s5i = _s5_mixer(u.reshape(b, l, s5w), s5r0, s5i0, lp["s5"], _tile(l, 256))
    zr3 = zr.reshape(b, l, rwc)
    tl = _tile(l, 256)
    r, k, v, lw, kkn, bv, g = _rw_pre(zr3, shift0, lp["rw"], v_first, tl)
    yb, wkv_new = _wkv(r, k, v, lw, kkn, bv, g, lp["rw"], wkv0, tl, min(WKV_CHUNK, l))
    x1, base, cmb = _merge(x.reshape(t, d), ya.reshape(t, s5w), yb.reshape(t, -1), gate,
                           p.reshape(t, -1), lp["mrg"], _tile(t, 256))
    out = _moe(x1, base, cmb, lp["moe"], _tile(t, 1024))
    shift_new = zr3[:, l - 1:, :]
    return out.reshape(b, l, d), shift_new, wkv_new, s5r, s5i, (v if v_first is None else v_first)


def _trunk(x, p, shift0, wkv0, s50, layers):
    b = x.shape[0]
    shifts, wkvs, s5s = [], [], []
    v_first = None
    for i, lp in enumerate(layers):
        ns = s50.shape[2] * s50.shape[3]
        s5r0 = s50[i, ..., 0].reshape(b, 1, ns)
        s5i0 = s50[i, ..., 1].reshape(b, 1, ns)
        x, sh, wk, s5r, s5i, v_first = _layer(x, p[i], shift0[i], wkv0[i], s5r0, s5i0, v_first, lp)
        shifts.append(sh)
        wkvs.append(wk)
        s5s.append(jnp.stack([s5r.reshape(s50.shape[1:4]), s5i.reshape(s50.shape[1:4])], axis=-1))
    return x, jnp.stack(shifts), jnp.stack(wkvs), jnp.stack(s5s)


def kernel(x_prompt, x_sample, state_shift, state_wkv, state_s5, p_prompt, p_sample, w_in, b_gate, s5_a_re, s5_a_im, s5_log_dt, s5_b_re, s5_b_im, s5_c_re, s5_c_im, s5_d, s5_w_glu, s5_b_glu, rw_mu, rw_w0, rw_w2, rw_a0, rw_a2, rw_g2, rw_v0, rw_v1, rw_v2, rw_k_k, rw_k_a, rw_r_k, rw_ln_g, rw_ln_b, w_pa, w_pb, w_o, ln1_g, ln1_b, w_rg, b_rg, w_re, b_re, w_gate, w_up, w_down, w_ple, w_pg, b_pg, ln2_g, ln2_b):
    depth, d = w_in.shape[0], w_in.shape[1]
    ngroups = w_rg.shape[-1]
    nper = w_re.shape[-1]
    nexp = ngroups * nper
    layers = []
    for i in range(depth):
        abr, abi, wbu, wy = _s5_discretize(s5_a_re[i], s5_a_im[i], s5_log_dt[i], s5_b_re[i], s5_b_im[i],
                                           s5_c_re[i], s5_c_im[i])
        s5 = dict(abr=abr, abi=abi, wbu=wbu, wy=wy, d=s5_d[i][None], wglu=s5_w_glu[i].astype(BF16),
                  bglu=s5_b_glu[i][None])
        rw = dict(mu=rw_mu[i][None], w0=rw_w0[i][None], w2=rw_w2[i].astype(BF16), a0=rw_a0[i][None],
                  a2=rw_a2[i].astype(BF16), g2=rw_g2[i].astype(BF16), k_k=rw_k_k[i][None],
                  k_a=rw_k_a[i][None], r_k=rw_r_k[i].reshape(1, -1), ln_g=rw_ln_g[i][None],
                  ln_b=rw_ln_b[i][None])
        if i > 0:
            rw.update(v0=rw_v0[i - 1][None], v1=rw_v1[i - 1].astype(BF16), v2=rw_v2[i - 1].astype(BF16))
        pad = ROUTER_LANES - ngroups - nexp
        w_router = jnp.concatenate([w_rg[i], w_re[i].reshape(d, nexp), jnp.zeros((d, pad), F32)], axis=1)
        b_router = jnp.concatenate([b_rg[i], b_re[i].reshape(nexp), jnp.zeros((pad,), F32)])[None]
        mrg = dict(w_pa=w_pa[i].astype(BF16), w_pb=w_pb[i].astype(BF16), w_o=w_o[i].astype(BF16),
                   ln1_g=ln1_g[i][None], ln1_b=ln1_b[i][None], w_pg=w_pg[i].astype(BF16),
                   b_pg=b_pg[i][None], w_ple=w_ple[i].astype(BF16), w_router=w_router,
                   b_router=b_router, ngroups=ngroups, nper=nper)
        moe = dict(w_gate=w_gate[i].astype(BF16), w_up=w_up[i].astype(BF16),
                   w_down=w_down[i].astype(BF16), ln2_g=ln2_g[i][None], ln2_b=ln2_b[i][None],
                   ngroups=ngroups)
        layers.append(dict(w_in=w_in[i].astype(BF16), b_gate=b_gate[i][None], s5=s5, rw=rw, mrg=mrg,
                           moe=moe))

    bp = x_prompt.shape[0]
    dt = x_prompt.dtype
    zero_shift = jnp.zeros((depth, bp) + state_shift.shape[2:], dt)
    zero_wkv = jnp.zeros((depth, bp) + state_wkv.shape[2:], dt)
    zero_s5 = jnp.zeros((depth, bp) + state_s5.shape[2:], dt)
    y_p, sh_p, wkv_p, s5_p = _trunk(x_prompt, p_prompt, zero_shift, zero_wkv, zero_s5, layers)
    y_s, sh_s, wkv_s, s5_s = _trunk(x_sample, p_sample, state_shift, state_wkv, state_s5, layers)
    return (y_p, y_s, sh_p, wkv_p, s5_p, sh_s, wkv_s, s5_s)
```

```python
import functools
import math

import jax
import jax.numpy as jnp
from jax import lax
from jax.experimental import pallas as pl
from jax.experimental.pallas import tpu as pltpu

F32 = jnp.float32
BF16 = jnp.bfloat16
HIGHEST = lax.Precision.HIGHEST

DEPTH = 2
ALPHA = (2 * DEPTH) ** 0.25
LN_EPS = 1e-5
GN_EPS = 64e-5

S5_GROUP = 16
S5_STATE = 64
S5_GROUPS_PER_BLOCK = 8
RW_HEAD = 64
SHIFT_ROWS = 8
ROUTER_LANES = 128
WKV_CHUNK = 64
VMEM_LIMIT = 56 * 1024 * 1024
WKV_TERMS = {"nt": 1, "inv": 1, "out": 1, "state": 1}
WKV_CHUNKS_PER_ITER = 4
MOE_EXPERTS_PER_STEP = 2
MOE_BLOCK_ROWS = 128


def _bdot(a, b):
    return jnp.dot(a.astype(BF16), b.astype(BF16), preferred_element_type=F32)


def _fdot(a, b):
    return jnp.dot(a, b, precision=HIGHEST, preferred_element_type=F32)


def _fdot_nt(a, b):
    return lax.dot_general(a, b, (((1,), (1,)), ((), ())), precision=HIGHEST,
                           preferred_element_type=F32)


def _fdot_tn(a, b):
    return lax.dot_general(a, b, (((0,), (0,)), ((), ())), precision=HIGHEST,
                           preferred_element_type=F32)


def _sigmoid(x):
    return 1.0 / (1.0 + jnp.exp(-x))


def _layer_norm(h, g, b):
    mu = jnp.mean(h, axis=-1, keepdims=True)
    d = h - mu
    var = jnp.mean(d * d, axis=-1, keepdims=True)
    return d * lax.rsqrt(var + LN_EPS) * g + b


def _params(sem):
    return pltpu.CompilerParams(dimension_semantics=sem, vmem_limit_bytes=VMEM_LIMIT)


def _full(shape):
    n = len(shape)
    return pl.BlockSpec(shape, lambda *_: (0,) * n)


def _inproj_body(x_ref, w_ref, bg_ref, u_ref, zr_ref, gate_ref, *, s5w, rwc):
    xb = x_ref[...].astype(BF16)
    u_ref[...] = jnp.dot(xb, w_ref[:, :s5w], preferred_element_type=F32)
    zr_ref[...] = jnp.dot(xb, w_ref[:, s5w:s5w + rwc], preferred_element_type=F32)
    zg = jnp.dot(xb, w_ref[:, s5w + rwc:], preferred_element_type=F32)
    gate_ref[...] = _sigmoid(zg + bg_ref[...])


def _inproj(x, w_in, b_gate, s5w, rwc, tm):
    t, d = x.shape
    n_in = w_in.shape[1]
    ng = n_in - s5w - rwc
    row = lambda n: pl.BlockSpec((tm, n), lambda i: (i, 0))
    return pl.pallas_call(
        functools.partial(_inproj_body, s5w=s5w, rwc=rwc),
        grid=(t // tm,),
        in_specs=[row(d), _full((d, n_in)), _full((1, ng))],
        out_specs=[row(s5w), row(rwc), row(ng)],
        out_shape=[jax.ShapeDtypeStruct((t, s5w), F32), jax.ShapeDtypeStruct((t, rwc), F32),
                   jax.ShapeDtypeStruct((t, ng), F32)],
        compiler_params=_params(("parallel",)),
        name="inproj",
    )(x, w_in, b_gate)


def _s5_body(u_ref, h0r_ref, h0i_ref, abr_ref, abi_ref, wbu_ref, wy_ref, d_ref, wglu_ref, bglu_ref,
             y_ref, hlr_ref, hli_ref, bur, bui, hr, hi, str_, sti, *, tl, nblk, cb, sb):
    i = pl.program_id(1)

    @pl.when(i == 0)
    def _():
        str_[...] = h0r_ref[...]
        sti[...] = h0i_ref[...]

    u = u_ref[...]
    for j in range(nblk):
        bu = _bdot(u[:, cb * j:cb * (j + 1)], wbu_ref[j])
        bur[:, sb * j:sb * (j + 1)] = bu[:, :sb]
        bui[:, sb * j:sb * (j + 1)] = bu[:, sb:]

    ar = abr_ref[...]
    ai = abi_ref[...]

    def step(t, carry):
        pr, pi = carry
        nr = ar * pr - ai * pi + bur[pl.ds(t, 1), :]
        ni = ar * pi + ai * pr + bui[pl.ds(t, 1), :]
        hr[pl.ds(t, 1), :] = nr
        hi[pl.ds(t, 1), :] = ni
        return nr, ni

    lr, li = lax.fori_loop(0, tl, step, (str_[...], sti[...]), unroll=8)
    str_[...] = lr
    sti[...] = li
    hlr_ref[...] = lr
    hli_ref[...] = li

    ys = []
    for j in range(nblk):
        ys.append(_bdot(hr[:, sb * j:sb * (j + 1)], wy_ref[j, :sb, :])
                  + _bdot(hi[:, sb * j:sb * (j + 1)], wy_ref[j, sb:, :]))
    y = jnp.concatenate(ys, axis=1) + d_ref[...] * u
    y = jax.nn.gelu(y)
    y_ref[...] = y * _sigmoid(_bdot(y, wglu_ref[...]) + bglu_ref[...])


def _s5_mixer(u, h0r, h0i, prm, tl):
    b, l, w = u.shape
    ns = h0r.shape[-1]
    nblk = w // (S5_GROUP * S5_GROUPS_PER_BLOCK)
    cb = w // nblk
    sb = ns // nblk
    seq = lambda n: pl.BlockSpec((None, tl, n), lambda bi, i: (bi, i, 0))
    st = pl.BlockSpec((None, 1, ns), lambda bi, i: (bi, 0, 0))
    return pl.pallas_call(
        functools.partial(_s5_body, tl=tl, nblk=nblk, cb=cb, sb=sb),
        grid=(b, l // tl),
        in_specs=[seq(w), st, st, _full((1, ns)), _full((1, ns)), _full(prm["wbu"].shape),
                  _full(prm["wy"].shape), _full((1, w)), _full((w, w)), _full((1, w))],
        out_specs=[seq(w), st, st],
        out_shape=[jax.ShapeDtypeStruct((b, l, w), F32), jax.ShapeDtypeStruct((b, 1, ns), F32),
                   jax.ShapeDtypeStruct((b, 1, ns), F32)],
        scratch_shapes=[pltpu.VMEM((tl, ns), F32)] * 4 + [pltpu.VMEM((1, ns), F32)] * 2,
        compiler_params=_params(("parallel", "arbitrary")),
        name="s5_mixer",
    )(u, h0r, h0i, prm["abr"], prm["abi"], prm["wbu"], prm["wy"], prm["d"], prm["wglu"], prm["bglu"])


def _s5_discretize(a_re, a_im, log_dt, b_re, b_im, c_re, c_im):
    g, p = a_re.shape
    c = b_re.shape[-1]
    gb = S5_GROUPS_PER_BLOCK
    nblk = g // gb
    dt = jnp.exp(log_dt)[:, None]
    mag = jnp.exp(a_re * dt)
    ab_re = mag * jnp.cos(a_im * dt)
    ab_im = mag * jnp.sin(a_im * dt)
    den = a_re * a_re + a_im * a_im
    n_re = ab_re - 1.0
    k_re = (n_re * a_re + ab_im * a_im) / den
    k_im = (ab_im * a_re - n_re * a_im) / den
    bb_re = k_re[..., None] * b_re - k_im[..., None] * b_im
    bb_im = k_re[..., None] * b_im + k_im[..., None] * b_re
    eye = jnp.eye(gb, dtype=F32)

    def pack_in(bb):
        return jnp.einsum("jgpc,gh->jgchp", bb.reshape(nblk, gb, p, c), eye).reshape(nblk, gb * c, gb * p)

    def pack_out(cc):
        return jnp.einsum("jgcp,gh->jgphc", cc.reshape(nblk, gb, c, p), eye).reshape(nblk, gb * p, gb * c)

    wbu = jnp.concatenate([pack_in(bb_re), pack_in(bb_im)], axis=-1).astype(BF16)
    wy = jnp.concatenate([pack_out(c_re), -pack_out(c_im)], axis=1).astype(BF16)
    return ab_re.reshape(1, g * p), ab_im.reshape(1, g * p), wbu, wy


def _split(x, n):
    parts = []
    rem = x
    for j in range(n):
        p = rem.astype(BF16)
        parts.append(p)
        if j + 1 < n:
            rem = rem - p.astype(F32)
    return parts


_NN = (((1,), (0,)), ((), ()))
_NT = (((1,), (1,)), ((), ()))
_TN = (((0,), (0,)), ((), ()))


def _mdot(a_parts, b_parts, dims=_NN):
    order = max(len(a_parts), len(b_parts))
    acc = None
    for ia, pa in enumerate(a_parts):
        for ib, pb in enumerate(b_parts):
            if ia + ib < order:
                t = lax.dot_general(pa, pb, dims, preferred_element_type=F32)
                acc = t if acc is None else acc + t
    return acc


def _head_sum(x, hd, terms=1):
    n = x.shape[-1]
    same = (lax.broadcasted_iota(jnp.int32, (n, n), 0) // hd
            == lax.broadcasted_iota(jnp.int32, (n, n), 1) // hd)
    return _mdot(_split(x, terms), [same.astype(BF16)])


def _rw_pre_body(*refs, w, dr, ar_, hd, has_vmix):
    (zr_ref, zp_ref, sh0_ref, mu_ref, w0_ref, w2_ref, a0_ref, a2_ref, g2_ref, kk_ref, ka_ref) = refs[:11]
    refs = refs[11:]
    if has_vmix:
        vf_ref, v0_ref, v1_ref, v2_ref = refs[:4]
        refs = refs[4:]
    r_ref, k_ref, v_ref, lw_ref, kkn_ref, bv_ref, g_ref = refs

    i = pl.program_id(1)
    z = zr_ref[...]
    last_prev = jnp.where(i == 0, sh0_ref[...], zp_ref[SHIFT_ROWS - 1:SHIFT_ROWS, :])
    row = lax.broadcasted_iota(jnp.int32, z.shape, 0)
    prev = jnp.where(row == 0, last_prev, pltpu.roll(z, 1, axis=0))
    zs = z + mu_ref[...] * (prev - z)

    r = zs[:, :w]
    k = zs[:, w:2 * w]
    v = zs[:, 2 * w:3 * w]
    zw = zs[:, 3 * w:3 * w + dr]
    za = zs[:, 3 * w + dr:3 * w + dr + ar_]
    zg = zs[:, 3 * w + dr + ar_:]

    wl = w0_ref[...] + _bdot(jnp.tanh(zw), w2_ref[...])
    nwl = -wl
    softplus = jnp.maximum(nwl, 0.0) + jnp.log(1.0 + jnp.exp(-jnp.abs(nwl)))
    lw_ref[...] = -jnp.exp(-softplus - 0.5)
    a = _sigmoid(a0_ref[...] + _bdot(za, a2_ref[...]))
    g_ref[...] = _bdot(_sigmoid(zg), g2_ref[...])
    if has_vmix:
        mix = _sigmoid(v0_ref[...] + _bdot(_bdot(v, v1_ref[...]), v2_ref[...]))
        v = v + (vf_ref[...] - v) * mix
    r_ref[...] = r
    v_ref[...] = v
    kk = k * kk_ref[...]
    nrm = jnp.sqrt(_head_sum(kk * kk, hd))
    kkn = kk / jnp.maximum(nrm, 1e-12)
    kkn_ref[...] = kkn
    bv_ref[...] = kkn * a
    k_ref[...] = k * (1.0 + (a - 1.0) * ka_ref[...])


def _rw_pre(zr, shift0, prm, v_first, tl):
    b, l, cols = zr.shape
    w = prm["w0"].shape[-1]
    dr = prm["w2"].shape[0]
    ar_ = prm["a2"].shape[0]
    has_vmix = v_first is not None
    nsub = tl // SHIFT_ROWS
    seq = lambda n: pl.BlockSpec((None, tl, n), lambda bi, i: (bi, i, 0))
    prev_spec = pl.BlockSpec((None, SHIFT_ROWS, cols),
                             lambda bi, i: (bi, jnp.maximum(i * nsub - 1, 0), 0))
    vec = lambda n: _full((1, n))
    ins = [zr, zr, shift0, prm["mu"], prm["w0"], prm["w2"], prm["a0"], prm["a2"], prm["g2"],
           prm["k_k"], prm["k_a"]]
    specs = [seq(cols), prev_spec, pl.BlockSpec((None, 1, cols), lambda bi, i: (bi, 0, 0)), vec(cols),
             vec(w), _full(prm["w2"].shape), vec(w), _full(prm["a2"].shape), _full(prm["g2"].shape),
             vec(w), vec(w)]
    if has_vmix:
        ins += [v_first, prm["v0"], prm["v1"], prm["v2"]]
        specs += [seq(w), vec(w), _full(prm["v1"].shape), _full(prm["v2"].shape)]
    return pl.pallas_call(
        functools.partial(_rw_pre_body, w=w, dr=dr, ar_=ar_, hd=RW_HEAD, has_vmix=has_vmix),
        grid=(b, l // tl),
        in_specs=specs,
        out_specs=[seq(w)] * 7,
        out_shape=[jax.ShapeDtypeStruct((b, l, w), F32)] * 7,
        compiler_params=_params(("parallel", "parallel")),
        name="rwkv_pre",
    )(*ins)


def _wkv_body(r_ref, k_ref, v_ref, lw_ref, kkn_ref, bv_ref, g_ref, rk_ref, lng_ref, lnb_ref, s0_ref,
              y_ref, sl_ref, s_scr, at_scr, rt_scr, kt_scr, bt_scr, kd_scr, bd_scr, ce_scr, y_scr,
              *, tl, c, cpi, nh, hd):
    i = pl.program_id(1)

    @pl.when(i == 0)
    def _():
        s_scr[...] = s0_ref[...]

    rowi = lax.broadcasted_iota(jnp.int32, (tl, tl), 0)
    coli = lax.broadcasted_iota(jnp.int32, (tl, tl), 1)
    same_chunk = (rowi // c) == (coli // c)
    lw = lw_ref[...]
    cl = _mdot([(same_chunk & (coli <= rowi)).astype(BF16)], _split(lw, 3))
    ce = jnp.concatenate([jnp.broadcast_to(cl[j * c + c - 1:(j + 1) * c, :], (c, cl.shape[1]))
                          for j in range(tl // c)], axis=0)
    kkn = kkn_ref[...]
    bv = bv_ref[...]
    k = k_ref[...]
    einv = jnp.exp(-cl)
    to_end = jnp.exp(ce - cl)
    at_scr[...] = -kkn * jnp.exp(cl - lw)
    rt_scr[...] = r_ref[...] * jnp.exp(cl)
    kt_scr[...] = k * einv
    bt_scr[...] = bv * einv
    kd_scr[...] = k * to_end
    bd_scr[...] = bv * to_end
    ce_scr[...] = ce

    ri = lax.broadcasted_iota(jnp.int32, (c, c), 0)
    ci_ = lax.broadcasted_iota(jnp.int32, (c, c), 1)
    lower = ci_ <= ri
    strict = ci_ < ri
    levels = int(math.log2(c))
    sp = lambda x, cls: _split(x, WKV_TERMS[cls])

    def chunk_group(gi, carry):
        base = pl.multiple_of(gi * (cpi * c), cpi * c)
        units = [(cc, h) for cc in range(cpi) for h in range(nh)]
        rows = [pl.ds(base + cc * c, c) for cc, _ in units]
        hsl = [slice(h * hd, (h + 1) * hd) for _, h in units]
        un = range(len(units))
        at = [at_scr[rows[u], hsl[u]] for u in un]
        rt = [rt_scr[rows[u], hsl[u]] for u in un]
        lhs = [sp(jnp.concatenate([at[u], rt[u]], axis=0), "nt") for u in un]
        rhs = [sp(jnp.concatenate([bt_scr[rows[u], hsl[u]], kt_scr[rows[u], hsl[u]]], axis=0), "nt")
               for u in un]
        pm = [_mdot(lhs[u], rhs[u], _NT) for u in un]
        n_ab = [jnp.where(strict, pm[u][:c, :c], 0.0) for u in un]
        a_ak = [sp(jnp.where(strict, pm[u][:c, c:], 0.0), "out") for u in un]
        a_rb = [sp(jnp.where(lower, pm[u][c:, :c], 0.0), "out") for u in un]
        a_rk = [sp(jnp.where(lower, pm[u][c:, c:], 0.0), "out") for u in un]
        v_p = [sp(v_ref[rows[u], hsl[u]], "out") for u in un]

        x = [jnp.concatenate([at[u], _mdot(a_ak[u], v_p[u])], axis=1) for u in un]
        rkv = [_mdot(a_rk[u], v_p[u]) for u in un]
        pw = n_ab
        for lv in range(levels):
            pw_p = [sp(pw[u], "inv") for u in un]
            x = [x[u] + _mdot(pw_p[u], sp(x[u], "inv")) for u in un]
            if lv + 1 < levels:
                pw = [_mdot(pw_p[u], pw_p[u]) for u in un]
        x_p = [sp(x[u], "out") for u in un]
        qy = [jnp.concatenate([rt[u], rkv[u]], axis=1) + _mdot(a_rb[u], x_p[u]) for u in un]
        mz = [_mdot(x_p[u], sp(bd_scr[rows[u], hsl[u]], "out"), _TN) for u in un]
        zt = [mz[u][hd:, :] + _mdot(v_p[u], sp(kd_scr[rows[u], hsl[u]], "out"), _TN) for u in un]

        s_cur = [s_scr[h] for h in range(nh)]
        for cc in range(cpi):
            g_end = jnp.exp(ce_scr[pl.ds(base + cc * c, 1), :])
            s_p = [sp(s_cur[h], "state") for h in range(nh)]
            for h in range(nh):
                u = cc * nh + h
                y_scr[rows[u], hsl[u]] = _mdot(sp(qy[u][:, :hd], "state"), s_p[h], _NT) + qy[u][:, hd:]
            s_cur = [s_cur[h] * g_end[:, hsl[h]] + _mdot(s_p[h], sp(mz[cc * nh + h][:hd, :], "state"))
                     + zt[cc * nh + h] for h in range(nh)]
        for h in range(nh):
            s_scr[h] = s_cur[h]
        return carry

    lax.fori_loop(0, tl // (cpi * c), chunk_group, 0)

    @pl.when(i == pl.num_programs(1) - 1)
    def _():
        sl_ref[...] = s_scr[...]

    y = y_scr[...]
    yc = y - _head_sum(y, hd) * (1.0 / hd)
    yv = _head_sum(yc * yc, hd) * (1.0 / hd)
    yn = yc * lax.rsqrt(yv + GN_EPS) * lng_ref[...] + lnb_ref[...]
    bonus = _head_sum(r_ref[...] * k_ref[...] * rk_ref[...], hd) * v_ref[...]
    y_ref[...] = (yn + bonus) * g_ref[...]


def _wkv(r, k, v, lw, kkn, bv, g, prm, s0, tl, c):
    b, l, w = r.shape
    nh, hd = s0.shape[1], s0.shape[2]
    seq = pl.BlockSpec((None, tl, w), lambda bi, i: (bi, i, 0))
    st = pl.BlockSpec((None, nh, hd, hd), lambda bi, i: (bi, 0, 0, 0))
    vec = _full((1, w))
    return pl.pallas_call(
        functools.partial(_wkv_body, tl=tl, c=c, cpi=min(WKV_CHUNKS_PER_ITER, tl // c), nh=nh, hd=hd),
        grid=(b, l // tl),
        in_specs=[seq] * 7 + [vec, vec, vec, st],
        out_specs=[seq, st],
        out_shape=[jax.ShapeDtypeStruct((b, l, w), F32), jax.ShapeDtypeStruct(s0.shape, F32)],
        scratch_shapes=[pltpu.VMEM((nh, hd, hd), F32)] + [pltpu.VMEM((tl, w), F32)] * 8,
        compiler_params=_params(("parallel", "arbitrary")),
        name="wkv",
    )(r, k, v, lw, kkn, bv, g, prm["r_k"], prm["ln_g"], prm["ln_b"], s0)


def _merge_body(x_ref, ya_ref, yb_ref, gate_ref, p_ref, wpa_ref, wpb_ref, wo_ref, g1_ref, b1_ref,
                wpg_ref, bpg_ref, wple_ref, wr_ref, br_ref, x1_ref, base_ref, cmb_ref,
                *, d, ngroups, nper):
    gate = gate_ref[...]
    merged = gate[:, :d] * _bdot(ya_ref[...], wpa_ref[...]) + gate[:, d:] * _bdot(yb_ref[...], wpb_ref[...])
    x1 = _layer_norm(ALPHA * x_ref[...] + _bdot(merged, wo_ref[...]), g1_ref[...], b1_ref[...])
    x1_ref[...] = x1.astype(BF16)
    ple = _sigmoid(_bdot(x1, wpg_ref[...]) + bpg_ref[...]) * _bdot(p_ref[...], wple_ref[...])
    base_ref[...] = ALPHA * x1 + ple

    logits = _mdot(_split(x1, 2), _split(wr_ref[...], 2)) + br_ref[...]
    lane = lax.broadcasted_iota(jnp.int32, logits.shape, 1)
    neg = -jnp.inf
    big = logits.shape[1]
    first_of = lambda hit: jnp.min(jnp.where(hit, lane, big), axis=-1, keepdims=True)
    gl = jnp.where(lane < ngroups, logits, neg)
    gmax = jnp.max(gl, axis=-1, keepdims=True)
    g_sel = first_of(gl == gmax)
    g_w = 1.0 / jnp.sum(jnp.exp(gl - gmax), axis=-1, keepdims=True)
    lo = ngroups + nper * g_sel
    el = jnp.where((lane >= lo) & (lane < lo + nper), logits, neg)
    m1 = jnp.max(el, axis=-1, keepdims=True)
    i1 = first_of(el == m1)
    el2 = jnp.where(lane == i1, neg, el)
    m2 = jnp.max(el2, axis=-1, keepdims=True)
    i2 = first_of(el2 == m2)
    e2 = jnp.exp(m2 - m1)
    w1 = g_w / (1.0 + e2)
    w2 = g_w * e2 / (1.0 + e2)
    cmb_ref[...] = (jnp.where(lane == i1, w1, 0.0) + jnp.where(lane == i2, w2, 0.0)
                    + jnp.where(lane == 0, g_sel.astype(F32), 0.0))


def _merge(x, ya, yb, gate, p, prm, tm):
    t, d = x.shape
    row = lambda n: pl.BlockSpec((tm, n), lambda i: (i, 0))
    ws = [prm["w_pa"], prm["w_pb"], prm["w_o"], prm["ln1_g"], prm["ln1_b"], prm["w_pg"], prm["b_pg"],
          prm["w_ple"], prm["w_router"], prm["b_router"]]
    return pl.pallas_call(
        functools.partial(_merge_body, d=d, ngroups=prm["ngroups"], nper=prm["nper"]),
        grid=(t // tm,),
        in_specs=[row(d), row(ya.shape[1]), row(yb.shape[1]), row(gate.shape[1]), row(p.shape[1])]
                 + [_full(a.shape) for a in ws],
        out_specs=[row(d), row(d), row(ROUTER_LANES)],
        out_shape=[jax.ShapeDtypeStruct((t, d), BF16), jax.ShapeDtypeStruct((t, d), F32),
                   jax.ShapeDtypeStruct((t, ROUTER_LANES), F32)],
        compiler_params=_params(("parallel",)),
        name="merge_ln1_router",
    )(x, ya, yb, gate, p, *ws)


def _moe_body(start_ref, nblk_ref, x1_ref, base_ref, cmb_ref, drow_ref, dcol_ref, wg_ref, wu_ref, wd_ref,
              g2_ref, b2_ref, o_ref, xs_scr, cs_scr, acc_scr, *, lane0, ngroups, nper, blk):
    i = pl.program_id(0)
    e = pl.program_id(1)
    nslot, tm = xs_scr.shape[0], x1_ref.shape[0]
    eps = wg_ref.shape[0]

    @pl.when(e == 0)
    def _():
        slot = lax.broadcasted_iota(jnp.int32, (nslot, tm), 0)
        pm = (slot == drow_ref[...]).astype(BF16)
        xs_scr[...] = jnp.dot(pm, x1_ref[...], preferred_element_type=F32).astype(BF16)
        cs_scr[...] = _mdot([pm], _split(cmb_ref[...], 2))
        acc_scr[...] = jnp.zeros_like(acc_scr)

    grp = (e * eps) // nper
    row0 = start_ref[i * ngroups + grp]
    lane = lax.broadcasted_iota(jnp.int32, (blk, cs_scr.shape[1]), 1)

    def block(j, carry):
        rows = pl.ds(pl.multiple_of(row0 + j * blk, blk), blk)
        xb = xs_scr[rows, :]
        cs = cs_scr[rows, :]
        part = None
        for ee in range(eps):
            ce = jnp.sum(jnp.where(lane == e * eps + ee + lane0, cs, 0.0), axis=-1, keepdims=True)
            hg = jnp.dot(xb, wg_ref[ee], preferred_element_type=F32)
            hu = jnp.dot(xb, wu_ref[ee], preferred_element_type=F32)
            t = _bdot(hg * _sigmoid(hg) * hu * ce, wd_ref[ee])
            part = t if part is None else part + t
        acc_scr[rows, :] += part
        return carry

    lax.fori_loop(0, nblk_ref[i * ngroups + grp], block, 0)

    @pl.when(e == pl.num_programs(1) - 1)
    def _():
        slot_t = lax.broadcasted_iota(jnp.int32, (tm, nslot), 1)
        pmt = (slot_t == dcol_ref[...]).astype(BF16)
        ffn = _mdot([pmt], _split(acc_scr[...], 2))
        o_ref[...] = _layer_norm(base_ref[...] + ffn, g2_ref[...], b2_ref[...])


def _moe_dispatch(cmb, tm, ngroups, blk):
    t = cmb.shape[0]
    nt = t // tm
    g = cmb[:, 0].astype(jnp.int32).reshape(nt, tm)
    oh = (g[..., None] == jnp.arange(ngroups, dtype=jnp.int32)).astype(jnp.int32)
    rank = jnp.sum((jnp.cumsum(oh, axis=1) - oh) * oh, axis=-1)
    nblk = (jnp.sum(oh, axis=1) + blk - 1) // blk
    start = (jnp.cumsum(nblk, axis=-1) - nblk) * blk
    dest = jnp.take_along_axis(start, g, axis=1) + rank
    return start.reshape(-1), nblk.reshape(-1), dest.reshape(nt, 1, tm), dest.reshape(t, 1)


def _moe(x1, base, cmb, prm, tm):
    t, d = base.shape
    ne, _, f = prm["w_gate"].shape
    ngroups, nper = prm["ngroups"], prm["nper"]
    eps = MOE_EXPERTS_PER_STEP
    blk = min(MOE_BLOCK_ROWS, tm)
    nslot = tm + ngroups * blk
    start, nblk, drow, dcol = _moe_dispatch(cmb, tm, ngroups, blk)
    row = lambda n: pl.BlockSpec((tm, n), lambda i, e, *_: (i, 0))
    wspec = lambda a, b: pl.BlockSpec((eps, a, b), lambda i, e, *_: (e, 0, 0))
    vec = pl.BlockSpec((1, d), lambda i, e, *_: (0, 0))
    return pl.pallas_call(
        functools.partial(_moe_body, lane0=ngroups, ngroups=ngroups, nper=nper, blk=blk),
        grid_spec=pltpu.PrefetchScalarGridSpec(
            num_scalar_prefetch=2,
            grid=(t // tm, ne // eps),
            in_specs=[row(d), row(d), row(ROUTER_LANES),
                      pl.BlockSpec((None, 1, tm), lambda i, e, *_: (i, 0, 0)), row(1),
                      wspec(d, f), wspec(d, f), wspec(f, d), vec, vec],
            out_specs=row(d),
            scratch_shapes=[pltpu.VMEM((nslot, d), BF16), pltpu.VMEM((nslot, ROUTER_LANES), F32),
                            pltpu.VMEM((nslot, d), F32)]),
        out_shape=jax.ShapeDtypeStruct((t, d), F32),
        compiler_params=_params(("parallel", "arbitrary")),
        name="moe_ln2",
    )(start, nblk, x1, base, cmb, drow, dcol, prm["w_gate"], prm["w_up"], prm["w_down"],
      prm["ln2_g"], prm["ln2_b"])


def _tile(n, target):
    t = min(n, target)
    assert n % t == 0, (n, t)
    return t


def _layer(x, p, shift0, wkv0, s5r0, s5i0, v_first, lp):
    b, l, d = x.shape
    t = b * l
    s5w = lp["s5"]["d"].shape[-1]
    rwc = lp["rw"]["mu"].shape[-1]
    u, zr, gate = _inproj(x.reshape(t, d), lp["w_in"], lp["b_gate"], s5w, rwc, _tile(t, 256))
    ya, s5r, s5i = _s5_mixer(u.reshape(b, l, s5w), s5r0, s5i0, lp["s5"], _tile(l, 256))
    zr3 = zr.reshape(b, l, rwc)
    tl = _tile(l, 256)
    r, k, v, lw, kkn, bv, g = _rw_pre(zr3, shift0, lp["rw"], v_first, tl)
    yb, wkv_new = _wkv(r, k, v, lw, kkn, bv, g, lp["rw"], wkv0, tl, min(WKV_CHUNK, l))
    x1, base, cmb = _merge(x.reshape(t, d), ya.reshape(t, s5w), yb.reshape(t, -1), gate,
                           p.reshape(t, -1), lp["mrg"], _tile(t, 256))
    out = _moe(x1, base, cmb, lp["moe"], _tile(t, 1024))
    shift_new = zr3[:, l - 1:, :]
    return out.reshape(b, l, d), shift_new, wkv_new, s5r, s5i, (v if v_first is None else v_first)


def _trunk(x, p, shift0, wkv0, s50, layers):
    b = x.shape[0]
    shifts, wkvs, s5s = [], [], []
    v_first = None
    for i, lp in enumerate(layers):
        ns = s50.shape[2] * s50.shape[3]
        s5r0 = s50[i, ..., 0].reshape(b, 1, ns)
        s5i0 = s50[i, ..., 1].reshape(b, 1, ns)
        x, sh, wk, s5r, s5i, v_first = _layer(x, p[i], shift0[i], wkv0[i], s5r0, s5i0, v_first, lp)
        shifts.append(sh)
        wkvs.append(wk)
        s5s.append(jnp.stack([s5r.reshape(s50.shape[1:4]), s5i.reshape(s50.shape[1:4])], axis=-1))
    return x, jnp.stack(shifts), jnp.stack(wkvs), jnp.stack(s5s)


def kernel(x_prompt, x_sample, state_shift, state_wkv, state_s5, p_prompt, p_sample, w_in, b_gate, s5_a_re, s5_a_im, s5_log_dt, s5_b_re, s5_b_im, s5_c_re, s5_c_im, s5_d, s5_w_glu, s5_b_glu, rw_mu, rw_w0, rw_w2, rw_a0, rw_a2, rw_g2, rw_v0, rw_v1, rw_v2, rw_k_k, rw_k_a, rw_r_k, rw_ln_g, rw_ln_b, w_pa, w_pb, w_o, ln1_g, ln1_b, w_rg, b_rg, w_re, b_re, w_gate, w_up, w_down, w_ple, w_pg, b_pg, ln2_g, ln2_b):
    depth, d = w_in.shape[0], w_in.shape[1]
    ngroups = w_rg.shape[-1]
    nper = w_re.shape[-1]
    nexp = ngroups * nper
    layers = []
    for i in range(depth):
        abr, abi, wbu, wy = _s5_discretize(s5_a_re[i], s5_a_im[i], s5_log_dt[i], s5_b_re[i], s5_b_im[i],
                                           s5_c_re[i], s5_c_im[i])
        s5 = dict(abr=abr, abi=abi, wbu=wbu, wy=wy, d=s5_d[i][None], wglu=s5_w_glu[i].astype(BF16),
                  bglu=s5_b_glu[i][None])
        rw = dict(mu=rw_mu[i][None], w0=rw_w0[i][None], w2=rw_w2[i].astype(BF16), a0=rw_a0[i][None],
                  a2=rw_a2[i].astype(BF16), g2=rw_g2[i].astype(BF16), k_k=rw_k_k[i][None],
                  k_a=rw_k_a[i][None], r_k=rw_r_k[i].reshape(1, -1), ln_g=rw_ln_g[i][None],
                  ln_b=rw_ln_b[i][None])
        if i > 0:
            rw.update(v0=rw_v0[i - 1][None], v1=rw_v1[i - 1].astype(BF16), v2=rw_v2[i - 1].astype(BF16))
        pad = ROUTER_LANES - ngroups - nexp
        w_router = jnp.concatenate([w_rg[i], w_re[i].reshape(d, nexp), jnp.zeros((d, pad), F32)], axis=1)
        b_router = jnp.concatenate([b_rg[i], b_re[i].reshape(nexp), jnp.zeros((pad,), F32)])[None]
        mrg = dict(w_pa=w_pa[i].astype(BF16), w_pb=w_pb[i].astype(BF16), w_o=w_o[i].astype(BF16),
                   ln1_g=ln1_g[i][None], ln1_b=ln1_b[i][None], w_pg=w_pg[i].astype(BF16),
                   b_pg=b_pg[i][None], w_ple=w_ple[i].astype(BF16), w_router=w_router,
                   b_router=b_router, ngroups=ngroups, nper=nper)
        moe = dict(w_gate=w_gate[i].astype(BF16), w_up=w_up[i].astype(BF16),
                   w_down=w_down[i].astype(BF16), ln2_g=ln2_g[i][None], ln2_b=ln2_b[i][None],
                   ngroups=ngroups, nper=nper)
        layers.append(dict(w_in=w_in[i].astype(BF16), b_gate=b_gate[i][None], s5=s5, rw=rw, mrg=mrg,
                           moe=moe))

    bp = x_prompt.shape[0]
    dt = x_prompt.dtype
    zero_shift = jnp.zeros((depth, bp) + state_shift.shape[2:], dt)
    zero_wkv = jnp.zeros((depth, bp) + state_wkv.shape[2:], dt)
    zero_s5 = jnp.zeros((depth, bp) + state_s5.shape[2:], dt)
    y_p, sh_p, wkv_p, s5_p = _trunk(x_prompt, p_prompt, zero_shift, zero_wkv, zero_s5, layers)
    y_s, sh_s, wkv_s, s5_s = _trunk(x_sample, p_sample, state_shift, state_wkv, state_s5, layers)
    return (y_p, y_s, sh_p, wkv_p, s5_p, sh_s, wkv_s, s5_s)
```

```python
import functools
import math

import jax
import jax.numpy as jnp
from jax import lax
from jax.experimental import pallas as pl
from jax.experimental.pallas import tpu as pltpu

F32 = jnp.float32
BF16 = jnp.bfloat16
HIGHEST = lax.Precision.HIGHEST

DEPTH = 2
ALPHA = (2 * DEPTH) ** 0.25
LN_EPS = 1e-5
GN_EPS = 64e-5

S5_GROUP = 16
S5_STATE = 64
S5_GROUPS_PER_BLOCK = 8
RW_HEAD = 64
SHIFT_ROWS = 8
ROUTER_LANES = 128
WKV_CHUNK = 64
VMEM_LIMIT = 56 * 1024 * 1024
WKV_TERMS = {"nt": 1, "inv": 1, "out": 1, "state": 1}
WKV_CHUNKS_PER_ITER = 4
MOE_EXPERTS_PER_STEP = 2
S5_SLAB_ROWS = 64
MERGE_ROW_SLABS = 2
MOE_BLOCK_ROWS = 128


def _bdot(a, b):
    return jnp.dot(a.astype(BF16), b.astype(BF16), preferred_element_type=F32)


def _fdot(a, b):
    return jnp.dot(a, b, precision=HIGHEST, preferred_element_type=F32)


def _fdot_nt(a, b):
    return lax.dot_general(a, b, (((1,), (1,)), ((), ())), precision=HIGHEST,
                           preferred_element_type=F32)


def _fdot_tn(a, b):
    return lax.dot_general(a, b, (((0,), (0,)), ((), ())), precision=HIGHEST,
                           preferred_element_type=F32)


def _sigmoid(x):
    return 1.0 / (1.0 + jnp.exp(-x))


def _layer_norm(h, g, b):
    mu = jnp.mean(h, axis=-1, keepdims=True)
    d = h - mu
    var = jnp.mean(d * d, axis=-1, keepdims=True)
    return d * lax.rsqrt(var + LN_EPS) * g + b


def _params(sem):
    return pltpu.CompilerParams(dimension_semantics=sem, vmem_limit_bytes=VMEM_LIMIT)


def _full(shape):
    n = len(shape)
    return pl.BlockSpec(shape, lambda *_: (0,) * n)


def _inproj_body(x_ref, w_ref, bg_ref, u_ref, zr_ref, gate_ref, *, s5w, rwc):
    xb = x_ref[...].astype(BF16)
    u_ref[...] = jnp.dot(xb, w_ref[:, :s5w], preferred_element_type=F32)
    zr_ref[...] = jnp.dot(xb, w_ref[:, s5w:s5w + rwc], preferred_element_type=F32)
    zg = jnp.dot(xb, w_ref[:, s5w + rwc:], preferred_element_type=F32)
    gate_ref[...] = _sigmoid(zg + bg_ref[...])


def _inproj(x, w_in, layer, b_gate, s5w, rwc, tm):
    t, d = x.shape
    n_in = w_in.shape[2]
    ng = n_in - s5w - rwc
    row = lambda n: pl.BlockSpec((tm, n), lambda i: (i, 0))
    return pl.pallas_call(
        functools.partial(_inproj_body, s5w=s5w, rwc=rwc),
        grid=(t // tm,),
        in_specs=[row(d), pl.BlockSpec((None, d, n_in), lambda i: (layer, 0, 0)), _full((1, ng))],
        out_specs=[row(s5w), row(rwc), row(ng)],
        out_shape=[jax.ShapeDtypeStruct((t, s5w), F32), jax.ShapeDtypeStruct((t, rwc), F32),
                   jax.ShapeDtypeStruct((t, ng), F32)],
        compiler_params=_params(("parallel",)),
        name="inproj",
    )(x, w_in, b_gate)


def _s5_body(u_ref, h0r_ref, h0i_ref, abr_ref, abi_ref, wbu_ref, wy_ref, d_ref, wglu_ref, bglu_ref,
             y_ref, hlr_ref, hli_ref, bur, bui, hr, hi, str_, sti, *, nb, tl, sub, nblk, cb, sb):
    i = pl.program_id(0)

    @pl.when(i == 0)
    def _():
        str_[...] = h0r_ref[...]
        sti[...] = h0i_ref[...]

    slabs = [(bi, s0) for s0 in range(0, tl, sub) for bi in range(nb)]
    for bi, s0 in slabs:
        u = u_ref[bi, s0:s0 + sub, :]
        for j in range(nblk):
            bu = _bdot(u[:, cb * j:cb * (j + 1)], wbu_ref[j])
            bur[bi, s0:s0 + sub, sb * j:sb * (j + 1)] = bu[:, :sb]
            bui[bi, s0:s0 + sub, sb * j:sb * (j + 1)] = bu[:, sb:]

    ar = abr_ref[...]
    ai = abi_ref[...]
    state = [(str_[bi], sti[bi]) for bi in range(nb)]
    for s0 in range(0, tl, sub):
        for t in range(s0, s0 + sub):
            for bi in range(nb):
                pr, pi = state[bi]
                nr = ar * pr - ai * pi + bur[bi, t:t + 1, :]
                ni = ar * pi + ai * pr + bui[bi, t:t + 1, :]
                hr[bi, t:t + 1, :] = nr
                hi[bi, t:t + 1, :] = ni
                state[bi] = (nr, ni)
        for bi in range(nb):
            u = u_ref[bi, s0:s0 + sub, :]
            ys = [_bdot(hr[bi, s0:s0 + sub, sb * j:sb * (j + 1)], wy_ref[j, :sb, :])
                  + _bdot(hi[bi, s0:s0 + sub, sb * j:sb * (j + 1)], wy_ref[j, sb:, :])
                  for j in range(nblk)]
            y = jax.nn.gelu(jnp.concatenate(ys, axis=1) + d_ref[...] * u)
            y_ref[bi, s0:s0 + sub, :] = y * _sigmoid(_bdot(y, wglu_ref[...]) + bglu_ref[...])
    for bi in range(nb):
        str_[bi] = state[bi][0]
        sti[bi] = state[bi][1]
        hlr_ref[bi] = state[bi][0]
        hli_ref[bi] = state[bi][1]


def _s5_mixer(u, h0r, h0i, prm, tl):
    b, l, w = u.shape
    ns = h0r.shape[-1]
    nblk = w // (S5_GROUP * S5_GROUPS_PER_BLOCK)
    cb = w // nblk
    sb = ns // nblk
    seq = pl.BlockSpec((b, tl, w), lambda i: (0, i, 0))
    st = _full((b, 1, ns))
    return pl.pallas_call(
        functools.partial(_s5_body, nb=b, tl=tl, sub=min(S5_SLAB_ROWS, tl), nblk=nblk, cb=cb, sb=sb),
        grid=(l // tl,),
        in_specs=[seq, st, st, _full((1, ns)), _full((1, ns)), _full(prm["wbu"].shape),
                  _full(prm["wy"].shape), _full((1, w)), _full((w, w)), _full((1, w))],
        out_specs=[seq, st, st],
        out_shape=[jax.ShapeDtypeStruct((b, l, w), F32), jax.ShapeDtypeStruct((b, 1, ns), F32),
                   jax.ShapeDtypeStruct((b, 1, ns), F32)],
        scratch_shapes=[pltpu.VMEM((b, tl, ns), F32)] * 4 + [pltpu.VMEM((b, 1, ns), F32)] * 2,
        compiler_params=_params(("arbitrary",)),
        name="s5_mixer",
    )(u, h0r, h0i, prm["abr"], prm["abi"], prm["wbu"], prm["wy"], prm["d"], prm["wglu"], prm["bglu"])


def _s5_discretize(a_re, a_im, log_dt, b_re, b_im, c_re, c_im):
    g, p = a_re.shape
    c = b_re.shape[-1]
    gb = S5_GROUPS_PER_BLOCK
    nblk = g // gb
    dt = jnp.exp(log_dt)[:, None]
    mag = jnp.exp(a_re * dt)
    ab_re = mag * jnp.cos(a_im * dt)
    ab_im = mag * jnp.sin(a_im * dt)
    den = a_re * a_re + a_im * a_im
    n_re = ab_re - 1.0
    k_re = (n_re * a_re + ab_im * a_im) / den
    k_im = (ab_im * a_re - n_re * a_im) / den
    bb_re = k_re[..., None] * b_re - k_im[..., None] * b_im
    bb_im = k_re[..., None] * b_im + k_im[..., None] * b_re
    eye = jnp.eye(gb, dtype=F32)

    def pack_in(bb):
        return jnp.einsum("jgpc,gh->jgchp", bb.reshape(nblk, gb, p, c), eye).reshape(nblk, gb * c, gb * p)

    def pack_out(cc):
        return jnp.einsum("jgcp,gh->jgphc", cc.reshape(nblk, gb, c, p), eye).reshape(nblk, gb * p, gb * c)

    wbu = jnp.concatenate([pack_in(bb_re), pack_in(bb_im)], axis=-1).astype(BF16)
    wy = jnp.concatenate([pack_out(c_re), -pack_out(c_im)], axis=1).astype(BF16)
    return ab_re.reshape(1, g * p), ab_im.reshape(1, g * p), wbu, wy


def _split(x, n):
    parts = []
    rem = x
    for j in range(n):
        p = rem.astype(BF16)
        parts.append(p)
        if j + 1 < n:
            rem = rem - p.astype(F32)
    return parts


_NN = (((1,), (0,)), ((), ()))
_NT = (((1,), (1,)), ((), ()))
_TN = (((0,), (0,)), ((), ()))


def _mdot(a_parts, b_parts, dims=_NN):
    order = max(len(a_parts), len(b_parts))
    acc = None
    for ia, pa in enumerate(a_parts):
        for ib, pb in enumerate(b_parts):
            if ia + ib < order:
                t = lax.dot_general(pa, pb, dims, preferred_element_type=F32)
                acc = t if acc is None else acc + t
    return acc


def _head_sum(x, hd, terms=1):
    n = x.shape[-1]
    same = (lax.broadcasted_iota(jnp.int32, (n, n), 0) // hd
            == lax.broadcasted_iota(jnp.int32, (n, n), 1) // hd)
    return _mdot(_split(x, terms), [same.astype(BF16)])


def _rw_pre_body(*refs, w, dr, ar_, hd, has_vmix):
    (zr_ref, zp_ref, sh0_ref, mu_ref, w0_ref, w2_ref, a0_ref, a2_ref, g2_ref, kk_ref, ka_ref) = refs[:11]
    refs = refs[11:]
    if has_vmix:
        vf_ref, v0_ref, v1_ref, v2_ref = refs[:4]
        refs = refs[4:]
    r_ref, k_ref, v_ref, lw_ref, kkn_ref, bv_ref, g_ref = refs

    i = pl.program_id(1)
    z = zr_ref[...]
    last_prev = jnp.where(i == 0, sh0_ref[...], zp_ref[SHIFT_ROWS - 1:SHIFT_ROWS, :])
    row = lax.broadcasted_iota(jnp.int32, z.shape, 0)
    prev = jnp.where(row == 0, last_prev, pltpu.roll(z, 1, axis=0))
    zs = z + mu_ref[...] * (prev - z)

    r = zs[:, :w]
    k = zs[:, w:2 * w]
    v = zs[:, 2 * w:3 * w]
    zw = zs[:, 3 * w:3 * w + dr]
    za = zs[:, 3 * w + dr:3 * w + dr + ar_]
    zg = zs[:, 3 * w + dr + ar_:]

    wl = w0_ref[...] + _bdot(jnp.tanh(zw), w2_ref[...])
    nwl = -wl
    softplus = jnp.maximum(nwl, 0.0) + jnp.log(1.0 + jnp.exp(-jnp.abs(nwl)))
    lw_ref[...] = -jnp.exp(-softplus - 0.5)
    a = _sigmoid(a0_ref[...] + _bdot(za, a2_ref[...]))
    g_ref[...] = _bdot(_sigmoid(zg), g2_ref[...])
    if has_vmix:
        mix = _sigmoid(v0_ref[...] + _bdot(_bdot(v, v1_ref[...]), v2_ref[...]))
        v = v + (vf_ref[...] - v) * mix
    r_ref[...] = r
    v_ref[...] = v
    kk = k * kk_ref[...]
    nrm = jnp.sqrt(_head_sum(kk * kk, hd))
    kkn = kk / jnp.maximum(nrm, 1e-12)
    kkn_ref[...] = kkn
    bv_ref[...] = kkn * a
    k_ref[...] = k * (1.0 + (a - 1.0) * ka_ref[...])


def _rw_pre(zr, shift0, prm, v_first, tl):
    b, l, cols = zr.shape
    w = prm["w0"].shape[-1]
    dr = prm["w2"].shape[0]
    ar_ = prm["a2"].shape[0]
    has_vmix = v_first is not None
    nsub = tl // SHIFT_ROWS
    seq = lambda n: pl.BlockSpec((None, tl, n), lambda bi, i: (bi, i, 0))
    prev_spec = pl.BlockSpec((None, SHIFT_ROWS, cols),
                             lambda bi, i: (bi, jnp.maximum(i * nsub - 1, 0), 0))
    vec = lambda n: _full((1, n))
    ins = [zr, zr, shift0, prm["mu"], prm["w0"], prm["w2"], prm["a0"], prm["a2"], prm["g2"],
           prm["k_k"], prm["k_a"]]
    specs = [seq(cols), prev_spec, pl.BlockSpec((None, 1, cols), lambda bi, i: (bi, 0, 0)), vec(cols),
             vec(w), _full(prm["w2"].shape), vec(w), _full(prm["a2"].shape), _full(prm["g2"].shape),
             vec(w), vec(w)]
    if has_vmix:
        ins += [v_first, prm["v0"], prm["v1"], prm["v2"]]
        specs += [seq(w), vec(w), _full(prm["v1"].shape), _full(prm["v2"].shape)]
    return pl.pallas_call(
        functools.partial(_rw_pre_body, w=w, dr=dr, ar_=ar_, hd=RW_HEAD, has_vmix=has_vmix),
        grid=(b, l // tl),
        in_specs=specs,
        out_specs=[seq(w)] * 7,
        out_shape=[jax.ShapeDtypeStruct((b, l, w), F32)] * 7,
        compiler_params=_params(("parallel", "parallel")),
        name="rwkv_pre",
    )(*ins)


def _wkv_body(r_ref, k_ref, v_ref, lw_ref, kkn_ref, bv_ref, g_ref, rk_ref, lng_ref, lnb_ref, s0_ref,
              y_ref, sl_ref, s_scr, at_scr, rt_scr, kt_scr, bt_scr, kd_scr, bd_scr, ce_scr, y_scr,
              *, tl, c, cpi, nh, hd):
    i = pl.program_id(1)

    @pl.when(i == 0)
    def _():
        s_scr[...] = s0_ref[...]

    rowi = lax.broadcasted_iota(jnp.int32, (tl, tl), 0)
    coli = lax.broadcasted_iota(jnp.int32, (tl, tl), 1)
    same_chunk = (rowi // c) == (coli // c)
    lw = lw_ref[...]
    cl = _mdot([(same_chunk & (coli <= rowi)).astype(BF16)], _split(lw, 3))
    ce = jnp.concatenate([jnp.broadcast_to(cl[j * c + c - 1:(j + 1) * c, :], (c, cl.shape[1]))
                          for j in range(tl // c)], axis=0)
    kkn = kkn_ref[...]
    bv = bv_ref[...]
    k = k_ref[...]
    einv = jnp.exp(-cl)
    to_end = jnp.exp(ce - cl)
    at_scr[...] = -kkn * jnp.exp(cl - lw)
    rt_scr[...] = r_ref[...] * jnp.exp(cl)
    kt_scr[...] = k * einv
    bt_scr[...] = bv * einv
    kd_scr[...] = k * to_end
    bd_scr[...] = bv * to_end
    ce_scr[...] = ce

    ri = lax.broadcasted_iota(jnp.int32, (c, c), 0)
    ci_ = lax.broadcasted_iota(jnp.int32, (c, c), 1)
    lower = ci_ <= ri
    strict = ci_ < ri
    levels = int(math.log2(c))
    sp = lambda x, cls: _split(x, WKV_TERMS[cls])

    def chunk_group(gi, carry):
        base = pl.multiple_of(gi * (cpi * c), cpi * c)
        units = [(cc, h) for cc in range(cpi) for h in range(nh)]
        rows = [pl.ds(base + cc * c, c) for cc, _ in units]
        hsl = [slice(h * hd, (h + 1) * hd) for _, h in units]
        un = range(len(units))
        at = [at_scr[rows[u], hsl[u]] for u in un]
        rt = [rt_scr[rows[u], hsl[u]] for u in un]
        lhs = [sp(jnp.concatenate([at[u], rt[u]], axis=0), "nt") for u in un]
        rhs = [sp(jnp.concatenate([bt_scr[rows[u], hsl[u]], kt_scr[rows[u], hsl[u]]], axis=0), "nt")
               for u in un]
        pm = [_mdot(lhs[u], rhs[u], _NT) for u in un]
        n_ab = [jnp.where(strict, pm[u][:c, :c], 0.0) for u in un]
        a_ak = [sp(jnp.where(strict, pm[u][:c, c:], 0.0), "out") for u in un]
        a_rb = [sp(jnp.where(lower, pm[u][c:, :c], 0.0), "out") for u in un]
        a_rk = [sp(jnp.where(lower, pm[u][c:, c:], 0.0), "out") for u in un]
        v_p = [sp(v_ref[rows[u], hsl[u]], "out") for u in un]

        x = [jnp.concatenate([at[u], _mdot(a_ak[u], v_p[u])], axis=1) for u in un]
        rkv = [_mdot(a_rk[u], v_p[u]) for u in un]
        pw = n_ab
        for lv in range(levels):
            pw_p = [sp(pw[u], "inv") for u in un]
            x = [x[u] + _mdot(pw_p[u], sp(x[u], "inv")) for u in un]
            if lv + 1 < levels:
                pw = [_mdot(pw_p[u], pw_p[u]) for u in un]
        x_p = [sp(x[u], "out") for u in un]
        qy = [jnp.concatenate([rt[u], rkv[u]], axis=1) + _mdot(a_rb[u], x_p[u]) for u in un]
        mz = [_mdot(x_p[u], sp(bd_scr[rows[u], hsl[u]], "out"), _TN) for u in un]
        zt = [mz[u][hd:, :] + _mdot(v_p[u], sp(kd_scr[rows[u], hsl[u]], "out"), _TN) for u in un]

        s_cur = [s_scr[h] for h in range(nh)]
        for cc in range(cpi):
            g_end = jnp.exp(ce_scr[pl.ds(base + cc * c, 1), :])
            s_p = [sp(s_cur[h], "state") for h in range(nh)]
            for h in range(nh):
                u = cc * nh + h
                y_scr[rows[u], hsl[u]] = _mdot(sp(qy[u][:, :hd], "state"), s_p[h], _NT) + qy[u][:, hd:]
            s_cur = [s_cur[h] * g_end[:, hsl[h]] + _mdot(s_p[h], sp(mz[cc * nh + h][:hd, :], "state"))
                     + zt[cc * nh + h] for h in range(nh)]
        for h in range(nh):
            s_scr[h] = s_cur[h]
        return carry

    lax.fori_loop(0, tl // (cpi * c), chunk_group, 0)

    @pl.when(i == pl.num_programs(1) - 1)
    def _():
        sl_ref[...] = s_scr[...]

    y = y_scr[...]
    yc = y - _head_sum(y, hd) * (1.0 / hd)
    yv = _head_sum(yc * yc, hd) * (1.0 / hd)
    yn = yc * lax.rsqrt(yv + GN_EPS) * lng_ref[...] + lnb_ref[...]
    bonus = _head_sum(r_ref[...] * k_ref[...] * rk_ref[...], hd) * v_ref[...]
    y_ref[...] = (yn + bonus) * g_ref[...]


def _wkv(r, k, v, lw, kkn, bv, g, prm, s0, tl, c):
    b, l, w = r.shape
    nh, hd = s0.shape[1], s0.shape[2]
    seq = pl.BlockSpec((None, tl, w), lambda bi, i: (bi, i, 0))
    st = pl.BlockSpec((None, nh, hd, hd), lambda bi, i: (bi, 0, 0, 0))
    vec = _full((1, w))
    return pl.pallas_call(
        functools.partial(_wkv_body, tl=tl, c=c, cpi=min(WKV_CHUNKS_PER_ITER, tl // c), nh=nh, hd=hd),
        grid=(b, l // tl),
        in_specs=[seq] * 7 + [vec, vec, vec, st],
        out_specs=[seq, st],
        out_shape=[jax.ShapeDtypeStruct((b, l, w), F32), jax.ShapeDtypeStruct(s0.shape, F32)],
        scratch_shapes=[pltpu.VMEM((nh, hd, hd), F32)] + [pltpu.VMEM((tl, w), F32)] * 8,
        compiler_params=_params(("parallel", "arbitrary")),
        name="wkv",
    )(r, k, v, lw, kkn, bv, g, prm["r_k"], prm["ln_g"], prm["ln_b"], s0)


def _merge_body(x_ref, ya_ref, yb_ref, gate_ref, p_ref, wpa_ref, wpb_ref, wo_ref, g1_ref, b1_ref,
                wpg_ref, bpg_ref, wple_ref, wr_ref, br_ref, x1_ref, base_ref, cmb_ref,
                *, d, ngroups, nper, nsplit):
    rs = x_ref.shape[0] // nsplit
    slabs = [slice(j * rs, (j + 1) * rs) for j in range(nsplit)]
    pa = [_bdot(ya_ref[s, :], wpa_ref[...]) for s in slabs]
    pb = [_bdot(yb_ref[s, :], wpb_ref[...]) for s in slabs]
    merged = [gate_ref[s, :d] * a + gate_ref[s, d:] * b for s, a, b in zip(slabs, pa, pb)]
    h = [ALPHA * x_ref[s, :] + _bdot(m, wo_ref[...]) for s, m in zip(slabs, merged)]
    x1s = [_layer_norm(v, g1_ref[...], b1_ref[...]) for v in h]
    pg = [_bdot(v, wpg_ref[...]) for v in x1s]
    pe = [_bdot(p_ref[s, :], wple_ref[...]) for s in slabs]
    wr_p = _split(wr_ref[...], 2)
    lg = [_mdot(_split(v, 2), wr_p) + br_ref[...] for v in x1s]
    for s, v, g_, e_, l_ in zip(slabs, x1s, pg, pe, lg):
        x1_ref[s, :] = v.astype(BF16)
        base_ref[s, :] = ALPHA * v + _sigmoid(g_ + bpg_ref[...]) * e_
        cmb_ref[s, :] = _route(l_, ngroups, nper)


def _route(logits, ngroups, nper):
    lane = lax.broadcasted_iota(jnp.int32, logits.shape, 1)
    neg = -jnp.inf
    big = logits.shape[1]
    first_of = lambda hit: jnp.min(jnp.where(hit, lane, big), axis=-1, keepdims=True)
    gl = jnp.where(lane < ngroups, logits, neg)
    gmax = jnp.max(gl, axis=-1, keepdims=True)
    g_sel = first_of(gl == gmax)
    g_w = 1.0 / jnp.sum(jnp.exp(gl - gmax), axis=-1, keepdims=True)
    lo = ngroups + nper * g_sel
    el = jnp.where((lane >= lo) & (lane < lo + nper), logits, neg)
    m1 = jnp.max(el, axis=-1, keepdims=True)
    i1 = first_of(el == m1)
    el2 = jnp.where(lane == i1, neg, el)
    m2 = jnp.max(el2, axis=-1, keepdims=True)
    i2 = first_of(el2 == m2)
    e2 = jnp.exp(m2 - m1)
    w1 = g_w / (1.0 + e2)
    w2 = g_w * e2 / (1.0 + e2)
    return (jnp.where(lane == i1, w1, 0.0) + jnp.where(lane == i2, w2, 0.0)
            + jnp.where(lane == 0, g_sel.astype(F32), 0.0))


def _merge(x, ya, yb, gate, p, prm, tm):
    t, d = x.shape
    row = lambda n: pl.BlockSpec((tm, n), lambda i: (i, 0))
    ws = [prm["w_pa"], prm["w_pb"], prm["w_o"], prm["ln1_g"], prm["ln1_b"], prm["w_pg"], prm["b_pg"],
          prm["w_ple"], prm["w_router"], prm["b_router"]]
    return pl.pallas_call(
        functools.partial(_merge_body, d=d, ngroups=prm["ngroups"], nper=prm["nper"],
                          nsplit=MERGE_ROW_SLABS),
        grid=(t // tm,),
        in_specs=[row(d), row(ya.shape[1]), row(yb.shape[1]), row(gate.shape[1]), row(p.shape[1])]
                 + [_full(a.shape) for a in ws],
        out_specs=[row(d), row(d), row(ROUTER_LANES)],
        out_shape=[jax.ShapeDtypeStruct((t, d), BF16), jax.ShapeDtypeStruct((t, d), F32),
                   jax.ShapeDtypeStruct((t, ROUTER_LANES), F32)],
        compiler_params=_params(("parallel",)),
        name="merge_ln1_router",
    )(x, ya, yb, gate, p, *ws)


def _moe_body(start_ref, nblk_ref, x1_ref, base_ref, cmb_ref, drow_ref, dcol_ref, wg_ref, wu_ref, wd_ref,
              g2_ref, b2_ref, o_ref, xs_scr, cs_scr, acc_scr, *, lane0, ngroups, nper, blk):
    i = pl.program_id(0)
    e = pl.program_id(1)
    nslot, tm = xs_scr.shape[0], x1_ref.shape[0]
    eps = wg_ref.shape[0]

    @pl.when(e == 0)
    def _():
        slot = lax.broadcasted_iota(jnp.int32, (nslot, tm), 0)
        pm = (slot == drow_ref[...]).astype(BF16)
        xs_scr[...] = jnp.dot(pm, x1_ref[...], preferred_element_type=F32).astype(BF16)
        cs_scr[...] = _bdot(pm, cmb_ref[...])
        acc_scr[...] = jnp.zeros_like(acc_scr)

    grp = (e * eps) // nper
    row0 = start_ref[i * ngroups + grp]
    lane = lax.broadcasted_iota(jnp.int32, (blk, cs_scr.shape[1]), 1)

    def block(j, carry):
        rows = pl.ds(pl.multiple_of(row0 + j * blk, blk), blk)
        xb = xs_scr[rows, :]
        cs = cs_scr[rows, :]
        part = None
        for ee in range(eps):
            ce = jnp.sum(jnp.where(lane == e * eps + ee + lane0, cs, 0.0), axis=-1, keepdims=True)
            hg = jnp.dot(xb, wg_ref[ee], preferred_element_type=F32)
            hu = jnp.dot(xb, wu_ref[ee], preferred_element_type=F32)
            t = _bdot(hg * _sigmoid(hg) * hu * ce, wd_ref[ee])
            part = t if part is None else part + t
        acc_scr[rows, :] += part
        return carry

    lax.fori_loop(0, nblk_ref[i * ngroups + grp], block, 0)

    @pl.when(e == pl.num_programs(1) - 1)
    def _():
        slot_t = lax.broadcasted_iota(jnp.int32, (tm, nslot), 1)
        pmt = (slot_t == dcol_ref[...]).astype(BF16)
        ffn = _bdot(pmt, acc_scr[...])
        o_ref[...] = _layer_norm(base_ref[...] + ffn, g2_ref[...], b2_ref[...])


def _moe_dispatch(cmb, tm, ngroups, blk):
    t = cmb.shape[0]
    nt = t // tm
    g = cmb[:, 0].reshape(nt, tm)
    oh = (g[..., None] == jnp.arange(ngroups, dtype=F32)).astype(F32)
    before = (jnp.arange(tm)[None, :] < jnp.arange(tm)[:, None]).astype(F32)
    rank = jnp.einsum("ts,nsg->ntg", before, oh)
    nblk = (jnp.sum(oh, axis=1).astype(jnp.int32) + blk - 1) // blk
    start = (jnp.cumsum(nblk, axis=-1) - nblk) * blk
    dest = jnp.sum(oh * (rank + start[:, None, :].astype(F32)), axis=-1).astype(jnp.int32)
    return start.reshape(-1), nblk.reshape(-1), dest.reshape(nt, 1, tm), dest.reshape(t, 1)


def _moe(x1, base, cmb, prm, tm):
    t, d = base.shape
    _, ne, _, f = prm["w_gate"].shape
    layer = prm["layer"]
    ngroups, nper = prm["ngroups"], prm["nper"]
    eps = MOE_EXPERTS_PER_STEP
    blk = min(MOE_BLOCK_ROWS, tm)
    nslot = tm + ngroups * blk
    start, nblk, drow, dcol = _moe_dispatch(cmb, tm, ngroups, blk)
    row = lambda n: pl.BlockSpec((tm, n), lambda i, e, *_: (i, 0))
    wspec = lambda a, b: pl.BlockSpec((None, eps, a, b), lambda i, e, *_: (layer, e, 0, 0))
    vec = pl.BlockSpec((1, d), lambda i, e, *_: (0, 0))
    return pl.pallas_call(
        functools.partial(_moe_body, lane0=ngroups, ngroups=ngroups, nper=nper, blk=blk),
        grid_spec=pltpu.PrefetchScalarGridSpec(
            num_scalar_prefetch=2,
            grid=(t // tm, ne // eps),
            in_specs=[row(d), row(d), row(ROUTER_LANES),
                      pl.BlockSpec((None, 1, tm), lambda i, e, *_: (i, 0, 0)), row(1),
                      wspec(d, f), wspec(d, f), wspec(f, d), vec, vec],
            out_specs=row(d),
            scratch_shapes=[pltpu.VMEM((nslot, d), BF16), pltpu.VMEM((nslot, ROUTER_LANES), F32),
                            pltpu.VMEM((nslot, d), F32)]),
        out_shape=jax.ShapeDtypeStruct((t, d), F32),
        compiler_params=_params(("parallel", "arbitrary")),
        name="moe_ln2",
    )(start, nblk, x1, base, cmb, drow, dcol, prm["w_gate"], prm["w_up"], prm["w_down"],
      prm["ln2_g"], prm["ln2_b"])


def _tile(n, target):
    t = min(n, target)
    assert n % t == 0, (n, t)
    return t


def _layer(x, p, shift0, wkv0, s5r0, s5i0, v_first, lp):
    b, l, d = x.shape
    t = b * l
    s5w = lp["s5"]["d"].shape[-1]
    rwc = lp["rw"]["mu"].shape[-1]
    u, zr, gate = _inproj(x.reshape(t, d), lp["w_in"], lp["layer"], lp["b_gate"], s5w, rwc, _tile(t, 256))
    ya, s5r, s5i = _s5_mixer(u.reshape(b, l, s5w), s5r0, s5i0, lp["s5"], _tile(l, 256))
    zr3 = zr.reshape(b, l, rwc)
    tl = _tile(l, 256)
    r, k, v, lw, kkn, bv, g = _rw_pre(zr3, shift0, lp["rw"], v_first, tl)
    yb, wkv_new = _wkv(r, k, v, lw, kkn, bv, g, lp["rw"], wkv0, tl, min(WKV_CHUNK, l))
    x1, base, cmb = _merge(x.reshape(t, d), ya.reshape(t, s5w), yb.reshape(t, -1), gate,
                           p.reshape(t, -1), lp["mrg"], _tile(t, 256))
    out = _moe(x1, base, cmb, lp["moe"], _tile(t, 1024))
    shift_new = zr3[:, l - 1:, :]
    return out.reshape(b, l, d), shift_new, wkv_new, s5r, s5i, (v if v_first is None else v_first)


def _trunk(x, p, shift0, wkv0, s50, layers):
    b = x.shape[0]
    shifts, wkvs, s5s = [], [], []
    v_first = None
    for i, lp in enumerate(layers):
        ns = s50.shape[2] * s50.shape[3]
        s5r0 = s50[i, ..., 0].reshape(b, 1, ns)
        s5i0 = s50[i, ..., 1].reshape(b, 1, ns)
        x, sh, wk, s5r, s5i, v_first = _layer(x, p[i], shift0[i], wkv0[i], s5r0, s5i0, v_first, lp)
        shifts.append(sh)
        wkvs.append(wk)
        s5s.append(jnp.stack([s5r.reshape(s50.shape[1:4]), s5i.reshape(s50.shape[1:4])], axis=-1))
    return x, jnp.stack(shifts), jnp.stack(wkvs), jnp.stack(s5s)


def kernel(x_prompt, x_sample, state_shift, state_wkv, state_s5, p_prompt, p_sample, w_in, b_gate, s5_a_re, s5_a_im, s5_log_dt, s5_b_re, s5_b_im, s5_c_re, s5_c_im, s5_d, s5_w_glu, s5_b_glu, rw_mu, rw_w0, rw_w2, rw_a0, rw_a2, rw_g2, rw_v0, rw_v1, rw_v2, rw_k_k, rw_k_a, rw_r_k, rw_ln_g, rw_ln_b, w_pa, w_pb, w_o, ln1_g, ln1_b, w_rg, b_rg, w_re, b_re, w_gate, w_up, w_down, w_ple, w_pg, b_pg, ln2_g, ln2_b):
    depth, d = w_in.shape[0], w_in.shape[1]
    ngroups = w_rg.shape[-1]
    nper = w_re.shape[-1]
    nexp = ngroups * nper
    w_in_all = w_in.astype(BF16)
    wg_all, wu_all, wd_all = w_gate.astype(BF16), w_up.astype(BF16), w_down.astype(BF16)
    layers = []
    for i in range(depth):
        abr, abi, wbu, wy = _s5_discretize(s5_a_re[i], s5_a_im[i], s5_log_dt[i], s5_b_re[i], s5_b_im[i],
                                           s5_c_re[i], s5_c_im[i])
        s5 = dict(abr=abr, abi=abi, wbu=wbu, wy=wy, d=s5_d[i][None], wglu=s5_w_glu[i].astype(BF16),
                  bglu=s5_b_glu[i][None])
        rw = dict(mu=rw_mu[i][None], w0=rw_w0[i][None], w2=rw_w2[i].astype(BF16), a0=rw_a0[i][None],
                  a2=rw_a2[i].astype(BF16), g2=rw_g2[i].astype(BF16), k_k=rw_k_k[i][None],
                  k_a=rw_k_a[i][None], r_k=rw_r_k[i].reshape(1, -1), ln_g=rw_ln_g[i][None],
                  ln_b=rw_ln_b[i][None])
        if i > 0:
            rw.update(v0=rw_v0[i - 1][None], v1=rw_v1[i - 1].astype(BF16), v2=rw_v2[i - 1].astype(BF16))
        pad = ROUTER_LANES - ngroups - nexp
        w_router = jnp.concatenate([w_rg[i], w_re[i].reshape(d, nexp), jnp.zeros((d, pad), F32)], axis=1)
        b_router = jnp.concatenate([b_rg[i], b_re[i].reshape(nexp), jnp.zeros((pad,), F32)])[None]
        mrg = dict(w_pa=w_pa[i].astype(BF16), w_pb=w_pb[i].astype(BF16), w_o=w_o[i].astype(BF16),
                   ln1_g=ln1_g[i][None], ln1_b=ln1_b[i][None], w_pg=w_pg[i].astype(BF16),
                   b_pg=b_pg[i][None], w_ple=w_ple[i].astype(BF16), w_router=w_router,
                   b_router=b_router, ngroups=ngroups, nper=nper)
        moe = dict(w_gate=wg_all, w_up=wu_all, w_down=wd_all, layer=i, ln2_g=ln2_g[i][None],
                   ln2_b=ln2_b[i][None], ngroups=ngroups, nper=nper)
        layers.append(dict(w_in=w_in_all, layer=i, b_gate=b_gate[i][None], s5=s5, rw=rw, mrg=mrg,
                           moe=moe))

    bp = x_prompt.shape[0]
    dt = x_prompt.dtype
    zero_shift = jnp.zeros((depth, bp) + state_shift.shape[2:], dt)
    zero_wkv = jnp.zeros((depth, bp) + state_wkv.shape[2:], dt)
    zero_s5 = jnp.zeros((depth, bp) + state_s5.shape[2:], dt)
    y_p, sh_p, wkv_p, s5_p = _trunk(x_prompt, p_prompt, zero_shift, zero_wkv, zero_s5, layers)
    y_s, sh_s, wkv_s, s5_s = _trunk(x_sample, p_sample, state_shift, state_wkv, state_s5, layers)
    return (y_p, y_s, sh_p, wkv_p, s5_p, sh_s, wkv_s, s5_s)
```

```python
import functools
import math

import jax
import jax.numpy as jnp
from jax import lax
from jax.experimental import pallas as pl
from jax.experimental.pallas import tpu as pltpu

F32 = jnp.float32
BF16 = jnp.bfloat16

DEPTH = 2
ALPHA = (2 * DEPTH) ** 0.25
LN_EPS = 1e-5
GN_EPS = 64e-5

S5_GROUP = 16
S5_STATE = 64
S5_GROUPS_PER_BLOCK = 8
RW_HEAD = 64
DECAY_SCALE = math.exp(-0.5)
ROUTER_LANES = 128
WKV_CHUNK = 64
VMEM_LIMIT = 56 * 1024 * 1024
WKV_TERMS = {"nt": 1, "inv": 1, "out": 1, "state": 1}
WKV_CHUNKS_PER_ITER = 4
MOE_EXPERTS_PER_STEP = 2
S5_SLAB_ROWS = 256
INPROJ_ROW_SLABS = 2
MERGE_ROW_SLABS = 2
MOE_BLOCK_ROWS = 128


def _bdot(a, b):
    return jnp.dot(a.astype(BF16), b.astype(BF16), preferred_element_type=F32)


def _split(x, n):
    parts = []
    rem = x
    for j in range(n):
        p = rem.astype(BF16)
        parts.append(p)
        if j + 1 < n:
            rem = rem - p.astype(F32)
    return parts


_NN = (((1,), (0,)), ((), ()))
_NT = (((1,), (1,)), ((), ()))
_TN = (((0,), (0,)), ((), ()))


def _mdot(a_parts, b_parts, dims=_NN):
    order = max(len(a_parts), len(b_parts))
    acc = None
    for ia, pa in enumerate(a_parts):
        for ib, pb in enumerate(b_parts):
            if ia + ib < order:
                t = lax.dot_general(pa, pb, dims, preferred_element_type=F32)
                acc = t if acc is None else acc + t
    return acc


def _head_sum(x, hd, terms=1):
    n = x.shape[-1]
    same = (lax.broadcasted_iota(jnp.int32, (n, n), 0) // hd
            == lax.broadcasted_iota(jnp.int32, (n, n), 1) // hd)
    return _mdot(_split(x, terms), [same.astype(BF16)])


def _sigmoid(x):
    return 1.0 / (1.0 + jnp.exp(-x))


def _layer_norm(h, g, b):
    mu = jnp.mean(h, axis=-1, keepdims=True)
    d = h - mu
    var = jnp.mean(d * d, axis=-1, keepdims=True)
    return d * lax.rsqrt(var + LN_EPS) * g + b


def _params(sem):
    return pltpu.CompilerParams(dimension_semantics=sem, vmem_limit_bytes=VMEM_LIMIT)


def _full(shape):
    n = len(shape)
    return pl.BlockSpec(shape, lambda *_: (0,) * n)


def _inproj_body(*refs, s5w, rwc, w, dr, ar_, hd, seq_len, nslab, has_vmix):
    (x_ref, win_ref, bg_ref, sh0_ref, mu_ref, w0_ref, w2_ref, a0_ref, a2_ref, g2_ref, kk_ref,
     ka_ref) = refs[:12]
    refs = refs[12:]
    if has_vmix:
        vf_ref, v0_ref, v1_ref, v2_ref = refs[:4]
        refs = refs[4:]
    u_ref, gate_ref, r_ref, k_ref, v_ref, lw_ref, kkn_ref, bv_ref, g_ref, zlast_ref, carry = refs

    i = pl.program_id(0)
    tm = x_ref.shape[0]
    rs = tm // nslab
    slabs = [slice(j * rs, (j + 1) * rs) for j in range(nslab)]
    xb = [x_ref[s, :].astype(BF16) for s in slabs]
    for s, xv in zip(slabs, xb):
        u_ref[s, :] = jnp.dot(xv, win_ref[:, :s5w], preferred_element_type=F32)
    zr = [jnp.dot(xv, win_ref[:, s5w:s5w + rwc], preferred_element_type=F32) for xv in xb]
    for s, xv in zip(slabs, xb):
        zg = jnp.dot(xv, win_ref[:, s5w + rwc:], preferred_element_type=F32)
        gate_ref[s, :] = _sigmoid(zg + bg_ref[...])

    row = lax.broadcasted_iota(jnp.int32, (rs, rwc), 0)
    @pl.when(i == 0)
    def _():
        carry[...] = jnp.zeros_like(carry)

    zs = []
    for j, z in enumerate(zr):
        if j > 0:
            before = zr[j - 1][rs - 1:rs, :]
        elif tm <= seq_len:
            before = jnp.where((i * tm) % seq_len == 0, sh0_ref[0], carry[...])
        else:
            before = carry[...]
        prev = jnp.where(row == 0, before, pltpu.roll(z, 1, axis=0))
        for bb in range(tm // seq_len):
            if j * rs <= bb * seq_len < (j + 1) * rs:
                prev = jnp.where(row == bb * seq_len - j * rs, sh0_ref[bb], prev)
        zs.append(z + mu_ref[...] * (prev - z))
    carry[...] = zr[-1][rs - 1:rs, :]
    if tm <= seq_len:
        zlast_ref[0] = zr[-1][rs - 1:rs, :]
    else:
        for bb in range(tm // seq_len):
            last = (bb + 1) * seq_len - 1
            zlast_ref[bb] = zr[last // rs][last % rs:last % rs + 1, :]

    c3 = 3 * w
    wl = [w0_ref[...] + _bdot(jnp.tanh(z[:, c3:c3 + dr]), w2_ref[...]) for z in zs]
    a = [_sigmoid(a0_ref[...] + _bdot(z[:, c3 + dr:c3 + dr + ar_], a2_ref[...])) for z in zs]
    g = [_bdot(_sigmoid(z[:, c3 + dr + ar_:]), g2_ref[...]) for z in zs]
    v = [z[:, 2 * w:c3] for z in zs]
    if has_vmix:
        mix = [_sigmoid(v0_ref[...] + _bdot(_bdot(vv, v1_ref[...]), v2_ref[...])) for vv in v]
        v = [vv + (vf_ref[s, :] - vv) * m for s, vv, m in zip(slabs, v, mix)]
    kk = [z[:, w:2 * w] * kk_ref[...] for z in zs]
    ss = [_head_sum(q * q, hd) for q in kk]
    for j, s in enumerate(slabs):
        lw_ref[s, :] = -DECAY_SCALE * _sigmoid(wl[j])
        g_ref[s, :] = g[j]
        r_ref[s, :] = zs[j][:, :w]
        v_ref[s, :] = v[j]
        kkn = kk[j] * lax.rsqrt(jnp.maximum(ss[j], 1e-24))
        kkn_ref[s, :] = kkn
        bv_ref[s, :] = kkn * a[j]
        k_ref[s, :] = zs[j][:, w:2 * w] * (1.0 + (a[j] - 1.0) * ka_ref[...])


def _inproj(x, w_in, layer, b_gate, shift0, prm, v_first, seq_len, s5w, tm):
    t, d = x.shape
    n_in = w_in.shape[2]
    rwc = prm["mu"].shape[-1]
    w = prm["w0"].shape[-1]
    ng = n_in - s5w - rwc
    has_vmix = v_first is not None
    nseq = max(1, tm // seq_len)
    row = lambda n: pl.BlockSpec((tm, n), lambda i: (i, 0))
    vec = lambda n: _full((1, n))
    ins = [x, w_in, b_gate, shift0, prm["mu"], prm["w0"], prm["w2"], prm["a0"], prm["a2"], prm["g2"],
           prm["k_k"], prm["k_a"]]
    specs = [row(d), pl.BlockSpec((None, d, n_in), lambda i: (layer, 0, 0)), vec(ng),
             pl.BlockSpec((nseq, 1, rwc), lambda i: ((i * tm) // (seq_len * nseq), 0, 0)), vec(rwc),
             vec(w), _full(prm["w2"].shape), vec(w), _full(prm["a2"].shape), _full(prm["g2"].shape),
             vec(w), vec(w)]
    if has_vmix:
        ins += [v_first, prm["v0"], prm["v1"], prm["v2"]]
        specs += [row(w), vec(w), _full(prm["v1"].shape), _full(prm["v2"].shape)]
    ntile = t // tm
    outs = pl.pallas_call(
        functools.partial(_inproj_body, s5w=s5w, rwc=rwc, w=w, dr=prm["w2"].shape[0],
                          ar_=prm["a2"].shape[0], hd=RW_HEAD, seq_len=seq_len, nslab=INPROJ_ROW_SLABS,
                          has_vmix=has_vmix),
        grid=(ntile,),
        in_specs=specs,
        out_specs=[row(s5w), row(ng)] + [row(w)] * 7
                  + [pl.BlockSpec((nseq, 1, rwc), lambda i: (i, 0, 0))],
        out_shape=[jax.ShapeDtypeStruct((t, s5w), F32), jax.ShapeDtypeStruct((t, ng), F32)]
                  + [jax.ShapeDtypeStruct((t, w), F32)] * 7
                  + [jax.ShapeDtypeStruct((ntile * nseq, 1, rwc), F32)],
        scratch_shapes=[pltpu.VMEM((1, rwc), F32)],
        compiler_params=_params(("arbitrary",)),
        name="inproj_premix",
    )(*ins)
    zlast = outs[-1].reshape(t // seq_len, -1, 1, rwc)[:, -1]
    return list(outs[:-1]) + [zlast]


def _s5_body(u_ref, h0r_ref, h0i_ref, abr_ref, abi_ref, wbu_ref, wy_ref, d_ref, wglu_ref, bglu_ref,
             y_ref, hlr_ref, hli_ref, bur, bui, hr, hi, str_, sti, *, nb, tl, sub, nblk, cb, sb):
    i = pl.program_id(0)

    @pl.when(i == 0)
    def _():
        str_[...] = h0r_ref[...]
        sti[...] = h0i_ref[...]

    slabs = [(bi, s0) for s0 in range(0, tl, sub) for bi in range(nb)]
    for bi, s0 in slabs:
        u = u_ref[bi, s0:s0 + sub, :]
        for j in range(nblk):
            bu = _bdot(u[:, cb * j:cb * (j + 1)], wbu_ref[j])
            bur[bi, s0:s0 + sub, sb * j:sb * (j + 1)] = bu[:, :sb]
            bui[bi, s0:s0 + sub, sb * j:sb * (j + 1)] = bu[:, sb:]

    ar = abr_ref[...]
    ai = abi_ref[...]
    state = [(str_[bi], sti[bi]) for bi in range(nb)]
    for s0 in range(0, tl, sub):
        for t in range(s0, s0 + sub):
            for bi in range(nb):
                pr, pi = state[bi]
                nr = ar * pr - ai * pi + bur[bi, t:t + 1, :]
                ni = ar * pi + ai * pr + bui[bi, t:t + 1, :]
                hr[bi, t:t + 1, :] = nr
                hi[bi, t:t + 1, :] = ni
                state[bi] = (nr, ni)
        for bi in range(nb):
            u = u_ref[bi, s0:s0 + sub, :]
            ys = [_bdot(hr[bi, s0:s0 + sub, sb * j:sb * (j + 1)], wy_ref[j, :sb, :])
                  + _bdot(hi[bi, s0:s0 + sub, sb * j:sb * (j + 1)], wy_ref[j, sb:, :])
                  for j in range(nblk)]
            y = jax.nn.gelu(jnp.concatenate(ys, axis=1) + d_ref[...] * u)
            y_ref[bi, s0:s0 + sub, :] = y * _sigmoid(_bdot(y, wglu_ref[...]) + bglu_ref[...])
    for bi in range(nb):
        str_[bi] = state[bi][0]
        sti[bi] = state[bi][1]
        hlr_ref[bi] = state[bi][0]
        hli_ref[bi] = state[bi][1]


def _s5_mixer(u, h0r, h0i, prm, tl):
    b, l, w = u.shape
    ns = h0r.shape[-1]
    nblk = w // (S5_GROUP * S5_GROUPS_PER_BLOCK)
    cb = w // nblk
    sb = ns // nblk
    seq = pl.BlockSpec((b, tl, w), lambda i: (0, i, 0))
    st = _full((b, 1, ns))
    return pl.pallas_call(
        functools.partial(_s5_body, nb=b, tl=tl, sub=min(S5_SLAB_ROWS, tl), nblk=nblk, cb=cb, sb=sb),
        grid=(l // tl,),
        in_specs=[seq, st, st, _full((1, ns)), _full((1, ns)), _full(prm["wbu"].shape),
                  _full(prm["wy"].shape), _full((1, w)), _full((w, w)), _full((1, w))],
        out_specs=[seq, st, st],
        out_shape=[jax.ShapeDtypeStruct((b, l, w), F32), jax.ShapeDtypeStruct((b, 1, ns), F32),
                   jax.ShapeDtypeStruct((b, 1, ns), F32)],
        scratch_shapes=[pltpu.VMEM((b, tl, ns), F32)] * 4 + [pltpu.VMEM((b, 1, ns), F32)] * 2,
        compiler_params=_params(("arbitrary",)),
        name="s5_mixer",
    )(u, h0r, h0i, prm["abr"], prm["abi"], prm["wbu"], prm["wy"], prm["d"], prm["wglu"], prm["bglu"])


def _s5_discretize(a_re, a_im, log_dt, b_re, b_im, c_re, c_im):
    g, p = a_re.shape
    c = b_re.shape[-1]
    gb = S5_GROUPS_PER_BLOCK
    nblk = g // gb
    dt = jnp.exp(log_dt)[:, None]
    mag = jnp.exp(a_re * dt)
    ab_re = mag * jnp.cos(a_im * dt)
    ab_im = mag * jnp.sin(a_im * dt)
    den = a_re * a_re + a_im * a_im
    n_re = ab_re - 1.0
    k_re = (n_re * a_re + ab_im * a_im) / den
    k_im = (ab_im * a_re - n_re * a_im) / den
    bb_re = k_re[..., None] * b_re - k_im[..., None] * b_im
    bb_im = k_re[..., None] * b_im + k_im[..., None] * b_re
    eye = jnp.eye(gb, dtype=F32)

    def pack_in(bb):
        return jnp.einsum("jgpc,gh->jgchp", bb.reshape(nblk, gb, p, c), eye).reshape(nblk, gb * c, gb * p)

    def pack_out(cc):
        return jnp.einsum("jgcp,gh->jgphc", cc.reshape(nblk, gb, c, p), eye).reshape(nblk, gb * p, gb * c)

    wbu = jnp.concatenate([pack_in(bb_re), pack_in(bb_im)], axis=-1).astype(BF16)
    wy = jnp.concatenate([pack_out(c_re), -pack_out(c_im)], axis=1).astype(BF16)
    return ab_re.reshape(1, g * p), ab_im.reshape(1, g * p), wbu, wy


def _wkv_body(r_ref, k_ref, v_ref, lw_ref, kkn_ref, bv_ref, g_ref, rk_ref, lng_ref, lnb_ref, s0_ref,
              y_ref, sl_ref, s_scr, at_scr, rt_scr, kt_scr, bt_scr, kd_scr, bd_scr, ce_scr, y_scr,
              *, tl, c, cpi, nh, hd):
    i = pl.program_id(1)

    @pl.when(i == 0)
    def _():
        s_scr[...] = s0_ref[...]

    rowi = lax.broadcasted_iota(jnp.int32, (tl, tl), 0)
    coli = lax.broadcasted_iota(jnp.int32, (tl, tl), 1)
    same_chunk = (rowi // c) == (coli // c)
    lw = lw_ref[...]
    cl = _mdot([(same_chunk & (coli <= rowi)).astype(BF16)], _split(lw, 3))
    ce = jnp.concatenate([jnp.broadcast_to(cl[j * c + c - 1:(j + 1) * c, :], (c, cl.shape[1]))
                          for j in range(tl // c)], axis=0)
    kkn = kkn_ref[...]
    bv = bv_ref[...]
    k = k_ref[...]
    einv = jnp.exp(-cl)
    to_end = jnp.exp(ce - cl)
    at_scr[...] = -kkn * jnp.exp(cl - lw)
    rt_scr[...] = r_ref[...] * jnp.exp(cl)
    kt_scr[...] = k * einv
    bt_scr[...] = bv * einv
    kd_scr[...] = k * to_end
    bd_scr[...] = bv * to_end
    ce_scr[...] = ce

    ri = lax.broadcasted_iota(jnp.int32, (c, c), 0)
    ci_ = lax.broadcasted_iota(jnp.int32, (c, c), 1)
    lower = ci_ <= ri
    strict = ci_ < ri
    levels = int(math.log2(c))
    sp = lambda x, cls: _split(x, WKV_TERMS[cls])

    def chunk_group(gi, carry):
        base = pl.multiple_of(gi * (cpi * c), cpi * c)
        units = [(cc, h) for cc in range(cpi) for h in range(nh)]
        rows = [pl.ds(base + cc * c, c) for cc, _ in units]
        hsl = [slice(h * hd, (h + 1) * hd) for _, h in units]
        un = range(len(units))
        at = [at_scr[rows[u], hsl[u]] for u in un]
        rt = [rt_scr[rows[u], hsl[u]] for u in un]
        lhs = [sp(jnp.concatenate([at[u], rt[u]], axis=0), "nt") for u in un]
        rhs = [sp(jnp.concatenate([bt_scr[rows[u], hsl[u]], kt_scr[rows[u], hsl[u]]], axis=0), "nt")
               for u in un]
        pm = [_mdot(lhs[u], rhs[u], _NT) for u in un]
        n_ab = [jnp.where(strict, pm[u][:c, :c], 0.0) for u in un]
        a_ak = [sp(jnp.where(strict, pm[u][:c, c:], 0.0), "out") for u in un]
        a_rb = [sp(jnp.where(lower, pm[u][c:, :c], 0.0), "out") for u in un]
        a_rk = [sp(jnp.where(lower, pm[u][c:, c:], 0.0), "out") for u in un]
        v_p = [sp(v_ref[rows[u], hsl[u]], "out") for u in un]

        x = [jnp.concatenate([at[u], _mdot(a_ak[u], v_p[u])], axis=1) for u in un]
        rkv = [_mdot(a_rk[u], v_p[u]) for u in un]
        pw = n_ab
        for lv in range(levels):
            pw_p = [sp(pw[u], "inv") for u in un]
            x = [x[u] + _mdot(pw_p[u], sp(x[u], "inv")) for u in un]
            if lv + 1 < levels:
                pw = [_mdot(pw_p[u], pw_p[u]) for u in un]
        x_p = [sp(x[u], "out") for u in un]
        qy = [jnp.concatenate([rt[u], rkv[u]], axis=1) + _mdot(a_rb[u], x_p[u]) for u in un]
        mz = [_mdot(x_p[u], sp(bd_scr[rows[u], hsl[u]], "out"), _TN) for u in un]
        zt = [mz[u][hd:, :] + _mdot(v_p[u], sp(kd_scr[rows[u], hsl[u]], "out"), _TN) for u in un]

        s_cur = [s_scr[h] for h in range(nh)]
        for cc in range(cpi):
            g_end = jnp.exp(ce_scr[pl.ds(base + cc * c, 1), :])
            s_p = [sp(s_cur[h], "state") for h in range(nh)]
            for h in range(nh):
                u = cc * nh + h
                y_scr[rows[u], hsl[u]] = _mdot(sp(qy[u][:, :hd], "state"), s_p[h], _NT) + qy[u][:, hd:]
            s_cur = [s_cur[h] * g_end[:, hsl[h]] + _mdot(s_p[h], sp(mz[cc * nh + h][:hd, :], "state"))
                     + zt[cc * nh + h] for h in range(nh)]
        for h in range(nh):
            s_scr[h] = s_cur[h]
        return carry

    lax.fori_loop(0, tl // (cpi * c), chunk_group, 0)

    @pl.when(i == pl.num_programs(1) - 1)
    def _():
        sl_ref[...] = s_scr[...]

    y = y_scr[...]
    yc = y - _head_sum(y, hd) * (1.0 / hd)
    yv = _head_sum(yc * yc, hd) * (1.0 / hd)
    yn = yc * lax.rsqrt(yv + GN_EPS) * lng_ref[...] + lnb_ref[...]
    bonus = _head_sum(r_ref[...] * k_ref[...] * rk_ref[...], hd) * v_ref[...]
    y_ref[...] = (yn + bonus) * g_ref[...]


def _wkv(r, k, v, lw, kkn, bv, g, prm, s0, tl, c):
    b, l, w = r.shape
    nh, hd = s0.shape[1], s0.shape[2]
    seq = pl.BlockSpec((None, tl, w), lambda bi, i: (bi, i, 0))
    st = pl.BlockSpec((None, nh, hd, hd), lambda bi, i: (bi, 0, 0, 0))
    vec = _full((1, w))
    return pl.pallas_call(
        functools.partial(_wkv_body, tl=tl, c=c, cpi=min(WKV_CHUNKS_PER_ITER, tl // c), nh=nh, hd=hd),
        grid=(b, l // tl),
        in_specs=[seq] * 7 + [vec, vec, vec, st],
        out_specs=[seq, st],
        out_shape=[jax.ShapeDtypeStruct((b, l, w), F32), jax.ShapeDtypeStruct(s0.shape, F32)],
        scratch_shapes=[pltpu.VMEM((nh, hd, hd), F32)] + [pltpu.VMEM((tl, w), F32)] * 8,
        compiler_params=_params(("parallel", "arbitrary")),
        name="wkv",
    )(r, k, v, lw, kkn, bv, g, prm["r_k"], prm["ln_g"], prm["ln_b"], s0)


def _merge_body(x_ref, ya_ref, yb_ref, gate_ref, p_ref, wpa_ref, wpb_ref, wo_ref, g1_ref, b1_ref,
                wpg_ref, bpg_ref, wple_ref, wr_ref, br_ref, x1_ref, base_ref, cmb_ref,
                *, d, ngroups, nper, nsplit):
    rs = x_ref.shape[0] // nsplit
    slabs = [slice(j * rs, (j + 1) * rs) for j in range(nsplit)]
    pa = [_bdot(ya_ref[s, :], wpa_ref[...]) for s in slabs]
    pb = [_bdot(yb_ref[s, :], wpb_ref[...]) for s in slabs]
    merged = [gate_ref[s, :d] * a + gate_ref[s, d:] * b for s, a, b in zip(slabs, pa, pb)]
    h = [ALPHA * x_ref[s, :] + _bdot(m, wo_ref[...]) for s, m in zip(slabs, merged)]
    x1s = [_layer_norm(v, g1_ref[...], b1_ref[...]) for v in h]
    pg = [_bdot(v, wpg_ref[...]) for v in x1s]
    pe = [_bdot(p_ref[s, :], wple_ref[...]) for s in slabs]
    wr_p = _split(wr_ref[...], 2)
    lg = [_mdot(_split(v, 2), wr_p) + br_ref[...] for v in x1s]
    for s, v, g_, e_, l_ in zip(slabs, x1s, pg, pe, lg):
        x1_ref[s, :] = v.astype(BF16)
        base_ref[s, :] = ALPHA * v + _sigmoid(g_ + bpg_ref[...]) * e_
        cmb_ref[s, :] = _route(l_, ngroups, nper)


def _route(logits, ngroups, nper):
    lane = lax.broadcasted_iota(jnp.int32, logits.shape, 1)
    neg = -jnp.inf
    big = logits.shape[1]
    first_of = lambda hit: jnp.min(jnp.where(hit, lane, big), axis=-1, keepdims=True)
    gl = jnp.where(lane < ngroups, logits, neg)
    gmax = jnp.max(gl, axis=-1, keepdims=True)
    g_sel = first_of(gl == gmax)
    g_w = 1.0 / jnp.sum(jnp.exp(gl - gmax), axis=-1, keepdims=True)
    lo = ngroups + nper * g_sel
    el = jnp.where((lane >= lo) & (lane < lo + nper), logits, neg)
    m1 = jnp.max(el, axis=-1, keepdims=True)
    i1 = first_of(el == m1)
    el2 = jnp.where(lane == i1, neg, el)
    m2 = jnp.max(el2, axis=-1, keepdims=True)
    i2 = first_of(el2 == m2)
    e2 = jnp.exp(m2 - m1)
    w1 = g_w / (1.0 + e2)
    w2 = g_w * e2 / (1.0 + e2)
    return (jnp.where(lane == i1, w1, 0.0) + jnp.where(lane == i2, w2, 0.0)
            + jnp.where(lane == 0, g_sel.astype(F32), 0.0))


def _merge(x, ya, yb, gate, p, prm, tm):
    t, d = x.shape
    row = lambda n: pl.BlockSpec((tm, n), lambda i: (i, 0))
    ws = [prm["w_pa"], prm["w_pb"], prm["w_o"], prm["ln1_g"], prm["ln1_b"], prm["w_pg"], prm["b_pg"],
          prm["w_ple"], prm["w_router"], prm["b_router"]]
    return pl.pallas_call(
        functools.partial(_merge_body, d=d, ngroups=prm["ngroups"], nper=prm["nper"],
                          nsplit=MERGE_ROW_SLABS),
        grid=(t // tm,),
        in_specs=[row(d), row(ya.shape[1]), row(yb.shape[1]), row(gate.shape[1]), row(p.shape[1])]
                 + [_full(a.shape) for a in ws],
        out_specs=[row(d), row(d), row(ROUTER_LANES)],
        out_shape=[jax.ShapeDtypeStruct((t, d), BF16), jax.ShapeDtypeStruct((t, d), F32),
                   jax.ShapeDtypeStruct((t, ROUTER_LANES), F32)],
        compiler_params=_params(("parallel",)),
        name="merge_ln1_router",
    )(x, ya, yb, gate, p, *ws)


def _moe_body(start_ref, nblk_ref, x1_ref, base_ref, cmb_ref, drow_ref, dcol_ref, wg_ref, wu_ref, wd_ref,
              g2_ref, b2_ref, o_ref, xs_scr, cs_scr, acc_scr, *, lane0, ngroups, nper, blk):
    i = pl.program_id(0)
    e = pl.program_id(1)
    nslot, tm = xs_scr.shape[0], x1_ref.shape[0]
    eps = wg_ref.shape[0]

    @pl.when(e == 0)
    def _():
        slot = lax.broadcasted_iota(jnp.int32, (nslot, tm), 0)
        pm = (slot == drow_ref[...]).astype(BF16)
        xs_scr[...] = jnp.dot(pm, x1_ref[...], preferred_element_type=F32).astype(BF16)
        cs_scr[...] = _bdot(pm, cmb_ref[...])
        acc_scr[...] = jnp.zeros_like(acc_scr)

    grp = (e * eps) // nper
    row0 = start_ref[i * ngroups + grp]
    lane = lax.broadcasted_iota(jnp.int32, (blk, cs_scr.shape[1]), 1)

    def block(j, carry):
        rows = pl.ds(pl.multiple_of(row0 + j * blk, blk), blk)
        xb = xs_scr[rows, :]
        cs = cs_scr[rows, :]
        part = None
        for ee in range(eps):
            ce = jnp.sum(jnp.where(lane == e * eps + ee + lane0, cs, 0.0), axis=-1, keepdims=True)
            hg = jnp.dot(xb, wg_ref[ee], preferred_element_type=F32)
            hu = jnp.dot(xb, wu_ref[ee], preferred_element_type=F32)
            t = _bdot(hg * _sigmoid(hg) * hu * ce, wd_ref[ee])
            part = t if part is None else part + t
        acc_scr[rows, :] += part
        return carry

    lax.fori_loop(0, nblk_ref[i * ngroups + grp], block, 0)

    @pl.when(e == pl.num_programs(1) - 1)
    def _():
        slot_t = lax.broadcasted_iota(jnp.int32, (tm, nslot), 1)
        pmt = (slot_t == dcol_ref[...]).astype(BF16)
        ffn = _bdot(pmt, acc_scr[...])
        o_ref[...] = _layer_norm(base_ref[...] + ffn, g2_ref[...], b2_ref[...])


def _moe_dispatch(cmb, tm, ngroups, blk):
    t = cmb.shape[0]
    nt = t // tm
    g = cmb[:, 0].reshape(nt, tm)
    oh = (g[..., None] == jnp.arange(ngroups, dtype=F32)).astype(F32)
    before = (jnp.arange(tm)[None, :] < jnp.arange(tm)[:, None]).astype(F32)
    rank = jnp.einsum("ts,nsg->ntg", before, oh)
    nblk = (jnp.sum(oh, axis=1).astype(jnp.int32) + blk - 1) // blk
    start = (jnp.cumsum(nblk, axis=-1) - nblk) * blk
    dest = jnp.sum(oh * (rank + start[:, None, :].astype(F32)), axis=-1).astype(jnp.int32)
    return start.reshape(-1), nblk.reshape(-1), dest.reshape(nt, 1, tm), dest.reshape(t, 1)


def _moe(x1, base, cmb, prm, tm):
    t, d = base.shape
    _, ne, _, f = prm["w_gate"].shape
    layer = prm["layer"]
    ngroups, nper = prm["ngroups"], prm["nper"]
    eps = MOE_EXPERTS_PER_STEP
    blk = min(MOE_BLOCK_ROWS, tm)
    nslot = tm + ngroups * blk
    start, nblk, drow, dcol = _moe_dispatch(cmb, tm, ngroups, blk)
    row = lambda n: pl.BlockSpec((tm, n), lambda i, e, *_: (i, 0))
    wspec = lambda a, b: pl.BlockSpec((None, eps, a, b), lambda i, e, *_: (layer, e, 0, 0))
    vec = pl.BlockSpec((1, d), lambda i, e, *_: (0, 0))
    return pl.pallas_call(
        functools.partial(_moe_body, lane0=ngroups, ngroups=ngroups, nper=nper, blk=blk),
        grid_spec=pltpu.PrefetchScalarGridSpec(
            num_scalar_prefetch=2,
            grid=(t // tm, ne // eps),
            in_specs=[row(d), row(d), row(ROUTER_LANES),
                      pl.BlockSpec((None, 1, tm), lambda i, e, *_: (i, 0, 0)), row(1),
                      wspec(d, f), wspec(d, f), wspec(f, d), vec, vec],
            out_specs=row(d),
            scratch_shapes=[pltpu.VMEM((nslot, d), BF16), pltpu.VMEM((nslot, ROUTER_LANES), F32),
                            pltpu.VMEM((nslot, d), F32)]),
        out_shape=jax.ShapeDtypeStruct((t, d), F32),
        compiler_params=_params(("parallel", "arbitrary")),
        name="moe_ln2",
    )(start, nblk, x1, base, cmb, drow, dcol, prm["w_gate"], prm["w_up"], prm["w_down"],
      prm["ln2_g"], prm["ln2_b"])


def _tile(n, target):
    t = min(n, target)
    assert n % t == 0, (n, t)
    return t


def _layer(x, p, shift0, wkv0, s5r0, s5i0, v_first, lp):
    b, l, d = x.shape
    t = b * l
    s5w = lp["s5"]["d"].shape[-1]
    vf = None if v_first is None else v_first.reshape(t, -1)
    u, gate, r, k, v, lw, kkn, bv, g, shift_new = _inproj(
        x.reshape(t, d), lp["w_in"], lp["layer"], lp["b_gate"], shift0, lp["rw"], vf, l, s5w, _tile(t, 256))
    ya, s5r, s5i = _s5_mixer(u.reshape(b, l, s5w), s5r0, s5i0, lp["s5"], _tile(l, 256))
    seq3 = lambda a: a.reshape(b, l, -1)
    yb, wkv_new = _wkv(seq3(r), seq3(k), seq3(v), seq3(lw), seq3(kkn), seq3(bv), seq3(g), lp["rw"], wkv0,
                       _tile(l, 256), min(WKV_CHUNK, l))
    x1, base, cmb = _merge(x.reshape(t, d), ya.reshape(t, s5w), yb.reshape(t, -1), gate,
                           p.reshape(t, -1), lp["mrg"], _tile(t, 256))
    out = _moe(x1, base, cmb, lp["moe"], _tile(t, 1024))
    return out.reshape(b, l, d), shift_new, wkv_new, s5r, s5i, (seq3(v) if v_first is None else v_first)


def _trunk(x, p, shift0, wkv0, s50, layers):
    b = x.shape[0]
    shifts, wkvs, s5s = [], [], []
    v_first = None
    for i, lp in enumerate(layers):
        ns = s50.shape[2] * s50.shape[3]
        s5r0 = s50[i, ..., 0].reshape(b, 1, ns)
        s5i0 = s50[i, ..., 1].reshape(b, 1, ns)
        x, sh, wk, s5r, s5i, v_first = _layer(x, p[i], shift0[i], wkv0[i], s5r0, s5i0, v_first, lp)
        shifts.append(sh)
        wkvs.append(wk)
        s5s.append(jnp.stack([s5r.reshape(s50.shape[1:4]), s5i.reshape(s50.shape[1:4])], axis=-1))
    return x, jnp.stack(shifts), jnp.stack(wkvs), jnp.stack(s5s)


def kernel(x_prompt, x_sample, state_shift, state_wkv, state_s5, p_prompt, p_sample, w_in, b_gate, s5_a_re, s5_a_im, s5_log_dt, s5_b_re, s5_b_im, s5_c_re, s5_c_im, s5_d, s5_w_glu, s5_b_glu, rw_mu, rw_w0, rw_w2, rw_a0, rw_a2, rw_g2, rw_v0, rw_v1, rw_v2, rw_k_k, rw_k_a, rw_r_k, rw_ln_g, rw_ln_b, w_pa, w_pb, w_o, ln1_g, ln1_b, w_rg, b_rg, w_re, b_re, w_gate, w_up, w_down, w_ple, w_pg, b_pg, ln2_g, ln2_b):
    depth, d = w_in.shape[0], w_in.shape[1]
    ngroups = w_rg.shape[-1]
    nper = w_re.shape[-1]
    nexp = ngroups * nper
    w_in_all = w_in.astype(BF16)
    wg_all, wu_all, wd_all = w_gate.astype(BF16), w_up.astype(BF16), w_down.astype(BF16)
    layers = []
    for i in range(depth):
        abr, abi, wbu, wy = _s5_discretize(s5_a_re[i], s5_a_im[i], s5_log_dt[i], s5_b_re[i], s5_b_im[i],
                                           s5_c_re[i], s5_c_im[i])
        s5 = dict(abr=abr, abi=abi, wbu=wbu, wy=wy, d=s5_d[i][None], wglu=s5_w_glu[i].astype(BF16),
                  bglu=s5_b_glu[i][None])
        rw = dict(mu=rw_mu[i][None], w0=rw_w0[i][None], w2=rw_w2[i].astype(BF16), a0=rw_a0[i][None],
                  a2=rw_a2[i].astype(BF16), g2=rw_g2[i].astype(BF16), k_k=rw_k_k[i][None],
                  k_a=rw_k_a[i][None], r_k=rw_r_k[i].reshape(1, -1), ln_g=rw_ln_g[i][None],
                  ln_b=rw_ln_b[i][None])
        if i > 0:
            rw.update(v0=rw_v0[i - 1][None], v1=rw_v1[i - 1].astype(BF16), v2=rw_v2[i - 1].astype(BF16))
        pad = ROUTER_LANES - ngroups - nexp
        w_router = jnp.concatenate([w_rg[i], w_re[i].reshape(d, nexp), jnp.zeros((d, pad), F32)], axis=1)
        b_router = jnp.concatenate([b_rg[i], b_re[i].reshape(nexp), jnp.zeros((pad,), F32)])[None]
        mrg = dict(w_pa=w_pa[i].astype(BF16), w_pb=w_pb[i].astype(BF16), w_o=w_o[i].astype(BF16),
                   ln1_g=ln1_g[i][None], ln1_b=ln1_b[i][None], w_pg=w_pg[i].astype(BF16),
                   b_pg=b_pg[i][None], w_ple=w_ple[i].astype(BF16), w_router=w_router,
                   b_router=b_router, ngroups=ngroups, nper=nper)
        moe = dict(w_gate=wg_all, w_up=wu_all, w_down=wd_all, layer=i, ln2_g=ln2_g[i][None],
                   ln2_b=ln2_b[i][None], ngroups=ngroups, nper=nper)
        layers.append(dict(w_in=w_in_all, layer=i, b_gate=b_gate[i][None], s5=s5, rw=rw, mrg=mrg,
                           moe=moe))

    bp = x_prompt.shape[0]
    dt = x_prompt.dtype
    zero_shift = jnp.zeros((depth, bp) + state_shift.shape[2:], dt)
    zero_wkv = jnp.zeros((depth, bp) + state_wkv.shape[2:], dt)
    zero_s5 = jnp.zeros((depth, bp) + state_s5.shape[2:], dt)
    y_p, sh_p, wkv_p, s5_p = _trunk(x_prompt, p_prompt, zero_shift, zero_wkv, zero_s5, layers)
    y_s, sh_s, wkv_s, s5_s = _trunk(x_sample, p_sample, state_shift, state_wkv, state_s5, layers)
    return (y_p, y_s, sh_p, wkv_p, s5_p, sh_s, wkv_s, s5_s)
```

```python
import functools
import math

import jax
import jax.numpy as jnp
from jax import lax
from jax.experimental import pallas as pl
from jax.experimental.pallas import tpu as pltpu

F32 = jnp.float32
BF16 = jnp.bfloat16

DEPTH = 2
ALPHA = (2 * DEPTH) ** 0.25
LN_EPS = 1e-5
GN_EPS = 64e-5

S5_GROUP = 16
S5_STATE = 64
S5_GROUPS_PER_BLOCK = 8
RW_HEAD = 64
DECAY_SCALE = math.exp(-0.5)
ROUTER_LANES = 128
WKV_CHUNK = 64
VMEM_LIMIT = 56 * 1024 * 1024
WKV_TERMS = {"nt": 1, "inv": 1, "out": 1, "state": 1}
WKV_CHUNKS_PER_ITER = 4
MOE_EXPERTS_PER_STEP = 2
S5_SLAB_ROWS = 256
INPROJ_ROW_SLABS = 2
MERGE_ROW_SLABS = 2
MOE_BLOCK_ROWS = 128


def _bdot(a, b):
    return jnp.dot(a.astype(BF16), b.astype(BF16), preferred_element_type=F32)


def _split(x, n):
    parts = []
    rem = x
    for j in range(n):
        p = rem.astype(BF16)
        parts.append(p)
        if j + 1 < n:
            rem = rem - p.astype(F32)
    return parts


_NN = (((1,), (0,)), ((), ()))
_NT = (((1,), (1,)), ((), ()))
_TN = (((0,), (0,)), ((), ()))


def _mdot(a_parts, b_parts, dims=_NN):
    order = max(len(a_parts), len(b_parts))
    acc = None
    for ia, pa in enumerate(a_parts):
        for ib, pb in enumerate(b_parts):
            if ia + ib < order:
                t = lax.dot_general(pa, pb, dims, preferred_element_type=F32)
                acc = t if acc is None else acc + t
    return acc


def _head_sum(x, hd, terms=1):
    n = x.shape[-1]
    same = (lax.broadcasted_iota(jnp.int32, (n, n), 0) // hd
            == lax.broadcasted_iota(jnp.int32, (n, n), 1) // hd)
    return _mdot(_split(x, terms), [same.astype(BF16)])


def _sigmoid(x):
    return 1.0 / (1.0 + jnp.exp(-x))


def _layer_norm(h, g, b):
    mu = jnp.mean(h, axis=-1, keepdims=True)
    d = h - mu
    var = jnp.mean(d * d, axis=-1, keepdims=True)
    return d * lax.rsqrt(var + LN_EPS) * g + b


def _params(sem):
    return pltpu.CompilerParams(dimension_semantics=sem, vmem_limit_bytes=VMEM_LIMIT)


def _full(shape):
    n = len(shape)
    return pl.BlockSpec(shape, lambda *_: (0,) * n)


def _inproj_body(*refs, s5w, rwc, w, dr, ar_, hd, seq_len, nslab, has_vmix):
    (x_ref, win_ref, bg_ref, sh0_ref, mu_ref, w0_ref, w2_ref, a0_ref, a2_ref, g2_ref, kk_ref,
     ka_ref) = refs[:12]
    refs = refs[12:]
    if has_vmix:
        vf_ref, v0_ref, v1_ref, v2_ref = refs[:4]
        refs = refs[4:]
    u_ref, gate_ref, r_ref, k_ref, v_ref, lw_ref, kkn_ref, bv_ref, g_ref, zlast_ref, carry = refs

    i = pl.program_id(0)
    tm = x_ref.shape[0]
    rs = tm // nslab
    slabs = [slice(j * rs, (j + 1) * rs) for j in range(nslab)]
    xb = [x_ref[s, :].astype(BF16) for s in slabs]
    for s, xv in zip(slabs, xb):
        u_ref[s, :] = jnp.dot(xv, win_ref[:, :s5w], preferred_element_type=F32)
    zr = [jnp.dot(xv, win_ref[:, s5w:s5w + rwc], preferred_element_type=F32) for xv in xb]
    for s, xv in zip(slabs, xb):
        zg = jnp.dot(xv, win_ref[:, s5w + rwc:], preferred_element_type=F32)
        gate_ref[s, :] = _sigmoid(zg + bg_ref[...])

    row = lax.broadcasted_iota(jnp.int32, (rs, rwc), 0)
    @pl.when(i == 0)
    def _():
        carry[...] = jnp.zeros_like(carry)

    zs = []
    for j, z in enumerate(zr):
        if j > 0:
            before = zr[j - 1][rs - 1:rs, :]
        elif tm <= seq_len:
            before = jnp.where((i * tm) % seq_len == 0, sh0_ref[0], carry[...])
        else:
            before = carry[...]
        prev = jnp.where(row == 0, before, pltpu.roll(z, 1, axis=0))
        for bb in range(tm // seq_len):
            if j * rs <= bb * seq_len < (j + 1) * rs:
                prev = jnp.where(row == bb * seq_len - j * rs, sh0_ref[bb], prev)
        zs.append(z + mu_ref[...] * (prev - z))
    carry[...] = zr[-1][rs - 1:rs, :]
    if tm <= seq_len:
        zlast_ref[0] = zr[-1][rs - 1:rs, :]
    else:
        for bb in range(tm // seq_len):
            last = (bb + 1) * seq_len - 1
            zlast_ref[bb] = zr[last // rs][last % rs:last % rs + 1, :]

    c3 = 3 * w
    wl = [w0_ref[...] + _bdot(jnp.tanh(z[:, c3:c3 + dr]), w2_ref[...]) for z in zs]
    a = [_sigmoid(a0_ref[...] + _bdot(z[:, c3 + dr:c3 + dr + ar_], a2_ref[...])) for z in zs]
    g = [_bdot(_sigmoid(z[:, c3 + dr + ar_:]), g2_ref[...]) for z in zs]
    v = [z[:, 2 * w:c3] for z in zs]
    if has_vmix:
        mix = [_sigmoid(v0_ref[...] + _bdot(_bdot(vv, v1_ref[...]), v2_ref[...])) for vv in v]
        v = [vv + (vf_ref[s, :] - vv) * m for s, vv, m in zip(slabs, v, mix)]
    kk = [z[:, w:2 * w] * kk_ref[...] for z in zs]
    ss = [_head_sum(q * q, hd) for q in kk]
    for j, s in enumerate(slabs):
        lw_ref[s, :] = -DECAY_SCALE * _sigmoid(wl[j])
        g_ref[s, :] = g[j]
        r_ref[s, :] = zs[j][:, :w]
        v_ref[s, :] = v[j]
        kkn = kk[j] * lax.rsqrt(jnp.maximum(ss[j], 1e-24))
        kkn_ref[s, :] = kkn
        bv_ref[s, :] = kkn * a[j]
        k_ref[s, :] = zs[j][:, w:2 * w] * (1.0 + (a[j] - 1.0) * ka_ref[...])


def _inproj(x, w_in, layer, b_gate, shift0, prm, v_first, seq_len, s5w, tm):
    t, d = x.shape
    n_in = w_in.shape[2]
    rwc = prm["mu"].shape[-1]
    w = prm["w0"].shape[-1]
    ng = n_in - s5w - rwc
    has_vmix = v_first is not None
    nseq = max(1, tm // seq_len)
    row = lambda n: pl.BlockSpec((tm, n), lambda i: (i, 0))
    vec = lambda n: _full((1, n))
    ins = [x, w_in, b_gate, shift0, prm["mu"], prm["w0"], prm["w2"], prm["a0"], prm["a2"], prm["g2"],
           prm["k_k"], prm["k_a"]]
    specs = [row(d), pl.BlockSpec((None, d, n_in), lambda i: (layer, 0, 0)), vec(ng),
             pl.BlockSpec((nseq, 1, rwc), lambda i: ((i * tm) // (seq_len * nseq), 0, 0)), vec(rwc),
             vec(w), _full(prm["w2"].shape), vec(w), _full(prm["a2"].shape), _full(prm["g2"].shape),
             vec(w), vec(w)]
    if has_vmix:
        ins += [v_first, prm["v0"], prm["v1"], prm["v2"]]
        specs += [row(w), vec(w), _full(prm["v1"].shape), _full(prm["v2"].shape)]
    ntile = t // tm
    outs = pl.pallas_call(
        functools.partial(_inproj_body, s5w=s5w, rwc=rwc, w=w, dr=prm["w2"].shape[0],
                          ar_=prm["a2"].shape[0], hd=RW_HEAD, seq_len=seq_len, nslab=INPROJ_ROW_SLABS,
                          has_vmix=has_vmix),
        grid=(ntile,),
        in_specs=specs,
        out_specs=[row(s5w), row(ng)] + [row(w)] * 7
                  + [pl.BlockSpec((nseq, 1, rwc), lambda i: (i, 0, 0))],
        out_shape=[jax.ShapeDtypeStruct((t, s5w), F32), jax.ShapeDtypeStruct((t, ng), F32)]
                  + [jax.ShapeDtypeStruct((t, w), F32)] * 7
                  + [jax.ShapeDtypeStruct((ntile * nseq, 1, rwc), F32)],
        scratch_shapes=[pltpu.VMEM((1, rwc), F32)],
        compiler_params=_params(("arbitrary",)),
        name="inproj_premix",
    )(*ins)
    zlast = outs[-1].reshape(t // seq_len, -1, 1, rwc)[:, -1]
    return list(outs[:-1]) + [zlast]


def _s5_body(u_ref, h0r_ref, h0i_ref, abr_ref, abi_ref, wbu_ref, wy_ref, d_ref, wglu_ref, bglu_ref,
             y_ref, hlr_ref, hli_ref, bur, bui, hr, hi, str_, sti, *, nb, tl, sub, nblk, cb, sb):
    i = pl.program_id(0)

    @pl.when(i == 0)
    def _():
        str_[...] = h0r_ref[...]
        sti[...] = h0i_ref[...]

    slabs = [(bi, s0) for s0 in range(0, tl, sub) for bi in range(nb)]
    for bi, s0 in slabs:
        u = u_ref[bi, s0:s0 + sub, :]
        for j in range(nblk):
            bu = _bdot(u[:, cb * j:cb * (j + 1)], wbu_ref[j])
            bur[bi, s0:s0 + sub, sb * j:sb * (j + 1)] = bu[:, :sb]
            bui[bi, s0:s0 + sub, sb * j:sb * (j + 1)] = bu[:, sb:]

    ar = abr_ref[...]
    ai = abi_ref[...]
    state = [(str_[bi], sti[bi]) for bi in range(nb)]
    for s0 in range(0, tl, sub):
        for t in range(s0, s0 + sub):
            for bi in range(nb):
                pr, pi = state[bi]
                nr = ar * pr - ai * pi + bur[bi, t:t + 1, :]
                ni = ar * pi + ai * pr + bui[bi, t:t + 1, :]
                hr[bi, t:t + 1, :] = nr
                hi[bi, t:t + 1, :] = ni
                state[bi] = (nr, ni)
        for bi in range(nb):
            u = u_ref[bi, s0:s0 + sub, :]
            ys = [_bdot(hr[bi, s0:s0 + sub, sb * j:sb * (j + 1)], wy_ref[j, :sb, :])
                  + _bdot(hi[bi, s0:s0 + sub, sb * j:sb * (j + 1)], wy_ref[j, sb:, :])
                  for j in range(nblk)]
            y = jax.nn.gelu(jnp.concatenate(ys, axis=1) + d_ref[...] * u)
            y_ref[bi, s0:s0 + sub, :] = y * _sigmoid(_bdot(y, wglu_ref[...]) + bglu_ref[...])
    for bi in range(nb):
        str_[bi] = state[bi][0]
        sti[bi] = state[bi][1]
        hlr_ref[bi] = state[bi][0]
        hli_ref[bi] = state[bi][1]


def _s5_mixer(u, h0r, h0i, prm, tl):
    b, l, w = u.shape
    ns = h0r.shape[-1]
    nblk = w // (S5_GROUP * S5_GROUPS_PER_BLOCK)
    cb = w // nblk
    sb = ns // nblk
    seq = pl.BlockSpec((b, tl, w), lambda i: (0, i, 0))
    st = _full((b, 1, ns))
    return pl.pallas_call(
        functools.partial(_s5_body, nb=b, tl=tl, sub=min(S5_SLAB_ROWS, tl), nblk=nblk, cb=cb, sb=sb),
        grid=(l // tl,),
        in_specs=[seq, st, st, _full((1, ns)), _full((1, ns)), _full(prm["wbu"].shape),
                  _full(prm["wy"].shape), _full((1, w)), _full((w, w)), _full((1, w))],
        out_specs=[seq, st, st],
        out_shape=[jax.ShapeDtypeStruct((b, l, w), F32), jax.ShapeDtypeStruct((b, 1, ns), F32),
                   jax.ShapeDtypeStruct((b, 1, ns), F32)],
        scratch_shapes=[pltpu.VMEM((b, tl, ns), F32)] * 4 + [pltpu.VMEM((b, 1, ns), F32)] * 2,
        compiler_params=_params(("arbitrary",)),
        name="s5_mixer",
    )(u, h0r, h0i, prm["abr"], prm["abi"], prm["wbu"], prm["wy"], prm["d"], prm["wglu"], prm["bglu"])


def _s5_discretize(a_re, a_im, log_dt, b_re, b_im, c_re, c_im):
    g, p = a_re.shape
    c = b_re.shape[-1]
    gb = S5_GROUPS_PER_BLOCK
    nblk = g // gb
    dt = jnp.exp(log_dt)[:, None]
    mag = jnp.exp(a_re * dt)
    ab_re = mag * jnp.cos(a_im * dt)
    ab_im = mag * jnp.sin(a_im * dt)
    den = a_re * a_re + a_im * a_im
    n_re = ab_re - 1.0
    k_re = (n_re * a_re + ab_im * a_im) / den
    k_im = (ab_im * a_re - n_re * a_im) / den
    bb_re = k_re[..., None] * b_re - k_im[..., None] * b_im
    bb_im = k_re[..., None] * b_im + k_im[..., None] * b_re
    eye = jnp.eye(gb, dtype=F32)

    def pack_in(bb):
        return jnp.einsum("jgpc,gh->jgchp", bb.reshape(nblk, gb, p, c), eye).reshape(nblk, gb * c, gb * p)

    def pack_out(cc):
        return jnp.einsum("jgcp,gh->jgphc", cc.reshape(nblk, gb, c, p), eye).reshape(nblk, gb * p, gb * c)

    wbu = jnp.concatenate([pack_in(bb_re), pack_in(bb_im)], axis=-1).astype(BF16)
    wy = jnp.concatenate([pack_out(c_re), -pack_out(c_im)], axis=1).astype(BF16)
    return ab_re.reshape(1, g * p), ab_im.reshape(1, g * p), wbu, wy


def _wkv_body(r_ref, k_ref, v_ref, lw_ref, kkn_ref, bv_ref, g_ref, rk_ref, lng_ref, lnb_ref, s0_ref,
              y_ref, sl_ref, s_scr, at_scr, rt_scr, kt_scr, bt_scr, kd_scr, bd_scr, ce_scr, y_scr,
              *, tl, c, cpi, nh, hd):
    i = pl.program_id(1)

    @pl.when(i == 0)
    def _():
        s_scr[...] = s0_ref[...]

    rowi = lax.broadcasted_iota(jnp.int32, (tl, tl), 0)
    coli = lax.broadcasted_iota(jnp.int32, (tl, tl), 1)
    same_chunk = (rowi // c) == (coli // c)
    lw = lw_ref[...]
    cl = _mdot([(same_chunk & (coli <= rowi)).astype(BF16)], _split(lw, 3))
    ce = jnp.concatenate([jnp.broadcast_to(cl[j * c + c - 1:(j + 1) * c, :], (c, cl.shape[1]))
                          for j in range(tl // c)], axis=0)
    kkn = kkn_ref[...]
    bv = bv_ref[...]
    k = k_ref[...]
    einv = jnp.exp(-cl)
    to_end = jnp.exp(ce - cl)
    at_scr[...] = -kkn * jnp.exp(cl - lw)
    rt_scr[...] = r_ref[...] * jnp.exp(cl)
    kt_scr[...] = k * einv
    bt_scr[...] = bv * einv
    kd_scr[...] = k * to_end
    bd_scr[...] = bv * to_end
    ce_scr[...] = ce

    ri = lax.broadcasted_iota(jnp.int32, (c, c), 0)
    ci_ = lax.broadcasted_iota(jnp.int32, (c, c), 1)
    lower = ci_ <= ri
    strict = ci_ < ri
    levels = int(math.log2(c))
    sp = lambda x, cls: _split(x, WKV_TERMS[cls])

    def chunk_group(gi, carry):
        base = pl.multiple_of(gi * (cpi * c), cpi * c)
        units = [(cc, h) for cc in range(cpi) for h in range(nh)]
        rows = [pl.ds(base + cc * c, c) for cc, _ in units]
        hsl = [slice(h * hd, (h + 1) * hd) for _, h in units]
        un = range(len(units))
        at = [at_scr[rows[u], hsl[u]] for u in un]
        rt = [rt_scr[rows[u], hsl[u]] for u in un]
        lhs = [sp(jnp.concatenate([at[u], rt[u]], axis=0), "nt") for u in un]
        rhs = [sp(jnp.concatenate([bt_scr[rows[u], hsl[u]], kt_scr[rows[u], hsl[u]]], axis=0), "nt")
               for u in un]
        pm = [_mdot(lhs[u], rhs[u], _NT) for u in un]
        n_ab = [jnp.where(strict, pm[u][:c, :c], 0.0) for u in un]
        a_ak = [sp(jnp.where(strict, pm[u][:c, c:], 0.0), "out") for u in un]
        a_rb = [sp(jnp.where(lower, pm[u][c:, :c], 0.0), "out") for u in un]
        a_rk = [sp(jnp.where(lower, pm[u][c:, c:], 0.0), "out") for u in un]
        v_p = [sp(v_ref[rows[u], hsl[u]], "out") for u in un]

        x = [jnp.concatenate([at[u], _mdot(a_ak[u], v_p[u])], axis=1) for u in un]
        rkv = [_mdot(a_rk[u], v_p[u]) for u in un]
        pw = n_ab
        for lv in range(levels):
            pw_p = [sp(pw[u], "inv") for u in un]
            x = [x[u] + _mdot(pw_p[u], sp(x[u], "inv")) for u in un]
            if lv + 1 < levels:
                pw = [_mdot(pw_p[u], pw_p[u]) for u in un]
        x_p = [sp(x[u], "out") for u in un]
        qy = [jnp.concatenate([rt[u], rkv[u]], axis=1) + _mdot(a_rb[u], x_p[u]) for u in un]
        mz = [_mdot(x_p[u], sp(bd_scr[rows[u], hsl[u]], "out"), _TN) for u in un]
        zt = [mz[u][hd:, :] + _mdot(v_p[u], sp(kd_scr[rows[u], hsl[u]], "out"), _TN) for u in un]

        s_cur = [s_scr[h] for h in range(nh)]
        for cc in range(cpi):
            g_end = jnp.exp(ce_scr[pl.ds(base + cc * c, 1), :])
            s_p = [sp(s_cur[h], "state") for h in range(nh)]
            for h in range(nh):
                u = cc * nh + h
                y_scr[rows[u], hsl[u]] = _mdot(sp(qy[u][:, :hd], "state"), s_p[h], _NT) + qy[u][:, hd:]
            s_cur = [s_cur[h] * g_end[:, hsl[h]] + _mdot(s_p[h], sp(mz[cc * nh + h][:hd, :], "state"))
                     + zt[cc * nh + h] for h in range(nh)]
        for h in range(nh):
            s_scr[h] = s_cur[h]
        return carry

    lax.fori_loop(0, tl // (cpi * c), chunk_group, 0)

    @pl.when(i == pl.num_programs(1) - 1)
    def _():
        sl_ref[...] = s_scr[...]

    y = y_scr[...]
    yc = y - _head_sum(y, hd) * (1.0 / hd)
    yv = _head_sum(yc * yc, hd) * (1.0 / hd)
    yn = yc * lax.rsqrt(yv + GN_EPS) * lng_ref[...] + lnb_ref[...]
    bonus = _head_sum(r_ref[...] * k_ref[...] * rk_ref[...], hd) * v_ref[...]
    y_ref[...] = (yn + bonus) * g_ref[...]


def _wkv(r, k, v, lw, kkn, bv, g, prm, s0, tl, c):
    b, l, w = r.shape
    nh, hd = s0.shape[1], s0.shape[2]
    seq = pl.BlockSpec((None, tl, w), lambda bi, i: (bi, i, 0))
    st = pl.BlockSpec((None, nh, hd, hd), lambda bi, i: (bi, 0, 0, 0))
    vec = _full((1, w))
    return pl.pallas_call(
        functools.partial(_wkv_body, tl=tl, c=c, cpi=min(WKV_CHUNKS_PER_ITER, tl // c), nh=nh, hd=hd),
        grid=(b, l // tl),
        in_specs=[seq] * 7 + [vec, vec, vec, st],
        out_specs=[seq, st],
        out_shape=[jax.ShapeDtypeStruct((b, l, w), F32), jax.ShapeDtypeStruct(s0.shape, F32)],
        scratch_shapes=[pltpu.VMEM((nh, hd, hd), F32)] + [pltpu.VMEM((tl, w), F32)] * 8,
        compiler_params=_params(("parallel", "arbitrary")),
        name="wkv",
    )(r, k, v, lw, kkn, bv, g, prm["r_k"], prm["ln_g"], prm["ln_b"], s0)


def _merge_body(x_ref, ya_ref, yb_ref, gate_ref, p_ref, wpa_ref, wpb_ref, wo_ref, g1_ref, b1_ref,
                wpg_ref, bpg_ref, wple_ref, wr_ref, br_ref, x1_ref, base_ref, cmb_ref,
                *, d, ngroups, nper, nsplit):
    rs = x_ref.shape[0] // nsplit
    slabs = [slice(j * rs, (j + 1) * rs) for j in range(nsplit)]
    pa = [_bdot(ya_ref[s, :], wpa_ref[...]) for s in slabs]
    pb = [_bdot(yb_ref[s, :], wpb_ref[...]) for s in slabs]
    merged = [gate_ref[s, :d] * a + gate_ref[s, d:] * b for s, a, b in zip(slabs, pa, pb)]
    h = [ALPHA * x_ref[s, :] + _bdot(m, wo_ref[...]) for s, m in zip(slabs, merged)]
    x1s = [_layer_norm(v, g1_ref[...], b1_ref[...]) for v in h]
    pg = [_bdot(v, wpg_ref[...]) for v in x1s]
    pe = [_bdot(p_ref[s, :], wple_ref[...]) for s in slabs]
    wr_p = _split(wr_ref[...], 2)
    lg = [_mdot(_split(v, 2), wr_p) + br_ref[...] for v in x1s]
    for s, v, g_, e_, l_ in zip(slabs, x1s, pg, pe, lg):
        x1_ref[s, :] = v.astype(BF16)
        base_ref[s, :] = ALPHA * v + _sigmoid(g_ + bpg_ref[...]) * e_
        cmb_ref[s, :] = _route(l_, ngroups, nper)


def _route(logits, ngroups, nper):
    lane = lax.broadcasted_iota(jnp.int32, logits.shape, 1)
    neg = -jnp.inf
    big = logits.shape[1]
    first_of = lambda hit: jnp.min(jnp.where(hit, lane, big), axis=-1, keepdims=True)
    gl = jnp.where(lane < ngroups, logits, neg)
    gmax = jnp.max(gl, axis=-1, keepdims=True)
    g_sel = first_of(gl == gmax)
    g_w = 1.0 / jnp.sum(jnp.exp(gl - gmax), axis=-1, keepdims=True)
    lo = ngroups + nper * g_sel
    el = jnp.where((lane >= lo) & (lane < lo + nper), logits, neg)
    m1 = jnp.max(el, axis=-1, keepdims=True)
    i1 = first_of(el == m1)
    el2 = jnp.where(lane == i1, neg, el)
    m2 = jnp.max(el2, axis=-1, keepdims=True)
    i2 = first_of(el2 == m2)
    e2 = jnp.exp(m2 - m1)
    w1 = g_w / (1.0 + e2)
    w2 = g_w * e2 / (1.0 + e2)
    return (jnp.where(lane == i1, w1, 0.0) + jnp.where(lane == i2, w2, 0.0)
            + jnp.where(lane == 0, g_sel.astype(F32), 0.0))


def _merge(x, ya, yb, gate, p, layer, prm, tm):
    t, d = x.shape
    row = lambda n: pl.BlockSpec((tm, n), lambda i: (i, 0))
    ws = [prm["w_pa"], prm["w_pb"], prm["w_o"], prm["ln1_g"], prm["ln1_b"], prm["w_pg"], prm["b_pg"],
          prm["w_ple"], prm["w_router"], prm["b_router"]]
    return pl.pallas_call(
        functools.partial(_merge_body, d=d, ngroups=prm["ngroups"], nper=prm["nper"],
                          nsplit=MERGE_ROW_SLABS),
        grid=(t // tm,),
        in_specs=[row(d), row(ya.shape[1]), row(yb.shape[1]), row(gate.shape[1]),
                  pl.BlockSpec((None, tm, p.shape[2]), lambda i: (layer, i, 0))]
                 + [_full(a.shape) for a in ws],
        out_specs=[row(d), row(d), row(ROUTER_LANES)],
        out_shape=[jax.ShapeDtypeStruct((t, d), BF16), jax.ShapeDtypeStruct((t, d), F32),
                   jax.ShapeDtypeStruct((t, ROUTER_LANES), F32)],
        compiler_params=_params(("parallel",)),
        name="merge_ln1_router",
    )(x, ya, yb, gate, p, *ws)


def _moe_body(start_ref, nblk_ref, x1_ref, base_ref, cmb_ref, drow_ref, dcol_ref, wg_ref, wu_ref, wd_ref,
              g2_ref, b2_ref, o_ref, xs_scr, cs_scr, acc_scr, *, lane0, ngroups, nper, blk):
    i = pl.program_id(0)
    e = pl.program_id(1)
    nslot, tm = xs_scr.shape[0], x1_ref.shape[0]
    eps = wg_ref.shape[0]

    @pl.when(e == 0)
    def _():
        slot = lax.broadcasted_iota(jnp.int32, (nslot, tm), 0)
        pm = (slot == drow_ref[...]).astype(BF16)
        xs_scr[...] = jnp.dot(pm, x1_ref[...], preferred_element_type=F32).astype(BF16)
        cs_scr[...] = _bdot(pm, cmb_ref[...])
        acc_scr[...] = jnp.zeros_like(acc_scr)

    grp = (e * eps) // nper
    row0 = start_ref[i * ngroups + grp]
    lane = lax.broadcasted_iota(jnp.int32, (blk, cs_scr.shape[1]), 1)

    def block(j, carry):
        rows = pl.ds(pl.multiple_of(row0 + j * blk, blk), blk)
        xb = xs_scr[rows, :]
        cs = cs_scr[rows, :]
        part = None
        for ee in range(eps):
            ce = jnp.sum(jnp.where(lane == e * eps + ee + lane0, cs, 0.0), axis=-1, keepdims=True)
            hg = jnp.dot(xb, wg_ref[ee], preferred_element_type=F32)
            hu = jnp.dot(xb, wu_ref[ee], preferred_element_type=F32)
            t = _bdot(hg * _sigmoid(hg) * hu * ce, wd_ref[ee])
            part = t if part is None else part + t
        acc_scr[rows, :] += part
        return carry

    lax.fori_loop(0, nblk_ref[i * ngroups + grp], block, 0)

    @pl.when(e == pl.num_programs(1) - 1)
    def _():
        slot_t = lax.broadcasted_iota(jnp.int32, (tm, nslot), 1)
        pmt = (slot_t == dcol_ref[...]).astype(BF16)
        ffn = _bdot(pmt, acc_scr[...])
        o_ref[...] = _layer_norm(base_ref[...] + ffn, g2_ref[...], b2_ref[...])


def _moe_dispatch(cmb, tm, ngroups, blk):
    t = cmb.shape[0]
    nt = t // tm
    g = cmb[:, 0].reshape(nt, tm)
    oh = (g[..., None] == jnp.arange(ngroups, dtype=F32)).astype(F32)
    before = (jnp.arange(tm)[None, :] < jnp.arange(tm)[:, None]).astype(F32)
    rank = jnp.einsum("ts,nsg->ntg", before, oh)
    nblk = (jnp.sum(oh, axis=1).astype(jnp.int32) + blk - 1) // blk
    start = (jnp.cumsum(nblk, axis=-1) - nblk) * blk
    dest = jnp.sum(oh * (rank + start[:, None, :].astype(F32)), axis=-1).astype(jnp.int32)
    return start.reshape(-1), nblk.reshape(-1), dest.reshape(nt, 1, tm), dest.reshape(t, 1)


def _moe(x1, base, cmb, prm, tm):
    t, d = base.shape
    _, ne, _, f = prm["w_gate"].shape
    layer = prm["layer"]
    ngroups, nper = prm["ngroups"], prm["nper"]
    eps = MOE_EXPERTS_PER_STEP
    blk = min(MOE_BLOCK_ROWS, tm)
    nslot = tm + ngroups * blk
    start, nblk, drow, dcol = _moe_dispatch(cmb, tm, ngroups, blk)
    row = lambda n: pl.BlockSpec((tm, n), lambda i, e, *_: (i, 0))
    wspec = lambda a, b: pl.BlockSpec((None, eps, a, b), lambda i, e, *_: (layer, e, 0, 0))
    vec = pl.BlockSpec((1, d), lambda i, e, *_: (0, 0))
    return pl.pallas_call(
        functools.partial(_moe_body, lane0=ngroups, ngroups=ngroups, nper=nper, blk=blk),
        grid_spec=pltpu.PrefetchScalarGridSpec(
            num_scalar_prefetch=2,
            grid=(t // tm, ne // eps),
            in_specs=[row(d), row(d), row(ROUTER_LANES),
                      pl.BlockSpec((None, 1, tm), lambda i, e, *_: (i, 0, 0)), row(1),
                      wspec(d, f), wspec(d, f), wspec(f, d), vec, vec],
            out_specs=row(d),
            scratch_shapes=[pltpu.VMEM((nslot, d), BF16), pltpu.VMEM((nslot, ROUTER_LANES), F32),
                            pltpu.VMEM((nslot, d), F32)]),
        out_shape=jax.ShapeDtypeStruct((t, d), F32),
        compiler_params=_params(("parallel", "arbitrary")),
        name="moe_ln2",
    )(start, nblk, x1, base, cmb, drow, dcol, prm["w_gate"], prm["w_up"], prm["w_down"],
      prm["ln2_g"], prm["ln2_b"])


def _tile(n, target):
    t = min(n, target)
    assert n % t == 0, (n, t)
    return t


def _layer(x, p, shift0, wkv0, s5r0, s5i0, v_first, lp):
    b, l, d = x.shape
    t = b * l
    s5w = lp["s5"]["d"].shape[-1]
    vf = None if v_first is None else v_first.reshape(t, -1)
    u, gate, r, k, v, lw, kkn, bv, g, shift_new = _inproj(
        x.reshape(t, d), lp["w_in"], lp["layer"], lp["b_gate"], shift0, lp["rw"], vf, l, s5w, _tile(t, 512))
    ya, s5r, s5i = _s5_mixer(u.reshape(b, l, s5w), s5r0, s5i0, lp["s5"], _tile(l, 256))
    seq3 = lambda a: a.reshape(b, l, -1)
    yb, wkv_new = _wkv(seq3(r), seq3(k), seq3(v), seq3(lw), seq3(kkn), seq3(bv), seq3(g), lp["rw"], wkv0,
                       _tile(l, 256), min(WKV_CHUNK, l))
    x1, base, cmb = _merge(x.reshape(t, d), ya.reshape(t, s5w), yb.reshape(t, -1), gate,
                           p.reshape(p.shape[0], t, -1), lp["layer"], lp["mrg"], _tile(t, 512))
    out = _moe(x1, base, cmb, lp["moe"], _tile(t, 1024))
    return out.reshape(b, l, d), shift_new, wkv_new, s5r, s5i, (seq3(v) if v_first is None else v_first)


def _trunk(x, p, shift0, wkv0, s50, layers):
    b = x.shape[0]
    shifts, wkvs, s5s = [], [], []
    v_first = None
    for i, lp in enumerate(layers):
        ns = s50.shape[2] * s50.shape[3]
        s5r0 = s50[i, ..., 0].reshape(b, 1, ns)
        s5i0 = s50[i, ..., 1].reshape(b, 1, ns)
        x, sh, wk, s5r, s5i, v_first = _layer(x, p, shift0[i], wkv0[i], s5r0, s5i0, v_first, lp)
        shifts.append(sh)
        wkvs.append(wk)
        s5s.append(jnp.stack([s5r.reshape(s50.shape[1:4]), s5i.reshape(s50.shape[1:4])], axis=-1))
    return x, jnp.stack(shifts), jnp.stack(wkvs), jnp.stack(s5s)


def kernel(x_prompt, x_sample, state_shift, state_wkv, state_s5, p_prompt, p_sample, w_in, b_gate, s5_a_re, s5_a_im, s5_log_dt, s5_b_re, s5_b_im, s5_c_re, s5_c_im, s5_d, s5_w_glu, s5_b_glu, rw_mu, rw_w0, rw_w2, rw_a0, rw_a2, rw_g2, rw_v0, rw_v1, rw_v2, rw_k_k, rw_k_a, rw_r_k, rw_ln_g, rw_ln_b, w_pa, w_pb, w_o, ln1_g, ln1_b, w_rg, b_rg, w_re, b_re, w_gate, w_up, w_down, w_ple, w_pg, b_pg, ln2_g, ln2_b):
    depth, d = w_in.shape[0], w_in.shape[1]
    ngroups = w_rg.shape[-1]
    nper = w_re.shape[-1]
    nexp = ngroups * nper
    w_in_all = w_in.astype(BF16)
    wg_all, wu_all, wd_all = w_gate.astype(BF16), w_up.astype(BF16), w_down.astype(BF16)
    layers = []
    for i in range(depth):
        abr, abi, wbu, wy = _s5_discretize(s5_a_re[i], s5_a_im[i], s5_log_dt[i], s5_b_re[i], s5_b_im[i],
                                           s5_c_re[i], s5_c_im[i])
        s5 = dict(abr=abr, abi=abi, wbu=wbu, wy=wy, d=s5_d[i][None], wglu=s5_w_glu[i].astype(BF16),
                  bglu=s5_b_glu[i][None])
        rw = dict(mu=rw_mu[i][None], w0=rw_w0[i][None], w2=rw_w2[i].astype(BF16), a0=rw_a0[i][None],
                  a2=rw_a2[i].astype(BF16), g2=rw_g2[i].astype(BF16), k_k=rw_k_k[i][None],
                  k_a=rw_k_a[i][None], r_k=rw_r_k[i].reshape(1, -1), ln_g=rw_ln_g[i][None],
                  ln_b=rw_ln_b[i][None])
        if i > 0:
            rw.update(v0=rw_v0[i - 1][None], v1=rw_v1[i - 1].astype(BF16), v2=rw_v2[i - 1].astype(BF16))
        pad = ROUTER_LANES - ngroups - nexp
        w_router = jnp.concatenate([w_rg[i], w_re[i].reshape(d, nexp), jnp.zeros((d, pad), F32)], axis=1)
        b_router = jnp.concatenate([b_rg[i], b_re[i].reshape(nexp), jnp.zeros((pad,), F32)])[None]
        mrg = dict(w_pa=w_pa[i].astype(BF16), w_pb=w_pb[i].astype(BF16), w_o=w_o[i].astype(BF16),
                   ln1_g=ln1_g[i][None], ln1_b=ln1_b[i][None], w_pg=w_pg[i].astype(BF16),
                   b_pg=b_pg[i][None], w_ple=w_ple[i].astype(BF16), w_router=w_router,
                   b_router=b_router, ngroups=ngroups, nper=nper)
        moe = dict(w_gate=wg_all, w_up=wu_all, w_down=wd_all, layer=i, ln2_g=ln2_g[i][None],
                   ln2_b=ln2_b[i][None], ngroups=ngroups, nper=nper)
        layers.append(dict(w_in=w_in_all, layer=i, b_gate=b_gate[i][None], s5=s5, rw=rw, mrg=mrg,
                           moe=moe))

    bp = x_prompt.shape[0]
    dt = x_prompt.dtype
    zero_shift = jnp.zeros((depth, bp) + state_shift.shape[2:], dt)
    zero_wkv = jnp.zeros((depth, bp) + state_wkv.shape[2:], dt)
    zero_s5 = jnp.zeros((depth, bp) + state_s5.shape[2:], dt)
    y_p, sh_p, wkv_p, s5_p = _trunk(x_prompt, p_prompt, zero_shift, zero_wkv, zero_s5, layers)
    y_s, sh_s, wkv_s, s5_s = _trunk(x_sample, p_sample, state_shift, state_wkv, state_s5, layers)
    return (y_p, y_s, sh_p, wkv_p, s5_p, sh_s, wkv_s, s5_s)
```

```python
import functools
import math

import jax
import jax.numpy as jnp
from jax import lax
from jax.experimental import pallas as pl
from jax.experimental.pallas import tpu as pltpu

F32 = jnp.float32
BF16 = jnp.bfloat16

DEPTH = 2
ALPHA = (2 * DEPTH) ** 0.25
LN_EPS = 1e-5
GN_EPS = 64e-5

S5_GROUP = 16
S5_GROUPS_PER_BLOCK = 8
RW_HEAD = 64
DECAY_SCALE = math.exp(-0.5)
ROUTER_LANES = 128
WKV_CHUNK = 64
VMEM_LIMIT = 56 * 1024 * 1024
WKV_TERMS = {"nt": 1, "inv": 1, "out": 1, "state": 1}
WKV_CHUNKS_PER_ITER = 4
MOE_EXPERTS_PER_STEP = 2
S5_SLAB_ROWS = 256
INPROJ_ROW_SLABS = 2
MERGE_ROW_SLABS = 2
MOE_BLOCK_ROWS = 128
MOE_ROW_ALIGN = 16


def _bdot(a, b):
    return jnp.dot(a.astype(BF16), b.astype(BF16), preferred_element_type=F32)


def _split(x, n):
    parts = []
    rem = x
    for j in range(n):
        p = rem.astype(BF16)
        parts.append(p)
        if j + 1 < n:
            rem = rem - p.astype(F32)
    return parts


_NN = (((1,), (0,)), ((), ()))
_NT = (((1,), (1,)), ((), ()))
_TN = (((0,), (0,)), ((), ()))


def _mdot(a_parts, b_parts, dims=_NN):
    order = max(len(a_parts), len(b_parts))
    acc = None
    for ia, pa in enumerate(a_parts):
        for ib, pb in enumerate(b_parts):
            if ia + ib < order:
                t = lax.dot_general(pa, pb, dims, preferred_element_type=F32)
                acc = t if acc is None else acc + t
    return acc


def _head_sum(x, hd, terms=1):
    n = x.shape[-1]
    same = (lax.broadcasted_iota(jnp.int32, (n, n), 0) // hd
            == lax.broadcasted_iota(jnp.int32, (n, n), 1) // hd)
    return _mdot(_split(x, terms), [same.astype(BF16)])


def _sigmoid(x):
    return 1.0 / (1.0 + jnp.exp(-x))


def _layer_norm(h, g, b):
    mu = jnp.mean(h, axis=-1, keepdims=True)
    d = h - mu
    var = jnp.mean(d * d, axis=-1, keepdims=True)
    return d * lax.rsqrt(var + LN_EPS) * g + b


def _params(sem):
    return pltpu.CompilerParams(dimension_semantics=sem, vmem_limit_bytes=VMEM_LIMIT)


def _full(shape):
    n = len(shape)
    return pl.BlockSpec(shape, lambda *_: (0,) * n)


def _inproj_body(*refs, s5w, rwc, w, dr, ar_, hd, seq_len, nslab, has_vmix):
    (x_ref, win_ref, bg_ref, sh0_ref, mu_ref, w0_ref, w2_ref, a0_ref, a2_ref, g2_ref, kk_ref,
     ka_ref) = refs[:12]
    refs = refs[12:]
    if has_vmix:
        vf_ref, v0_ref, v1_ref, v2_ref = refs[:4]
        refs = refs[4:]
    u_ref, gate_ref, r_ref, k_ref, v_ref, lw_ref, kkn_ref, bv_ref, g_ref, zlast_ref, carry = refs

    i = pl.program_id(0)
    tm = x_ref.shape[0]
    rs = tm // nslab
    slabs = [slice(j * rs, (j + 1) * rs) for j in range(nslab)]
    xb = [x_ref[s, :].astype(BF16) for s in slabs]
    for s, xv in zip(slabs, xb):
        u_ref[s, :] = jnp.dot(xv, win_ref[:, :s5w], preferred_element_type=F32)
    zr = [jnp.dot(xv, win_ref[:, s5w:s5w + rwc], preferred_element_type=F32) for xv in xb]
    for s, xv in zip(slabs, xb):
        zg = jnp.dot(xv, win_ref[:, s5w + rwc:], preferred_element_type=F32)
        gate_ref[s, :] = _sigmoid(zg + bg_ref[...])

    row = lax.broadcasted_iota(jnp.int32, (rs, rwc), 0)

    @pl.when(i == 0)
    def _():
        carry[...] = jnp.zeros_like(carry)

    zs = []
    for j, z in enumerate(zr):
        if j > 0:
            before = zr[j - 1][rs - 1:rs, :]
        elif tm <= seq_len:
            before = jnp.where((i * tm) % seq_len == 0, sh0_ref[0], carry[...])
        else:
            before = carry[...]
        prev = jnp.where(row == 0, before, pltpu.roll(z, 1, axis=0))
        for bb in range(tm // seq_len):
            if j * rs <= bb * seq_len < (j + 1) * rs:
                prev = jnp.where(row == bb * seq_len - j * rs, sh0_ref[bb], prev)
        zs.append(z + mu_ref[...] * (prev - z))
    carry[...] = zr[-1][rs - 1:rs, :]
    if tm <= seq_len:
        zlast_ref[0] = zr[-1][rs - 1:rs, :]
    else:
        for bb in range(tm // seq_len):
            last = (bb + 1) * seq_len - 1
            zlast_ref[bb] = zr[last // rs][last % rs:last % rs + 1, :]

    c3 = 3 * w
    wl = [w0_ref[...] + _bdot(jnp.tanh(z[:, c3:c3 + dr]), w2_ref[...]) for z in zs]
    a = [_sigmoid(a0_ref[...] + _bdot(z[:, c3 + dr:c3 + dr + ar_], a2_ref[...])) for z in zs]
    g = [_bdot(_sigmoid(z[:, c3 + dr + ar_:]), g2_ref[...]) for z in zs]
    v = [z[:, 2 * w:c3] for z in zs]
    if has_vmix:
        mix = [_sigmoid(v0_ref[...] + _bdot(_bdot(vv, v1_ref[...]), v2_ref[...])) for vv in v]
        v = [vv + (vf_ref[s, :] - vv) * m for s, vv, m in zip(slabs, v, mix)]
    kk = [z[:, w:2 * w] * kk_ref[...] for z in zs]
    ss = [_head_sum(q * q, hd) for q in kk]
    for j, s in enumerate(slabs):
        lw_ref[s, :] = -DECAY_SCALE * _sigmoid(wl[j])
        g_ref[s, :] = g[j]
        r_ref[s, :] = zs[j][:, :w]
        v_ref[s, :] = v[j]
        kkn = kk[j] * lax.rsqrt(jnp.maximum(ss[j], 1e-24))
        kkn_ref[s, :] = kkn
        bv_ref[s, :] = kkn * a[j]
        k_ref[s, :] = zs[j][:, w:2 * w] * (1.0 + (a[j] - 1.0) * ka_ref[...])


def _inproj(x, w_in, layer, b_gate, shift0, prm, v_first, seq_len, s5w, tm):
    t, d = x.shape
    n_in = w_in.shape[2]
    rwc = prm["mu"].shape[-1]
    w = prm["w0"].shape[-1]
    ng = n_in - s5w - rwc
    has_vmix = v_first is not None
    nseq = max(1, tm // seq_len)
    row = lambda n: pl.BlockSpec((tm, n), lambda i: (i, 0))
    vec = lambda n: _full((1, n))
    ins = [x, w_in, b_gate, shift0, prm["mu"], prm["w0"], prm["w2"], prm["a0"], prm["a2"], prm["g2"],
           prm["k_k"], prm["k_a"]]
    specs = [row(d), pl.BlockSpec((None, d, n_in), lambda i: (layer, 0, 0)), vec(ng),
             pl.BlockSpec((nseq, 1, rwc), lambda i: ((i * tm) // (seq_len * nseq), 0, 0)), vec(rwc),
             vec(w), _full(prm["w2"].shape), vec(w), _full(prm["a2"].shape), _full(prm["g2"].shape),
             vec(w), vec(w)]
    if has_vmix:
        ins += [v_first, prm["v0"], prm["v1"], prm["v2"]]
        specs += [row(w), vec(w), _full(prm["v1"].shape), _full(prm["v2"].shape)]
    ntile = t // tm
    outs = pl.pallas_call(
        functools.partial(_inproj_body, s5w=s5w, rwc=rwc, w=w, dr=prm["w2"].shape[0],
                          ar_=prm["a2"].shape[0], hd=RW_HEAD, seq_len=seq_len, nslab=INPROJ_ROW_SLABS,
                          has_vmix=has_vmix),
        grid=(ntile,),
        in_specs=specs,
        out_specs=[row(s5w), row(ng)] + [row(w)] * 7
                  + [pl.BlockSpec((nseq, 1, rwc), lambda i: (i, 0, 0))],
        out_shape=[jax.ShapeDtypeStruct((t, s5w), F32), jax.ShapeDtypeStruct((t, ng), F32)]
                  + [jax.ShapeDtypeStruct((t, w), F32)] * 7
                  + [jax.ShapeDtypeStruct((ntile * nseq, 1, rwc), F32)],
        scratch_shapes=[pltpu.VMEM((1, rwc), F32)],
        compiler_params=_params(("arbitrary",)),
        name="inproj_premix",
    )(*ins)
    zlast = outs[-1].reshape(t // seq_len, -1, 1, rwc)[:, -1]
    return list(outs[:-1]) + [zlast]


def _s5_body(u_ref, h0r_ref, h0i_ref, abr_ref, abi_ref, wbu_ref, wy_ref, d_ref, wglu_ref, bglu_ref,
             y_ref, hlr_ref, hli_ref, bur, bui, hr, hi, str_, sti, *, nb, tl, sub, nblk, cb, sb):
    i = pl.program_id(0)

    @pl.when(i == 0)
    def _():
        str_[...] = h0r_ref[...]
        sti[...] = h0i_ref[...]

    slabs = [(bi, s0) for s0 in range(0, tl, sub) for bi in range(nb)]
    for bi, s0 in slabs:
        u = u_ref[bi, s0:s0 + sub, :]
        for j in range(nblk):
            bu = _bdot(u[:, cb * j:cb * (j + 1)], wbu_ref[j])
            bur[bi, s0:s0 + sub, sb * j:sb * (j + 1)] = bu[:, :sb]
            bui[bi, s0:s0 + sub, sb * j:sb * (j + 1)] = bu[:, sb:]

    ar = abr_ref[...]
    ai = abi_ref[...]
    state = [(str_[bi], sti[bi]) for bi in range(nb)]
    for s0 in range(0, tl, sub):
        for t in range(s0, s0 + sub):
            for bi in range(nb):
                pr, pi = state[bi]
                nr = ar * pr - ai * pi + bur[bi, t:t + 1, :]
                ni = ar * pi + ai * pr + bui[bi, t:t + 1, :]
                hr[bi, t:t + 1, :] = nr
                hi[bi, t:t + 1, :] = ni
                state[bi] = (nr, ni)
        for bi in range(nb):
            u = u_ref[bi, s0:s0 + sub, :]
            ys = [_bdot(hr[bi, s0:s0 + sub, sb * j:sb * (j + 1)], wy_ref[j, :sb, :])
                  + _bdot(hi[bi, s0:s0 + sub, sb * j:sb * (j + 1)], wy_ref[j, sb:, :])
                  for j in range(nblk)]
            y = jax.nn.gelu(jnp.concatenate(ys, axis=1) + d_ref[...] * u)
            y_ref[bi, s0:s0 + sub, :] = y * _sigmoid(_bdot(y, wglu_ref[...]) + bglu_ref[...])
    for bi in range(nb):
        str_[bi] = state[bi][0]
        sti[bi] = state[bi][1]
        hlr_ref[bi] = state[bi][0]
        hli_ref[bi] = state[bi][1]


def _s5_mixer(u, h0r, h0i, prm, tl):
    b, l, w = u.shape
    ns = h0r.shape[-1]
    nblk = w // (S5_GROUP * S5_GROUPS_PER_BLOCK)
    cb = w // nblk
    sb = ns // nblk
    seq = pl.BlockSpec((b, tl, w), lambda i: (0, i, 0))
    st = _full((b, 1, ns))
    return pl.pallas_call(
        functools.partial(_s5_body, nb=b, tl=tl, sub=min(S5_SLAB_ROWS, tl), nblk=nblk, cb=cb, sb=sb),
        grid=(l // tl,),
        in_specs=[seq, st, st, _full((1, ns)), _full((1, ns)), _full(prm["wbu"].shape),
                  _full(prm["wy"].shape), _full((1, w)), _full((w, w)), _full((1, w))],
        out_specs=[seq, st, st],
        out_shape=[jax.ShapeDtypeStruct((b, l, w), F32), jax.ShapeDtypeStruct((b, 1, ns), F32),
                   jax.ShapeDtypeStruct((b, 1, ns), F32)],
        scratch_shapes=[pltpu.VMEM((b, tl, ns), F32)] * 4 + [pltpu.VMEM((b, 1, ns), F32)] * 2,
        compiler_params=_params(("arbitrary",)),
        name="s5_mixer",
    )(u, h0r, h0i, prm["abr"], prm["abi"], prm["wbu"], prm["wy"], prm["d"], prm["wglu"], prm["bglu"])


def _s5_discretize(a_re, a_im, log_dt, b_re, b_im, c_re, c_im):
    g, p = a_re.shape
    c = b_re.shape[-1]
    gb = S5_GROUPS_PER_BLOCK
    nblk = g // gb
    dt = jnp.exp(log_dt)[:, None]
    mag = jnp.exp(a_re * dt)
    ab_re = mag * jnp.cos(a_im * dt)
    ab_im = mag * jnp.sin(a_im * dt)
    den = a_re * a_re + a_im * a_im
    n_re = ab_re - 1.0
    k_re = (n_re * a_re + ab_im * a_im) / den
    k_im = (ab_im * a_re - n_re * a_im) / den
    bb_re = k_re[..., None] * b_re - k_im[..., None] * b_im
    bb_im = k_re[..., None] * b_im + k_im[..., None] * b_re
    eye = jnp.eye(gb, dtype=F32)

    def pack_in(bb):
        return jnp.einsum("jgpc,gh->jgchp", bb.reshape(nblk, gb, p, c), eye).reshape(nblk, gb * c, gb * p)

    def pack_out(cc):
        return jnp.einsum("jgcp,gh->jgphc", cc.reshape(nblk, gb, c, p), eye).reshape(nblk, gb * p, gb * c)

    wbu = jnp.concatenate([pack_in(bb_re), pack_in(bb_im)], axis=-1).astype(BF16)
    wy = jnp.concatenate([pack_out(c_re), -pack_out(c_im)], axis=1).astype(BF16)
    return ab_re.reshape(1, g * p), ab_im.reshape(1, g * p), wbu, wy


def _wkv_body(r_ref, k_ref, v_ref, lw_ref, kkn_ref, bv_ref, g_ref, rk_ref, lng_ref, lnb_ref, s0_ref,
              y_ref, sl_ref, s_scr, at_scr, rt_scr, kt_scr, bt_scr, kd_scr, bd_scr, ce_scr, y_scr,
              *, tl, c, cpi, nh, hd):
    i = pl.program_id(1)

    @pl.when(i == 0)
    def _():
        s_scr[...] = s0_ref[...]

    rowi = lax.broadcasted_iota(jnp.int32, (tl, tl), 0)
    coli = lax.broadcasted_iota(jnp.int32, (tl, tl), 1)
    same_chunk = (rowi // c) == (coli // c)
    lw = lw_ref[...]
    cl = _mdot([(same_chunk & (coli <= rowi)).astype(BF16)], _split(lw, 3))
    ce = jnp.concatenate([jnp.broadcast_to(cl[j * c + c - 1:(j + 1) * c, :], (c, cl.shape[1]))
                          for j in range(tl // c)], axis=0)
    kkn = kkn_ref[...]
    bv = bv_ref[...]
    k = k_ref[...]
    einv = jnp.exp(-cl)
    to_end = jnp.exp(ce - cl)
    at_scr[...] = -kkn * jnp.exp(cl - lw)
    rt_scr[...] = r_ref[...] * jnp.exp(cl)
    kt_scr[...] = k * einv
    bt_scr[...] = bv * einv
    kd_scr[...] = k * to_end
    bd_scr[...] = bv * to_end
    ce_scr[...] = ce

    ri = lax.broadcasted_iota(jnp.int32, (c, c), 0)
    ci_ = lax.broadcasted_iota(jnp.int32, (c, c), 1)
    lower = ci_ <= ri
    strict = ci_ < ri
    levels = int(math.log2(c))
    sp = lambda x, cls: _split(x, WKV_TERMS[cls])

    def chunk_group(gi, carry):
        base = pl.multiple_of(gi * (cpi * c), cpi * c)
        units = [(cc, h) for cc in range(cpi) for h in range(nh)]
        rows = [pl.ds(base + cc * c, c) for cc, _ in units]
        hsl = [slice(h * hd, (h + 1) * hd) for _, h in units]
        un = range(len(units))
        at = [at_scr[rows[u], hsl[u]] for u in un]
        rt = [rt_scr[rows[u], hsl[u]] for u in un]
        lhs = [sp(jnp.concatenate([at[u], rt[u]], axis=0), "nt") for u in un]
        rhs = [sp(jnp.concatenate([bt_scr[rows[u], hsl[u]], kt_scr[rows[u], hsl[u]]], axis=0), "nt")
               for u in un]
        pm = [_mdot(lhs[u], rhs[u], _NT) for u in un]
        n_ab = [jnp.where(strict, pm[u][:c, :c], 0.0) for u in un]
        a_ak = [sp(jnp.where(strict, pm[u][:c, c:], 0.0), "out") for u in un]
        a_rb = [sp(jnp.where(lower, pm[u][c:, :c], 0.0), "out") for u in un]
        a_rk = [sp(jnp.where(lower, pm[u][c:, c:], 0.0), "out") for u in un]
        v_p = [sp(v_ref[rows[u], hsl[u]], "out") for u in un]

        x = [jnp.concatenate([at[u], _mdot(a_ak[u], v_p[u])], axis=1) for u in un]
        rkv = [_mdot(a_rk[u], v_p[u]) for u in un]
        pw = n_ab
        for lv in range(levels):
            pw_p = [sp(pw[u], "inv") for u in un]
            x = [x[u] + _mdot(pw_p[u], sp(x[u], "inv")) for u in un]
            if lv + 1 < levels:
                pw = [_mdot(pw_p[u], pw_p[u]) for u in un]
        x_p = [sp(x[u], "out") for u in un]
        qy = [jnp.concatenate([rt[u], rkv[u]], axis=1) + _mdot(a_rb[u], x_p[u]) for u in un]
        mz = [_mdot(x_p[u], sp(bd_scr[rows[u], hsl[u]], "out"), _TN) for u in un]
        zt = [mz[u][hd:, :] + _mdot(v_p[u], sp(kd_scr[rows[u], hsl[u]], "out"), _TN) for u in un]

        s_cur = [s_scr[h] for h in range(nh)]
        for cc in range(cpi):
            g_end = jnp.exp(ce_scr[pl.ds(base + cc * c, 1), :])
            s_p = [sp(s_cur[h], "state") for h in range(nh)]
            for h in range(nh):
                u = cc * nh + h
                y_scr[rows[u], hsl[u]] = _mdot(sp(qy[u][:, :hd], "state"), s_p[h], _NT) + qy[u][:, hd:]
            s_cur = [s_cur[h] * g_end[:, hsl[h]] + _mdot(s_p[h], sp(mz[cc * nh + h][:hd, :], "state"))
                     + zt[cc * nh + h] for h in range(nh)]
        for h in range(nh):
            s_scr[h] = s_cur[h]
        return carry

    lax.fori_loop(0, tl // (cpi * c), chunk_group, 0)

    @pl.when(i == pl.num_programs(1) - 1)
    def _():
        sl_ref[...] = s_scr[...]

    y = y_scr[...]
    yc = y - _head_sum(y, hd) * (1.0 / hd)
    yv = _head_sum(yc * yc, hd) * (1.0 / hd)
    yn = yc * lax.rsqrt(yv + GN_EPS) * lng_ref[...] + lnb_ref[...]
    bonus = _head_sum(r_ref[...] * k_ref[...] * rk_ref[...], hd) * v_ref[...]
    y_ref[...] = (yn + bonus) * g_ref[...]


def _wkv(r, k, v, lw, kkn, bv, g, prm, s0, tl, c):
    b, l, w = r.shape
    nh, hd = s0.shape[1], s0.shape[2]
    seq = pl.BlockSpec((None, tl, w), lambda bi, i: (bi, i, 0))
    st = pl.BlockSpec((None, nh, hd, hd), lambda bi, i: (bi, 0, 0, 0))
    vec = _full((1, w))
    return pl.pallas_call(
        functools.partial(_wkv_body, tl=tl, c=c, cpi=min(WKV_CHUNKS_PER_ITER, tl // c), nh=nh, hd=hd),
        grid=(b, l // tl),
        in_specs=[seq] * 7 + [vec, vec, vec, st],
        out_specs=[seq, st],
        out_shape=[jax.ShapeDtypeStruct((b, l, w), F32), jax.ShapeDtypeStruct(s0.shape, F32)],
        scratch_shapes=[pltpu.VMEM((nh, hd, hd), F32)] + [pltpu.VMEM((tl, w), F32)] * 8,
        compiler_params=_params(("parallel", "arbitrary")),
        name="wkv",
    )(r, k, v, lw, kkn, bv, g, prm["r_k"], prm["ln_g"], prm["ln_b"], s0)


def _merge_body(x_ref, ya_ref, yb_ref, gate_ref, p_ref, wpa_ref, wpb_ref, wo_ref, g1_ref, b1_ref,
                wpg_ref, bpg_ref, wple_ref, wr_ref, br_ref, x1_ref, base_ref, cmb_ref,
                *, d, ngroups, nper, nsplit):
    rs = x_ref.shape[0] // nsplit
    slabs = [slice(j * rs, (j + 1) * rs) for j in range(nsplit)]
    pa = [_bdot(ya_ref[s, :], wpa_ref[...]) for s in slabs]
    pb = [_bdot(yb_ref[s, :], wpb_ref[...]) for s in slabs]
    merged = [gate_ref[s, :d] * a + gate_ref[s, d:] * b for s, a, b in zip(slabs, pa, pb)]
    h = [ALPHA * x_ref[s, :] + _bdot(m, wo_ref[...]) for s, m in zip(slabs, merged)]
    x1s = [_layer_norm(v, g1_ref[...], b1_ref[...]) for v in h]
    pg = [_bdot(v, wpg_ref[...]) for v in x1s]
    pe = [_bdot(p_ref[s, :], wple_ref[...]) for s in slabs]
    wr_p = _split(wr_ref[...], 2)
    lg = [_mdot(_split(v, 2), wr_p) + br_ref[...] for v in x1s]
    for s, v, g_, e_, l_ in zip(slabs, x1s, pg, pe, lg):
        x1_ref[s, :] = v.astype(BF16)
        base_ref[s, :] = ALPHA * v + _sigmoid(g_ + bpg_ref[...]) * e_
        cmb_ref[s, :] = _route(l_, ngroups, nper)


def _route(logits, ngroups, nper):
    lane = lax.broadcasted_iota(jnp.int32, logits.shape, 1)
    neg = -jnp.inf
    big = logits.shape[1]
    first_of = lambda hit: jnp.min(jnp.where(hit, lane, big), axis=-1, keepdims=True)
    gl = jnp.where(lane < ngroups, logits, neg)
    gmax = jnp.max(gl, axis=-1, keepdims=True)
    g_sel = first_of(gl == gmax)
    g_w = 1.0 / jnp.sum(jnp.exp(gl - gmax), axis=-1, keepdims=True)
    lo = ngroups + nper * g_sel
    el = jnp.where((lane >= lo) & (lane < lo + nper), logits, neg)
    m1 = jnp.max(el, axis=-1, keepdims=True)
    i1 = first_of(el == m1)
    el2 = jnp.where(lane == i1, neg, el)
    m2 = jnp.max(el2, axis=-1, keepdims=True)
    i2 = first_of(el2 == m2)
    e2 = jnp.exp(m2 - m1)
    w1 = g_w / (1.0 + e2)
    w2 = g_w * e2 / (1.0 + e2)
    return (jnp.where(lane == i1, w1, 0.0) + jnp.where(lane == i2, w2, 0.0)
            + jnp.where(lane == 0, g_sel.astype(F32), 0.0))


def _merge(x, ya, yb, gate, p, layer, prm, tm):
    t, d = x.shape
    row = lambda n: pl.BlockSpec((tm, n), lambda i: (i, 0))
    ws = [prm["w_pa"], prm["w_pb"], prm["w_o"], prm["ln1_g"], prm["ln1_b"], prm["w_pg"], prm["b_pg"],
          prm["w_ple"], prm["w_router"], prm["b_router"]]
    return pl.pallas_call(
        functools.partial(_merge_body, d=d, ngroups=prm["ngroups"], nper=prm["nper"],
                          nsplit=MERGE_ROW_SLABS),
        grid=(t // tm,),
        in_specs=[row(d), row(ya.shape[1]), row(yb.shape[1]), row(gate.shape[1]),
                  pl.BlockSpec((None, tm, p.shape[2]), lambda i: (layer, i, 0))]
                 + [_full(a.shape) for a in ws],
        out_specs=[row(d), row(d), row(ROUTER_LANES)],
        out_shape=[jax.ShapeDtypeStruct((t, d), BF16), jax.ShapeDtypeStruct((t, d), F32),
                   jax.ShapeDtypeStruct((t, ROUTER_LANES), F32)],
        compiler_params=_params(("parallel",)),
        name="merge_ln1_router",
    )(x, ya, yb, gate, p, *ws)


def _moe_body(start_ref, nblk_ref, x1_ref, base_ref, cmb_ref, drow_ref, dcol_ref, wg_ref, wu_ref, wd_ref,
              g2_ref, b2_ref, o_ref, xs_scr, cs_scr, acc_scr, *, lane0, ngroups, nper, blk):
    i = pl.program_id(0)
    e = pl.program_id(1)
    nslot, tm = xs_scr.shape[0], x1_ref.shape[0]
    eps = wg_ref.shape[0]

    @pl.when(e == 0)
    def _():
        slot = lax.broadcasted_iota(jnp.int32, (nslot, tm), 0)
        pm = (slot == drow_ref[...]).astype(BF16)
        xs_scr[...] = jnp.dot(pm, x1_ref[...], preferred_element_type=F32).astype(BF16)
        cs_scr[...] = _bdot(pm, cmb_ref[...])
        acc_scr[...] = jnp.zeros_like(acc_scr)

    grp = (e * eps) // nper
    row0 = start_ref[i * ngroups + grp]
    lane = lax.broadcasted_iota(jnp.int32, (blk, cs_scr.shape[1]), 1)

    def block(j, carry):
        rows = pl.ds(pl.multiple_of(row0 + j * blk, MOE_ROW_ALIGN), blk)
        xb = xs_scr[rows, :]
        cs = cs_scr[rows, :]
        part = None
        for ee in range(eps):
            ce = jnp.sum(jnp.where(lane == e * eps + ee + lane0, cs, 0.0), axis=-1, keepdims=True)
            hg = jnp.dot(xb, wg_ref[ee], preferred_element_type=F32)
            hu = jnp.dot(xb, wu_ref[ee], preferred_element_type=F32)
            t = _bdot(hg * _sigmoid(hg) * hu * ce, wd_ref[ee])
            part = t if part is None else part + t
        acc_scr[rows, :] += part
        return carry

    lax.fori_loop(0, nblk_ref[i * ngroups + grp], block, 0)

    @pl.when(e == pl.num_programs(1) - 1)
    def _():
        slot_t = lax.broadcasted_iota(jnp.int32, (tm, nslot), 1)
        pmt = (slot_t == dcol_ref[...]).astype(BF16)
        ffn = _bdot(pmt, acc_scr[...])
        o_ref[...] = _layer_norm(base_ref[...] + ffn, g2_ref[...], b2_ref[...])


def _moe_dispatch(cmb, tm, ngroups, blk):
    t = cmb.shape[0]
    nt = t // tm
    g = cmb[:, 0].reshape(nt, tm)
    oh = (g[..., None] == jnp.arange(ngroups, dtype=F32)).astype(F32)
    before = (jnp.arange(tm)[None, :] < jnp.arange(tm)[:, None]).astype(F32)
    rank = jnp.einsum("ts,nsg->ntg", before, oh)
    count = jnp.sum(oh, axis=1).astype(jnp.int32)
    nblk = (count + blk - 1) // blk
    span = (count + MOE_ROW_ALIGN - 1) // MOE_ROW_ALIGN * MOE_ROW_ALIGN
    start = jnp.cumsum(span, axis=-1) - span
    dest = jnp.sum(oh * (rank + start[:, None, :].astype(F32)), axis=-1).astype(jnp.int32)
    return start.reshape(-1), nblk.reshape(-1), dest.reshape(nt, 1, tm), dest.reshape(t, 1)


def _moe(x1, base, cmb, prm, tm):
    t, d = base.shape
    _, ne, _, f = prm["w_gate"].shape
    layer = prm["layer"]
    ngroups, nper = prm["ngroups"], prm["nper"]
    eps = MOE_EXPERTS_PER_STEP
    blk = min(MOE_BLOCK_ROWS, tm)
    nslot = -(-(tm + ngroups * MOE_ROW_ALIGN + blk) // blk) * blk
    start, nblk, drow, dcol = _moe_dispatch(cmb, tm, ngroups, blk)
    row = lambda n: pl.BlockSpec((tm, n), lambda i, e, *_: (i, 0))
    wspec = lambda a, b: pl.BlockSpec((None, eps, a, b), lambda i, e, *_: (layer, e, 0, 0))
    vec = pl.BlockSpec((1, d), lambda i, e, *_: (0, 0))
    return pl.pallas_call(
        functools.partial(_moe_body, lane0=ngroups, ngroups=ngroups, nper=nper, blk=blk),
        grid_spec=pltpu.PrefetchScalarGridSpec(
            num_scalar_prefetch=2,
            grid=(t // tm, ne // eps),
            in_specs=[row(d), row(d), row(ROUTER_LANES),
                      pl.BlockSpec((None, 1, tm), lambda i, e, *_: (i, 0, 0)), row(1),
                      wspec(d, f), wspec(d, f), wspec(f, d), vec, vec],
            out_specs=row(d),
            scratch_shapes=[pltpu.VMEM((nslot, d), BF16), pltpu.VMEM((nslot, ROUTER_LANES), F32),
                            pltpu.VMEM((nslot, d), F32)]),
        out_shape=jax.ShapeDtypeStruct((t, d), F32),
        compiler_params=_params(("parallel", "arbitrary")),
        name="moe_ln2",
    )(start, nblk, x1, base, cmb, drow, dcol, prm["w_gate"], prm["w_up"], prm["w_down"],
      prm["ln2_g"], prm["ln2_b"])


def _tile(n, target):
    t = min(n, target)
    assert n % t == 0, (n, t)
    return t


def _layer(x, p, shift0, wkv0, s5r0, s5i0, v_first, lp):
    b, l, d = x.shape
    t = b * l
    s5w = lp["s5"]["d"].shape[-1]
    vf = None if v_first is None else v_first.reshape(t, -1)
    u, gate, r, k, v, lw, kkn, bv, g, shift_new = _inproj(
        x.reshape(t, d), lp["w_in"], lp["layer"], lp["b_gate"], shift0, lp["rw"], vf, l, s5w, _tile(t, 512))
    ya, s5r, s5i = _s5_mixer(u.reshape(b, l, s5w), s5r0, s5i0, lp["s5"], _tile(l, 256))
    seq3 = lambda a: a.reshape(b, l, -1)
    yb, wkv_new = _wkv(seq3(r), seq3(k), seq3(v), seq3(lw), seq3(kkn), seq3(bv), seq3(g), lp["rw"], wkv0,
                       _tile(l, 256), min(WKV_CHUNK, l))
    x1, base, cmb = _merge(x.reshape(t, d), ya.reshape(t, s5w), yb.reshape(t, -1), gate,
                           p.reshape(p.shape[0], t, -1), lp["layer"], lp["mrg"], _tile(t, 512))
    out = _moe(x1, base, cmb, lp["moe"], _tile(t, 1024))
    return out.reshape(b, l, d), shift_new, wkv_new, s5r, s5i, (seq3(v) if v_first is None else v_first)


def _trunk(x, p, shift0, wkv0, s50, layers):
    b = x.shape[0]
    shifts, wkvs, s5s = [], [], []
    v_first = None
    for i, lp in enumerate(layers):
        ns = s50.shape[2] * s50.shape[3]
        s5r0 = s50[i, ..., 0].reshape(b, 1, ns)
        s5i0 = s50[i, ..., 1].reshape(b, 1, ns)
        x, sh, wk, s5r, s5i, v_first = _layer(x, p, shift0[i], wkv0[i], s5r0, s5i0, v_first, lp)
        shifts.append(sh)
        wkvs.append(wk)
        s5s.append(jnp.stack([s5r.reshape(s50.shape[1:4]), s5i.reshape(s50.shape[1:4])], axis=-1))
    return x, jnp.stack(shifts), jnp.stack(wkvs), jnp.stack(s5s)


def kernel(x_prompt, x_sample, state_shift, state_wkv, state_s5, p_prompt, p_sample, w_in, b_gate, s5_a_re, s5_a_im, s5_log_dt, s5_b_re, s5_b_im, s5_c_re, s5_c_im, s5_d, s5_w_glu, s5_b_glu, rw_mu, rw_w0, rw_w2, rw_a0, rw_a2, rw_g2, rw_v0, rw_v1, rw_v2, rw_k_k, rw_k_a, rw_r_k, rw_ln_g, rw_ln_b, w_pa, w_pb, w_o, ln1_g, ln1_b, w_rg, b_rg, w_re, b_re, w_gate, w_up, w_down, w_ple, w_pg, b_pg, ln2_g, ln2_b):
    depth, d = w_in.shape[0], w_in.shape[1]
    ngroups = w_rg.shape[-1]
    nper = w_re.shape[-1]
    nexp = ngroups * nper
    w_in_all = w_in.astype(BF16)
    wg_all, wu_all, wd_all = w_gate.astype(BF16), w_up.astype(BF16), w_down.astype(BF16)
    layers = []
    for i in range(depth):
        abr, abi, wbu, wy = _s5_discretize(s5_a_re[i], s5_a_im[i], s5_log_dt[i], s5_b_re[i], s5_b_im[i],
                                           s5_c_re[i], s5_c_im[i])
        s5 = dict(abr=abr, abi=abi, wbu=wbu, wy=wy, d=s5_d[i][None], wglu=s5_w_glu[i].astype(BF16),
                  bglu=s5_b_glu[i][None])
        rw = dict(mu=rw_mu[i][None], w0=rw_w0[i][None], w2=rw_w2[i].astype(BF16), a0=rw_a0[i][None],
                  a2=rw_a2[i].astype(BF16), g2=rw_g2[i].astype(BF16), k_k=rw_k_k[i][None],
                  k_a=rw_k_a[i][None], r_k=rw_r_k[i].reshape(1, -1), ln_g=rw_ln_g[i][None],
                  ln_b=rw_ln_b[i][None])
        if i > 0:
            rw.update(v0=rw_v0[i - 1][None], v1=rw_v1[i - 1].astype(BF16), v2=rw_v2[i - 1].astype(BF16))
        pad = ROUTER_LANES - ngroups - nexp
        w_router = jnp.concatenate([w_rg[i], w_re[i].reshape(d, nexp), jnp.zeros((d, pad), F32)], axis=1)
        b_router = jnp.concatenate([b_rg[i], b_re[i].reshape(nexp), jnp.zeros((pad,), F32)])[None]
        mrg = dict(w_pa=w_pa[i].astype(BF16), w_pb=w_pb[i].astype(BF16), w_o=w_o[i].astype(BF16),
                   ln1_g=ln1_g[i][None], ln1_b=ln1_b[i][None], w_pg=w_pg[i].astype(BF16),
                   b_pg=b_pg[i][None], w_ple=w_ple[i].astype(BF16), w_router=w_router,
                   b_router=b_router, ngroups=ngroups, nper=nper)
        moe = dict(w_gate=wg_all, w_up=wu_all, w_down=wd_all, layer=i, ln2_g=ln2_g[i][None],
                   ln2_b=ln2_b[i][None], ngroups=ngroups, nper=nper)
        layers.append(dict(w_in=w_in_all, layer=i, b_gate=b_gate[i][None], s5=s5, rw=rw, mrg=mrg,
                           moe=moe))

    bp = x_prompt.shape[0]
    dt = x_prompt.dtype
    zero_shift = jnp.zeros((depth, bp) + state_shift.shape[2:], dt)
    zero_wkv = jnp.zeros((depth, bp) + state_wkv.shape[2:], dt)
    zero_s5 = jnp.zeros((depth, bp) + state_s5.shape[2:], dt)
    y_p, sh_p, wkv_p, s5_p = _trunk(x_prompt, p_prompt, zero_shift, zero_wkv, zero_s5, layers)
    y_s, sh_s, wkv_s, s5_s = _trunk(x_sample, p_sample, state_shift, state_wkv, state_s5, layers)
    return (y_p, y_s, sh_p, wkv_p, s5_p, sh_s, wkv_s, s5_s)
```

```python
import functools
import math

import jax
import jax.numpy as jnp
from jax import lax
from jax.experimental import pallas as pl
from jax.experimental.pallas import tpu as pltpu

F32 = jnp.float32
BF16 = jnp.bfloat16

DEPTH = 2
ALPHA = (2 * DEPTH) ** 0.25
LN_EPS = 1e-5
GN_EPS = 64e-5

S5_GROUP = 16
S5_GROUPS_PER_BLOCK = 8
RW_HEAD = 64
DECAY_SCALE = math.exp(-0.5)
ROUTER_LANES = 128
WKV_CHUNK = 64
VMEM_LIMIT = 56 * 1024 * 1024
WKV_TERMS = {"nt": 1, "inv": 1, "out": 1, "state": 1}
WKV_CHUNKS_PER_ITER = 4
MOE_EXPERTS_PER_STEP = 2
S5_SLAB_ROWS = 256
INPROJ_ROW_SLABS = 2
MERGE_ROW_SLABS = 2
MOE_BLOCK_ROWS = 128
MOE_ROW_ALIGN = 16


def _bdot(a, b):
    return jnp.dot(a.astype(BF16), b.astype(BF16), preferred_element_type=F32)


def _split(x, n):
    parts = []
    rem = x
    for j in range(n):
        p = rem.astype(BF16)
        parts.append(p)
        if j + 1 < n:
            rem = rem - p.astype(F32)
    return parts


_NN = (((1,), (0,)), ((), ()))
_NT = (((1,), (1,)), ((), ()))
_TN = (((0,), (0,)), ((), ()))


def _mdot(a_parts, b_parts, dims=_NN):
    order = max(len(a_parts), len(b_parts))
    acc = None
    for ia, pa in enumerate(a_parts):
        for ib, pb in enumerate(b_parts):
            if ia + ib < order:
                t = lax.dot_general(pa, pb, dims, preferred_element_type=F32)
                acc = t if acc is None else acc + t
    return acc


def _head_sum(x, hd, terms=1):
    n = x.shape[-1]
    same = (lax.broadcasted_iota(jnp.int32, (n, n), 0) // hd
            == lax.broadcasted_iota(jnp.int32, (n, n), 1) // hd)
    return _mdot(_split(x, terms), [same.astype(BF16)])


def _sigmoid(x):
    return 1.0 / (1.0 + jnp.exp(-x))


def _layer_norm(h, g, b):
    mu = jnp.mean(h, axis=-1, keepdims=True)
    d = h - mu
    var = jnp.mean(d * d, axis=-1, keepdims=True)
    return d * lax.rsqrt(var + LN_EPS) * g + b


def _params(sem):
    return pltpu.CompilerParams(dimension_semantics=sem, vmem_limit_bytes=VMEM_LIMIT)


def _full(shape):
    n = len(shape)
    return pl.BlockSpec(shape, lambda *_: (0,) * n)


def _of_layer(arr, layer):
    tail = arr.shape[1:]
    return pl.BlockSpec((None,) + tail, lambda *_: (layer,) + (0,) * len(tail))


def _inproj_body(*refs, s5w, rwc, w, dr, ar_, hd, seq_len, nslab, has_vmix):
    (x_ref, win_ref, bg_ref, sh0_ref, mu_ref, w0_ref, w2_ref, a0_ref, a2_ref, g2_ref, kk_ref,
     ka_ref) = refs[:12]
    refs = refs[12:]
    if has_vmix:
        vf_ref, v0_ref, v1_ref, v2_ref = refs[:4]
        refs = refs[4:]
    u_ref, gate_ref, r_ref, k_ref, v_ref, lw_ref, kkn_ref, bv_ref, g_ref, zlast_ref, carry = refs

    i = pl.program_id(0)
    tm = x_ref.shape[0]
    rs = tm // nslab
    slabs = [slice(j * rs, (j + 1) * rs) for j in range(nslab)]
    xb = [x_ref[s, :].astype(BF16) for s in slabs]
    for s, xv in zip(slabs, xb):
        u_ref[s, :] = jnp.dot(xv, win_ref[:, :s5w], preferred_element_type=F32)
    zr = [jnp.dot(xv, win_ref[:, s5w:s5w + rwc], preferred_element_type=F32) for xv in xb]
    for s, xv in zip(slabs, xb):
        zg = jnp.dot(xv, win_ref[:, s5w + rwc:], preferred_element_type=F32)
        gate_ref[s, :] = _sigmoid(zg + bg_ref[...])

    row = lax.broadcasted_iota(jnp.int32, (rs, rwc), 0)

    @pl.when(i == 0)
    def _():
        carry[...] = jnp.zeros_like(carry)

    zs = []
    for j, z in enumerate(zr):
        if j > 0:
            before = zr[j - 1][rs - 1:rs, :]
        elif tm <= seq_len:
            before = jnp.where((i * tm) % seq_len == 0, sh0_ref[0], carry[...])
        else:
            before = carry[...]
        prev = jnp.where(row == 0, before, pltpu.roll(z, 1, axis=0))
        for bb in range(tm // seq_len):
            if j * rs <= bb * seq_len < (j + 1) * rs:
                prev = jnp.where(row == bb * seq_len - j * rs, sh0_ref[bb], prev)
        zs.append(z + mu_ref[...] * (prev - z))
    carry[...] = zr[-1][rs - 1:rs, :]
    if tm <= seq_len:
        zlast_ref[0] = zr[-1][rs - 1:rs, :]
    else:
        for bb in range(tm // seq_len):
            last = (bb + 1) * seq_len - 1
            zlast_ref[bb] = zr[last // rs][last % rs:last % rs + 1, :]

    c3 = 3 * w
    wl = [w0_ref[...] + _bdot(jnp.tanh(z[:, c3:c3 + dr]), w2_ref[...]) for z in zs]
    a = [_sigmoid(a0_ref[...] + _bdot(z[:, c3 + dr:c3 + dr + ar_], a2_ref[...])) for z in zs]
    g = [_bdot(_sigmoid(z[:, c3 + dr + ar_:]), g2_ref[...]) for z in zs]
    v = [z[:, 2 * w:c3] for z in zs]
    if has_vmix:
        mix = [_sigmoid(v0_ref[...] + _bdot(_bdot(vv, v1_ref[...]), v2_ref[...])) for vv in v]
        v = [vv + (vf_ref[s, :] - vv) * m for s, vv, m in zip(slabs, v, mix)]
    kk = [z[:, w:2 * w] * kk_ref[...] for z in zs]
    ss = [_head_sum(q * q, hd) for q in kk]
    for j, s in enumerate(slabs):
        lw_ref[s, :] = -DECAY_SCALE * _sigmoid(wl[j])
        g_ref[s, :] = g[j]
        r_ref[s, :] = zs[j][:, :w]
        v_ref[s, :] = v[j]
        kkn = kk[j] * lax.rsqrt(jnp.maximum(ss[j], 1e-24))
        kkn_ref[s, :] = kkn
        bv_ref[s, :] = kkn * a[j]
        k_ref[s, :] = zs[j][:, w:2 * w] * (1.0 + (a[j] - 1.0) * ka_ref[...])


def _inproj(x, prm, layer, shift0, v_first, seq_len, tm):
    t, d = x.shape
    n_in = prm["w_in"].shape[2]
    rwc = prm["mu"].shape[-1]
    w = prm["w0"].shape[-1]
    s5w = prm["s5_d"].shape[-1]
    ng = n_in - s5w - rwc
    has_vmix = v_first is not None
    nseq = max(1, tm // seq_len)
    row = lambda n: pl.BlockSpec((tm, n), lambda i: (i, 0))
    names = ["w_in", "b_gate", None, "mu", "w0", "w2", "a0", "a2", "g2", "k_k", "k_a"]
    ins = [x] + [shift0 if n is None else prm[n] for n in names]
    specs = [row(d)] + [pl.BlockSpec((nseq, 1, rwc), lambda i: ((i * tm) // (seq_len * nseq), 0, 0))
                        if n is None else _of_layer(prm[n], layer) for n in names]
    if has_vmix:
        ins += [v_first] + [prm[n] for n in ("v0", "v1", "v2")]
        specs += [row(w)] + [_of_layer(prm[n], layer - 1) for n in ("v0", "v1", "v2")]
    ntile = t // tm
    outs = pl.pallas_call(
        functools.partial(_inproj_body, s5w=s5w, rwc=rwc, w=w, dr=prm["w2"].shape[1],
                          ar_=prm["a2"].shape[1], hd=RW_HEAD, seq_len=seq_len, nslab=INPROJ_ROW_SLABS,
                          has_vmix=has_vmix),
        grid=(ntile,),
        in_specs=specs,
        out_specs=[row(s5w), row(ng)] + [row(w)] * 7
                  + [pl.BlockSpec((nseq, 1, rwc), lambda i: (i, 0, 0))],
        out_shape=[jax.ShapeDtypeStruct((t, s5w), F32), jax.ShapeDtypeStruct((t, ng), F32)]
                  + [jax.ShapeDtypeStruct((t, w), F32)] * 7
                  + [jax.ShapeDtypeStruct((ntile * nseq, 1, rwc), F32)],
        scratch_shapes=[pltpu.VMEM((1, rwc), F32)],
        compiler_params=_params(("arbitrary",)),
        name="inproj_premix",
    )(*ins)
    zlast = outs[-1].reshape(t // seq_len, -1, 1, rwc)[:, -1]
    return list(outs[:-1]) + [zlast]


def _s5_body(u_ref, h0r_ref, h0i_ref, abr_ref, abi_ref, wbu_ref, wy_ref, d_ref, wglu_ref, bglu_ref,
             y_ref, hlr_ref, hli_ref, bur, bui, hr, hi, str_, sti, *, nb, tl, sub, nblk, cb, sb):
    i = pl.program_id(0)

    @pl.when(i == 0)
    def _():
        str_[...] = h0r_ref[...]
        sti[...] = h0i_ref[...]

    slabs = [(bi, s0) for s0 in range(0, tl, sub) for bi in range(nb)]
    for bi, s0 in slabs:
        u = u_ref[bi, s0:s0 + sub, :]
        for j in range(nblk):
            bu = _bdot(u[:, cb * j:cb * (j + 1)], wbu_ref[j])
            bur[bi, s0:s0 + sub, sb * j:sb * (j + 1)] = bu[:, :sb]
            bui[bi, s0:s0 + sub, sb * j:sb * (j + 1)] = bu[:, sb:]

    ar = abr_ref[...]
    ai = abi_ref[...]
    state = [(str_[bi], sti[bi]) for bi in range(nb)]
    for s0 in range(0, tl, sub):
        for t in range(s0, s0 + sub):
            for bi in range(nb):
                pr, pi = state[bi]
                nr = ar * pr - ai * pi + bur[bi, t:t + 1, :]
                ni = ar * pi + ai * pr + bui[bi, t:t + 1, :]
                hr[bi, t:t + 1, :] = nr
                hi[bi, t:t + 1, :] = ni
                state[bi] = (nr, ni)
        for bi in range(nb):
            u = u_ref[bi, s0:s0 + sub, :]
            ys = [_bdot(hr[bi, s0:s0 + sub, sb * j:sb * (j + 1)], wy_ref[j, :sb, :])
                  + _bdot(hi[bi, s0:s0 + sub, sb * j:sb * (j + 1)], wy_ref[j, sb:, :])
                  for j in range(nblk)]
            y = jax.nn.gelu(jnp.concatenate(ys, axis=1) + d_ref[...] * u)
            y_ref[bi, s0:s0 + sub, :] = y * _sigmoid(_bdot(y, wglu_ref[...]) + bglu_ref[...])
    for bi in range(nb):
        str_[bi] = state[bi][0]
        sti[bi] = state[bi][1]
        hlr_ref[bi] = state[bi][0]
        hli_ref[bi] = state[bi][1]


def _s5_mixer(u, h0r, h0i, prm, layer, tl):
    b, l, w = u.shape
    ns = h0r.shape[-1]
    nblk = w // (S5_GROUP * S5_GROUPS_PER_BLOCK)
    cb = w // nblk
    sb = ns // nblk
    seq = pl.BlockSpec((b, tl, w), lambda i: (0, i, 0))
    st = _full((b, 1, ns))
    names = ("s5_abr", "s5_abi", "s5_wbu", "s5_wy", "s5_d", "s5_wglu", "s5_bglu")
    return pl.pallas_call(
        functools.partial(_s5_body, nb=b, tl=tl, sub=min(S5_SLAB_ROWS, tl), nblk=nblk, cb=cb, sb=sb),
        grid=(l // tl,),
        in_specs=[seq, st, st] + [_of_layer(prm[n], layer) for n in names],
        out_specs=[seq, st, st],
        out_shape=[jax.ShapeDtypeStruct((b, l, w), F32), jax.ShapeDtypeStruct((b, 1, ns), F32),
                   jax.ShapeDtypeStruct((b, 1, ns), F32)],
        scratch_shapes=[pltpu.VMEM((b, tl, ns), F32)] * 4 + [pltpu.VMEM((b, 1, ns), F32)] * 2,
        compiler_params=_params(("arbitrary",)),
        name="s5_mixer",
    )(u, h0r, h0i, *[prm[n] for n in names])


def _s5_discretize(a_re, a_im, log_dt, b_re, b_im, c_re, c_im):
    nl, g, p = a_re.shape
    c = b_re.shape[-1]
    gb = S5_GROUPS_PER_BLOCK
    nblk = g // gb
    dt = jnp.exp(log_dt)[..., None]
    mag = jnp.exp(a_re * dt)
    ab_re = mag * jnp.cos(a_im * dt)
    ab_im = mag * jnp.sin(a_im * dt)
    den = a_re * a_re + a_im * a_im
    n_re = ab_re - 1.0
    k_re = (n_re * a_re + ab_im * a_im) / den
    k_im = (ab_im * a_re - n_re * a_im) / den
    bb_re = k_re[..., None] * b_re - k_im[..., None] * b_im
    bb_im = k_re[..., None] * b_im + k_im[..., None] * b_re
    eye = jnp.eye(gb, dtype=F32)

    def pack_in(bb):
        return jnp.einsum("ljgpc,gh->ljgchp", bb.reshape(nl, nblk, gb, p, c),
                          eye).reshape(nl, nblk, gb * c, gb * p)

    def pack_out(cc):
        return jnp.einsum("ljgcp,gh->ljgphc", cc.reshape(nl, nblk, gb, c, p),
                          eye).reshape(nl, nblk, gb * p, gb * c)

    wbu = jnp.concatenate([pack_in(bb_re), pack_in(bb_im)], axis=-1).astype(BF16)
    wy = jnp.concatenate([pack_out(c_re), -pack_out(c_im)], axis=2).astype(BF16)
    return ab_re.reshape(nl, 1, g * p), ab_im.reshape(nl, 1, g * p), wbu, wy


def _wkv_body(r_ref, k_ref, v_ref, lw_ref, kkn_ref, bv_ref, g_ref, rk_ref, lng_ref, lnb_ref, s0_ref,
              y_ref, sl_ref, s_scr, at_scr, rt_scr, kt_scr, bt_scr, kd_scr, bd_scr, ce_scr, y_scr,
              *, tl, c, cpi, nh, hd):
    i = pl.program_id(1)

    @pl.when(i == 0)
    def _():
        s_scr[...] = s0_ref[...]

    rowi = lax.broadcasted_iota(jnp.int32, (tl, tl), 0)
    coli = lax.broadcasted_iota(jnp.int32, (tl, tl), 1)
    same_chunk = (rowi // c) == (coli // c)
    lw = lw_ref[...]
    cl = _mdot([(same_chunk & (coli <= rowi)).astype(BF16)], _split(lw, 3))
    ce = jnp.concatenate([jnp.broadcast_to(cl[j * c + c - 1:(j + 1) * c, :], (c, cl.shape[1]))
                          for j in range(tl // c)], axis=0)
    kkn = kkn_ref[...]
    bv = bv_ref[...]
    k = k_ref[...]
    einv = jnp.exp(-cl)
    to_end = jnp.exp(ce - cl)
    at_scr[...] = -kkn * jnp.exp(cl - lw)
    rt_scr[...] = r_ref[...] * jnp.exp(cl)
    kt_scr[...] = k * einv
    bt_scr[...] = bv * einv
    kd_scr[...] = k * to_end
    bd_scr[...] = bv * to_end
    ce_scr[...] = ce

    ri = lax.broadcasted_iota(jnp.int32, (c, c), 0)
    ci_ = lax.broadcasted_iota(jnp.int32, (c, c), 1)
    lower = ci_ <= ri
    strict = ci_ < ri
    levels = int(math.log2(c))
    sp = lambda x, cls: _split(x, WKV_TERMS[cls])

    def chunk_group(gi, carry):
        base = pl.multiple_of(gi * (cpi * c), cpi * c)
        units = [(cc, h) for cc in range(cpi) for h in range(nh)]
        rows = [pl.ds(base + cc * c, c) for cc, _ in units]
        hsl = [slice(h * hd, (h + 1) * hd) for _, h in units]
        un = range(len(units))
        at = [at_scr[rows[u], hsl[u]] for u in un]
        rt = [rt_scr[rows[u], hsl[u]] for u in un]
        lhs = [sp(jnp.concatenate([at[u], rt[u]], axis=0), "nt") for u in un]
        rhs = [sp(jnp.concatenate([bt_scr[rows[u], hsl[u]], kt_scr[rows[u], hsl[u]]], axis=0), "nt")
               for u in un]
        pm = [_mdot(lhs[u], rhs[u], _NT) for u in un]
        n_ab = [jnp.where(strict, pm[u][:c, :c], 0.0) for u in un]
        a_ak = [sp(jnp.where(strict, pm[u][:c, c:], 0.0), "out") for u in un]
        a_rb = [sp(jnp.where(lower, pm[u][c:, :c], 0.0), "out") for u in un]
        a_rk = [sp(jnp.where(lower, pm[u][c:, c:], 0.0), "out") for u in un]
        v_p = [sp(v_ref[rows[u], hsl[u]], "out") for u in un]

        x = [jnp.concatenate([at[u], _mdot(a_ak[u], v_p[u])], axis=1) for u in un]
        rkv = [_mdot(a_rk[u], v_p[u]) for u in un]
        pw = n_ab
        for lv in range(levels):
            pw_p = [sp(pw[u], "inv") for u in un]
            x = [x[u] + _mdot(pw_p[u], sp(x[u], "inv")) for u in un]
            if lv + 1 < levels:
                pw = [_mdot(pw_p[u], pw_p[u]) for u in un]
        x_p = [sp(x[u], "out") for u in un]
        qy = [jnp.concatenate([rt[u], rkv[u]], axis=1) + _mdot(a_rb[u], x_p[u]) for u in un]
        mz = [_mdot(x_p[u], sp(bd_scr[rows[u], hsl[u]], "out"), _TN) for u in un]
        zt = [mz[u][hd:, :] + _mdot(v_p[u], sp(kd_scr[rows[u], hsl[u]], "out"), _TN) for u in un]

        s_cur = [s_scr[h] for h in range(nh)]
        for cc in range(cpi):
            g_end = jnp.exp(ce_scr[pl.ds(base + cc * c, 1), :])
            s_p = [sp(s_cur[h], "state") for h in range(nh)]
            for h in range(nh):
                u = cc * nh + h
                y_scr[rows[u], hsl[u]] = _mdot(sp(qy[u][:, :hd], "state"), s_p[h], _NT) + qy[u][:, hd:]
            s_cur = [s_cur[h] * g_end[:, hsl[h]] + _mdot(s_p[h], sp(mz[cc * nh + h][:hd, :], "state"))
                     + zt[cc * nh + h] for h in range(nh)]
        for h in range(nh):
            s_scr[h] = s_cur[h]
        return carry

    lax.fori_loop(0, tl // (cpi * c), chunk_group, 0)

    @pl.when(i == pl.num_programs(1) - 1)
    def _():
        sl_ref[...] = s_scr[...]

    y = y_scr[...]
    yc = y - _head_sum(y, hd) * (1.0 / hd)
    yv = _head_sum(yc * yc, hd) * (1.0 / hd)
    yn = yc * lax.rsqrt(yv + GN_EPS) * lng_ref[...] + lnb_ref[...]
    bonus = _head_sum(r_ref[...] * k_ref[...] * rk_ref[...], hd) * v_ref[...]
    y_ref[...] = (yn + bonus) * g_ref[...]


def _wkv(r, k, v, lw, kkn, bv, g, prm, layer, s0, tl, c):
    b, l, w = r.shape
    nh, hd = s0.shape[1], s0.shape[2]
    seq = pl.BlockSpec((None, tl, w), lambda bi, i: (bi, i, 0))
    st = pl.BlockSpec((None, nh, hd, hd), lambda bi, i: (bi, 0, 0, 0))
    return pl.pallas_call(
        functools.partial(_wkv_body, tl=tl, c=c, cpi=min(WKV_CHUNKS_PER_ITER, tl // c), nh=nh, hd=hd),
        grid=(b, l // tl),
        in_specs=[seq] * 7 + [_of_layer(prm[n], layer) for n in ("r_k", "ln_g", "ln_b")] + [st],
        out_specs=[seq, st],
        out_shape=[jax.ShapeDtypeStruct((b, l, w), F32), jax.ShapeDtypeStruct(s0.shape, F32)],
        scratch_shapes=[pltpu.VMEM((nh, hd, hd), F32)] + [pltpu.VMEM((tl, w), F32)] * 8,
        compiler_params=_params(("parallel", "arbitrary")),
        name="wkv",
    )(r, k, v, lw, kkn, bv, g, prm["r_k"], prm["ln_g"], prm["ln_b"], s0)


def _merge_body(x_ref, ya_ref, yb_ref, gate_ref, p_ref, wpa_ref, wpb_ref, wo_ref, g1_ref, b1_ref,
                wpg_ref, bpg_ref, wple_ref, wr_ref, br_ref, x1_ref, base_ref, cmb_ref,
                *, d, ngroups, nper, nsplit):
    rs = x_ref.shape[0] // nsplit
    slabs = [slice(j * rs, (j + 1) * rs) for j in range(nsplit)]
    pa = [_bdot(ya_ref[s, :], wpa_ref[...]) for s in slabs]
    pb = [_bdot(yb_ref[s, :], wpb_ref[...]) for s in slabs]
    merged = [gate_ref[s, :d] * a + gate_ref[s, d:] * b for s, a, b in zip(slabs, pa, pb)]
    h = [ALPHA * x_ref[s, :] + _bdot(m, wo_ref[...]) for s, m in zip(slabs, merged)]
    x1s = [_layer_norm(v, g1_ref[...], b1_ref[...]) for v in h]
    pg = [_bdot(v, wpg_ref[...]) for v in x1s]
    pe = [_bdot(p_ref[s, :], wple_ref[...]) for s in slabs]
    wr_p = _split(wr_ref[...], 2)
    lg = [_mdot(_split(v, 2), wr_p) + br_ref[...] for v in x1s]
    for s, v, g_, e_, l_ in zip(slabs, x1s, pg, pe, lg):
        x1_ref[s, :] = v.astype(BF16)
        base_ref[s, :] = ALPHA * v + _sigmoid(g_ + bpg_ref[...]) * e_
        cmb_ref[s, :] = _route(l_, ngroups, nper)


def _route(logits, ngroups, nper):
    lane = lax.broadcasted_iota(jnp.int32, logits.shape, 1)
    neg = -jnp.inf
    big = logits.shape[1]
    first_of = lambda hit: jnp.min(jnp.where(hit, lane, big), axis=-1, keepdims=True)
    gl = jnp.where(lane < ngroups, logits, neg)
    gmax = jnp.max(gl, axis=-1, keepdims=True)
    g_sel = first_of(gl == gmax)
    g_w = 1.0 / jnp.sum(jnp.exp(gl - gmax), axis=-1, keepdims=True)
    lo = ngroups + nper * g_sel
    el = jnp.where((lane >= lo) & (lane < lo + nper), logits, neg)
    m1 = jnp.max(el, axis=-1, keepdims=True)
    i1 = first_of(el == m1)
    el2 = jnp.where(lane == i1, neg, el)
    m2 = jnp.max(el2, axis=-1, keepdims=True)
    i2 = first_of(el2 == m2)
    e2 = jnp.exp(m2 - m1)
    w1 = g_w / (1.0 + e2)
    w2 = g_w * e2 / (1.0 + e2)
    return (jnp.where(lane == i1, w1, 0.0) + jnp.where(lane == i2, w2, 0.0)
            + jnp.where(lane == 0, g_sel.astype(F32), 0.0))


def _merge(x, ya, yb, gate, p, layer, prm, tm):
    t, d = x.shape
    row = lambda n: pl.BlockSpec((tm, n), lambda i: (i, 0))
    ws = [prm[n] for n in ("w_pa", "w_pb", "w_o", "ln1_g", "ln1_b", "w_pg", "b_pg", "w_ple", "w_router",
                           "b_router")]
    return pl.pallas_call(
        functools.partial(_merge_body, d=d, ngroups=prm["ngroups"], nper=prm["nper"],
                          nsplit=MERGE_ROW_SLABS),
        grid=(t // tm,),
        in_specs=[row(d), row(ya.shape[1]), row(yb.shape[1]), row(gate.shape[1]),
                  pl.BlockSpec((None, tm, p.shape[2]), lambda i: (layer, i, 0))]
                 + [_of_layer(a, layer) for a in ws],
        out_specs=[row(d), row(d), row(ROUTER_LANES)],
        out_shape=[jax.ShapeDtypeStruct((t, d), BF16), jax.ShapeDtypeStruct((t, d), F32),
                   jax.ShapeDtypeStruct((t, ROUTER_LANES), F32)],
        compiler_params=_params(("parallel",)),
        name="merge_ln1_router",
    )(x, ya, yb, gate, p, *ws)


def _moe_body(start_ref, nblk_ref, x1_ref, base_ref, cmb_ref, drow_ref, dcol_ref, wg_ref, wu_ref, wd_ref,
              g2_ref, b2_ref, o_ref, xs_scr, cs_scr, acc_scr, *, lane0, ngroups, nper, blk):
    i = pl.program_id(0)
    e = pl.program_id(1)
    nslot, tm = xs_scr.shape[0], x1_ref.shape[0]
    eps = wg_ref.shape[0]

    @pl.when(e == 0)
    def _():
        slot = lax.broadcasted_iota(jnp.int32, (nslot, tm), 0)
        pm = (slot == drow_ref[...]).astype(BF16)
        xs_scr[...] = jnp.dot(pm, x1_ref[...], preferred_element_type=F32).astype(BF16)
        cs_scr[...] = _bdot(pm, cmb_ref[...])
        acc_scr[...] = jnp.zeros_like(acc_scr)

    grp = (e * eps) // nper
    row0 = start_ref[i * ngroups + grp]
    lane = lax.broadcasted_iota(jnp.int32, (blk, cs_scr.shape[1]), 1)

    def block(j, carry):
        rows = pl.ds(pl.multiple_of(row0 + j * blk, MOE_ROW_ALIGN), blk)
        xb = xs_scr[rows, :]
        cs = cs_scr[rows, :]
        part = None
        for ee in range(eps):
            ce = jnp.sum(jnp.where(lane == e * eps + ee + lane0, cs, 0.0), axis=-1, keepdims=True)
            hg = jnp.dot(xb, wg_ref[ee], preferred_element_type=F32)
            hu = jnp.dot(xb, wu_ref[ee], preferred_element_type=F32)
            t = _bdot(hg * _sigmoid(hg) * hu * ce, wd_ref[ee])
            part = t if part is None else part + t
        acc_scr[rows, :] += part
        return carry

    lax.fori_loop(0, nblk_ref[i * ngroups + grp], block, 0)

    @pl.when(e == pl.num_programs(1) - 1)
    def _():
        slot_t = lax.broadcasted_iota(jnp.int32, (tm, nslot), 1)
        pmt = (slot_t == dcol_ref[...]).astype(BF16)
        ffn = _bdot(pmt, acc_scr[...])
        o_ref[...] = _layer_norm(base_ref[...] + ffn, g2_ref[...], b2_ref[...])


def _moe_dispatch(cmb, tm, ngroups, blk):
    t = cmb.shape[0]
    nt = t // tm
    g = cmb[:, 0].reshape(nt, tm)
    oh = (g[..., None] == jnp.arange(ngroups, dtype=F32)).astype(F32)
    before = (jnp.arange(tm)[None, :] < jnp.arange(tm)[:, None]).astype(F32)
    rank = jnp.einsum("ts,nsg->ntg", before, oh)
    count = jnp.sum(oh, axis=1).astype(jnp.int32)
    nblk = (count + blk - 1) // blk
    span = (count + MOE_ROW_ALIGN - 1) // MOE_ROW_ALIGN * MOE_ROW_ALIGN
    start = jnp.cumsum(span, axis=-1) - span
    dest = jnp.sum(oh * (rank + start[:, None, :].astype(F32)), axis=-1).astype(jnp.int32)
    return start.reshape(-1), nblk.reshape(-1), dest.reshape(nt, 1, tm), dest.reshape(t, 1)


def _moe(x1, base, cmb, prm, layer, tm):
    t, d = base.shape
    _, ne, _, f = prm["w_gate"].shape
    ngroups, nper = prm["ngroups"], prm["nper"]
    eps = MOE_EXPERTS_PER_STEP
    blk = min(MOE_BLOCK_ROWS, tm)
    nslot = -(-(tm + ngroups * MOE_ROW_ALIGN + blk) // blk) * blk
    start, nblk, drow, dcol = _moe_dispatch(cmb, tm, ngroups, blk)
    row = lambda n: pl.BlockSpec((tm, n), lambda i, e, *_: (i, 0))
    wspec = lambda a, b: pl.BlockSpec((None, eps, a, b), lambda i, e, *_: (layer, e, 0, 0))
    vec = _of_layer(prm["ln2_g"], layer)
    return pl.pallas_call(
        functools.partial(_moe_body, lane0=ngroups, ngroups=ngroups, nper=nper, blk=blk),
        grid_spec=pltpu.PrefetchScalarGridSpec(
            num_scalar_prefetch=2,
            grid=(t // tm, ne // eps),
            in_specs=[row(d), row(d), row(ROUTER_LANES),
                      pl.BlockSpec((None, 1, tm), lambda i, e, *_: (i, 0, 0)), row(1),
                      wspec(d, f), wspec(d, f), wspec(f, d), vec, vec],
            out_specs=row(d),
            scratch_shapes=[pltpu.VMEM((nslot, d), BF16), pltpu.VMEM((nslot, ROUTER_LANES), F32),
                            pltpu.VMEM((nslot, d), F32)]),
        out_shape=jax.ShapeDtypeStruct((t, d), F32),
        compiler_params=_params(("parallel", "arbitrary")),
        name="moe_ln2",
    )(start, nblk, x1, base, cmb, drow, dcol, prm["w_gate"], prm["w_up"], prm["w_down"],
      prm["ln2_g"], prm["ln2_b"])


def _tile(n, target):
    t = min(n, target)
    assert n % t == 0, (n, t)
    return t


def _layer(x, p, shift0, wkv0, s5r0, s5i0, v_first, prm, layer):
    b, l, d = x.shape
    t = b * l
    vf = None if v_first is None else v_first.reshape(t, -1)
    u, gate, r, k, v, lw, kkn, bv, g, shift_new = _inproj(x.reshape(t, d), prm, layer, shift0, vf, l,
                                                          _tile(t, 512))
    seq3 = lambda a: a.reshape(b, l, -1)
    ya, s5r, s5i = _s5_mixer(seq3(u), s5r0, s5i0, prm, layer, _tile(l, 256))
    yb, wkv_new = _wkv(seq3(r), seq3(k), seq3(v), seq3(lw), seq3(kkn), seq3(bv), seq3(g), prm, layer, wkv0,
                       _tile(l, 256), min(WKV_CHUNK, l))
    x1, base, cmb = _merge(x.reshape(t, d), ya.reshape(t, -1), yb.reshape(t, -1), gate,
                           p.reshape(p.shape[0], t, -1), layer, prm, _tile(t, 512))
    out = _moe(x1, base, cmb, prm, layer, _tile(t, 1024))
    return out.reshape(b, l, d), shift_new, wkv_new, s5r, s5i, (seq3(v) if v_first is None else v_first)


def _trunk(x, p, shift0, wkv0, s50, prm, depth):
    b = x.shape[0]
    shifts, wkvs, s5s = [], [], []
    v_first = None
    for i in range(depth):
        ns = s50.shape[2] * s50.shape[3]
        s5r0 = s50[i, ..., 0].reshape(b, 1, ns)
        s5i0 = s50[i, ..., 1].reshape(b, 1, ns)
        x, sh, wk, s5r, s5i, v_first = _layer(x, p, shift0[i], wkv0[i], s5r0, s5i0, v_first, prm, i)
        shifts.append(sh)
        wkvs.append(wk)
        s5s.append(jnp.stack([s5r.reshape(s50.shape[1:4]), s5i.reshape(s50.shape[1:4])], axis=-1))
    return x, jnp.stack(shifts), jnp.stack(wkvs), jnp.stack(s5s)


def kernel(x_prompt, x_sample, state_shift, state_wkv, state_s5, p_prompt, p_sample, w_in, b_gate, s5_a_re, s5_a_im, s5_log_dt, s5_b_re, s5_b_im, s5_c_re, s5_c_im, s5_d, s5_w_glu, s5_b_glu, rw_mu, rw_w0, rw_w2, rw_a0, rw_a2, rw_g2, rw_v0, rw_v1, rw_v2, rw_k_k, rw_k_a, rw_r_k, rw_ln_g, rw_ln_b, w_pa, w_pb, w_o, ln1_g, ln1_b, w_rg, b_rg, w_re, b_re, w_gate, w_up, w_down, w_ple, w_pg, b_pg, ln2_g, ln2_b):
    depth, d = w_in.shape[0], w_in.shape[1]
    ngroups = w_rg.shape[-1]
    nper = w_re.shape[-1]
    nexp = ngroups * nper
    rowv = lambda a: a.reshape(a.shape[0], 1, -1)
    bf = lambda a: a.astype(BF16)
    abr, abi, wbu, wy = _s5_discretize(s5_a_re, s5_a_im, s5_log_dt, s5_b_re, s5_b_im, s5_c_re, s5_c_im)
    pad = ROUTER_LANES - ngroups - nexp
    w_router = jnp.concatenate([w_rg, w_re.reshape(depth, d, nexp), jnp.zeros((depth, d, pad), F32)], axis=2)
    b_router = jnp.concatenate([b_rg, b_re.reshape(depth, nexp), jnp.zeros((depth, pad), F32)], axis=1)
    prm = dict(
        w_in=bf(w_in), b_gate=rowv(b_gate),
        s5_abr=abr, s5_abi=abi, s5_wbu=wbu, s5_wy=wy, s5_d=rowv(s5_d), s5_wglu=bf(s5_w_glu),
        s5_bglu=rowv(s5_b_glu),
        mu=rowv(rw_mu), w0=rowv(rw_w0), w2=bf(rw_w2), a0=rowv(rw_a0), a2=bf(rw_a2), g2=bf(rw_g2),
        v0=rowv(rw_v0), v1=bf(rw_v1), v2=bf(rw_v2), k_k=rowv(rw_k_k), k_a=rowv(rw_k_a), r_k=rowv(rw_r_k),
        ln_g=rowv(rw_ln_g), ln_b=rowv(rw_ln_b),
        w_pa=bf(w_pa), w_pb=bf(w_pb), w_o=bf(w_o), ln1_g=rowv(ln1_g), ln1_b=rowv(ln1_b), w_pg=bf(w_pg),
        b_pg=rowv(b_pg), w_ple=bf(w_ple), w_router=w_router, b_router=rowv(b_router),
        w_gate=bf(w_gate), w_up=bf(w_up), w_down=bf(w_down), ln2_g=rowv(ln2_g), ln2_b=rowv(ln2_b),
        ngroups=ngroups, nper=nper)

    bp = x_prompt.shape[0]
    dt = x_prompt.dtype
    zero_shift = jnp.zeros((depth, bp) + state_shift.shape[2:], dt)
    zero_wkv = jnp.zeros((depth, bp) + state_wkv.shape[2:], dt)
    zero_s5 = jnp.zeros((depth, bp) + state_s5.shape[2:], dt)
    y_p, sh_p, wkv_p, s5_p = _trunk(x_prompt, p_prompt, zero_shift, zero_wkv, zero_s5, prm, depth)
    y_s, sh_s, wkv_s, s5_s = _trunk(x_sample, p_sample, state_shift, state_wkv, state_s5, prm, depth)
    return (y_p, y_s, sh_p, wkv_p, s5_p, sh_s, wkv_s, s5_s)
```

```python
import functools
import math

import jax
import jax.numpy as jnp
from jax import lax
from jax.experimental import pallas as pl
from jax.experimental.pallas import tpu as pltpu

F32 = jnp.float32
BF16 = jnp.bfloat16

DEPTH = 2
ALPHA = (2 * DEPTH) ** 0.25
LN_EPS = 1e-5
GN_EPS = 64e-5

S5_GROUP = 16
S5_GROUPS_PER_BLOCK = 8
RW_HEAD = 64
DECAY_SCALE = math.exp(-0.5)
ROUTER_LANES = 128
WKV_CHUNK = 64
VMEM_LIMIT = 56 * 1024 * 1024
WKV_TERMS = {"nt": 1, "inv": 1, "out": 1, "state": 1}
WKV_CHUNKS_PER_ITER = 4
WKV_STACK_ROWS = 256
TOKEN_TILE = 512
TIME_TILE = 256
MOE_TILE = 1024
MOE_EXPERTS_PER_STEP = 2
S5_SLAB_ROWS = 256
INPROJ_ROW_SLABS = 4
MERGE_ROW_SLABS = 2
MOE_BLOCK_ROWS = 128
MOE_ROW_ALIGN = 16


def _bdot(a, b):
    return jnp.dot(a.astype(BF16), b.astype(BF16), preferred_element_type=F32)


def _split(x, n):
    parts = []
    rem = x
    for j in range(n):
        p = rem.astype(BF16)
        parts.append(p)
        if j + 1 < n:
            rem = rem - p.astype(F32)
    return parts


_NN = (((1,), (0,)), ((), ()))
_NT = (((1,), (1,)), ((), ()))
_TN = (((0,), (0,)), ((), ()))


def _mdot(a_parts, b_parts, dims=_NN):
    order = max(len(a_parts), len(b_parts))
    acc = None
    for ia, pa in enumerate(a_parts):
        for ib, pb in enumerate(b_parts):
            if ia + ib < order:
                t = lax.dot_general(pa, pb, dims, preferred_element_type=F32)
                acc = t if acc is None else acc + t
    return acc


def _head_sum(x, hd, terms=1):
    n = x.shape[-1]
    same = (lax.broadcasted_iota(jnp.int32, (n, n), 0) // hd
            == lax.broadcasted_iota(jnp.int32, (n, n), 1) // hd)
    return _mdot(_split(x, terms), [same.astype(BF16)])


def _sigmoid(x):
    return 1.0 / (1.0 + jnp.exp(-x))


def _layer_norm(h, g, b):
    mu = jnp.mean(h, axis=-1, keepdims=True)
    d = h - mu
    var = jnp.mean(d * d, axis=-1, keepdims=True)
    return d * lax.rsqrt(var + LN_EPS) * g + b


def _params(sem):
    return pltpu.CompilerParams(dimension_semantics=sem, vmem_limit_bytes=VMEM_LIMIT)


def _full(shape):
    n = len(shape)
    return pl.BlockSpec(shape, lambda *_: (0,) * n)


def _of_layer(arr, layer):
    tail = arr.shape[1:]
    return pl.BlockSpec((None,) + tail, lambda *_: (layer,) + (0,) * len(tail))


def _inproj_body(*refs, s5w, rwc, w, dr, ar_, hd, seq_len, nslab, has_vmix):
    (x_ref, win_ref, bg_ref, sh0_ref, mu_ref, w0_ref, w2_ref, a0_ref, a2_ref, g2_ref, kk_ref,
     ka_ref) = refs[:12]
    refs = refs[12:]
    if has_vmix:
        vf_ref, v0_ref, v1_ref, v2_ref = refs[:4]
        refs = refs[4:]
    u_ref, gate_ref, r_ref, k_ref, v_ref, lw_ref, kkn_ref, bv_ref, g_ref, zlast_ref, carry = refs

    i = pl.program_id(0)
    tm = x_ref.shape[0]
    rs = tm // nslab
    slabs = [slice(j * rs, (j + 1) * rs) for j in range(nslab)]
    xb = [x_ref[s, :].astype(BF16) for s in slabs]
    for s, xv in zip(slabs, xb):
        u_ref[s, :] = jnp.dot(xv, win_ref[:, :s5w], preferred_element_type=F32)
    zr = [jnp.dot(xv, win_ref[:, s5w:s5w + rwc], preferred_element_type=F32) for xv in xb]
    for s, xv in zip(slabs, xb):
        zg = jnp.dot(xv, win_ref[:, s5w + rwc:], preferred_element_type=F32)
        gate_ref[s, :] = _sigmoid(zg + bg_ref[...])

    row = lax.broadcasted_iota(jnp.int32, (rs, rwc), 0)

    @pl.when(i == 0)
    def _():
        carry[...] = jnp.zeros_like(carry)

    zs = []
    for j, z in enumerate(zr):
        if j > 0:
            before = zr[j - 1][rs - 1:rs, :]
        elif tm <= seq_len:
            before = jnp.where((i * tm) % seq_len == 0, sh0_ref[0], carry[...])
        else:
            before = carry[...]
        prev = jnp.where(row == 0, before, pltpu.roll(z, 1, axis=0))
        for bb in range(tm // seq_len):
            if j * rs <= bb * seq_len < (j + 1) * rs:
                prev = jnp.where(row == bb * seq_len - j * rs, sh0_ref[bb], prev)
        zs.append(z + mu_ref[...] * (prev - z))
    carry[...] = zr[-1][rs - 1:rs, :]
    if tm <= seq_len:
        zlast_ref[0] = zr[-1][rs - 1:rs, :]
    else:
        for bb in range(tm // seq_len):
            last = (bb + 1) * seq_len - 1
            zlast_ref[bb] = zr[last // rs][last % rs:last % rs + 1, :]

    c3 = 3 * w
    wl = [w0_ref[...] + _bdot(jnp.tanh(z[:, c3:c3 + dr]), w2_ref[...]) for z in zs]
    a = [_sigmoid(a0_ref[...] + _bdot(z[:, c3 + dr:c3 + dr + ar_], a2_ref[...])) for z in zs]
    g = [_bdot(_sigmoid(z[:, c3 + dr + ar_:]), g2_ref[...]) for z in zs]
    v = [z[:, 2 * w:c3] for z in zs]
    if has_vmix:
        mix = [_sigmoid(v0_ref[...] + _bdot(_bdot(vv, v1_ref[...]), v2_ref[...])) for vv in v]
        v = [vv + (vf_ref[s, :] - vv) * m for s, vv, m in zip(slabs, v, mix)]
    kk = [z[:, w:2 * w] * kk_ref[...] for z in zs]
    ss = [_head_sum(q * q, hd) for q in kk]
    for j, s in enumerate(slabs):
        lw_ref[s, :] = -DECAY_SCALE * _sigmoid(wl[j])
        g_ref[s, :] = g[j]
        r_ref[s, :] = zs[j][:, :w]
        v_ref[s, :] = v[j]
        kkn = kk[j] * lax.rsqrt(jnp.maximum(ss[j], 1e-24))
        kkn_ref[s, :] = kkn
        bv_ref[s, :] = kkn * a[j]
        k_ref[s, :] = zs[j][:, w:2 * w] * (1.0 + (a[j] - 1.0) * ka_ref[...])


def _inproj(x, prm, layer, shift0, v_first, seq_len, tm):
    t, d = x.shape
    n_in = prm["w_in"].shape[2]
    rwc = prm["mu"].shape[-1]
    w = prm["w0"].shape[-1]
    s5w = prm["s5_d"].shape[-1]
    ng = n_in - s5w - rwc
    has_vmix = v_first is not None
    nseq = max(1, tm // seq_len)
    row = lambda n: pl.BlockSpec((tm, n), lambda i: (i, 0))
    names = ["w_in", "b_gate", None, "mu", "w0", "w2", "a0", "a2", "g2", "k_k", "k_a"]
    ins = [x] + [shift0 if n is None else prm[n] for n in names]
    specs = [row(d)] + [pl.BlockSpec((nseq, 1, rwc), lambda i: ((i * tm) // (seq_len * nseq), 0, 0))
                        if n is None else _of_layer(prm[n], layer) for n in names]
    if has_vmix:
        ins += [v_first] + [prm[n] for n in ("v0", "v1", "v2")]
        specs += [row(w)] + [_of_layer(prm[n], layer - 1) for n in ("v0", "v1", "v2")]
    ntile = t // tm
    outs = pl.pallas_call(
        functools.partial(_inproj_body, s5w=s5w, rwc=rwc, w=w, dr=prm["w2"].shape[1],
                          ar_=prm["a2"].shape[1], hd=RW_HEAD, seq_len=seq_len, nslab=INPROJ_ROW_SLABS,
                          has_vmix=has_vmix),
        grid=(ntile,),
        in_specs=specs,
        out_specs=[row(s5w), row(ng)] + [row(w)] * 7
                  + [pl.BlockSpec((nseq, 1, rwc), lambda i: (i, 0, 0))],
        out_shape=[jax.ShapeDtypeStruct((t, s5w), F32), jax.ShapeDtypeStruct((t, ng), F32)]
                  + [jax.ShapeDtypeStruct((t, w), F32)] * 7
                  + [jax.ShapeDtypeStruct((ntile * nseq, 1, rwc), F32)],
        scratch_shapes=[pltpu.VMEM((1, rwc), F32)],
        compiler_params=_params(("arbitrary",)),
        name="inproj_premix",
    )(*ins)
    zlast = outs[-1].reshape(t // seq_len, -1, 1, rwc)[:, -1]
    return list(outs[:-1]) + [zlast]


def _s5_body(u_ref, h0r_ref, h0i_ref, abr_ref, abi_ref, wbu_ref, wy_ref, d_ref, wglu_ref, bglu_ref,
             y_ref, hlr_ref, hli_ref, bur, bui, hr, hi, str_, sti, *, nb, tl, sub, nblk, cb, sb):
    i = pl.program_id(0)

    @pl.when(i == 0)
    def _():
        str_[...] = h0r_ref[...]
        sti[...] = h0i_ref[...]

    slabs = [(bi, s0) for s0 in range(0, tl, sub) for bi in range(nb)]
    for bi, s0 in slabs:
        u = u_ref[bi, s0:s0 + sub, :]
        for j in range(nblk):
            bu = _bdot(u[:, cb * j:cb * (j + 1)], wbu_ref[j])
            bur[bi, s0:s0 + sub, sb * j:sb * (j + 1)] = bu[:, :sb]
            bui[bi, s0:s0 + sub, sb * j:sb * (j + 1)] = bu[:, sb:]

    ar = abr_ref[...]
    ai = abi_ref[...]
    state = [(str_[bi], sti[bi]) for bi in range(nb)]
    for s0 in range(0, tl, sub):
        for t in range(s0, s0 + sub):
            for bi in range(nb):
                pr, pi = state[bi]
                nr = ar * pr - ai * pi + bur[bi, t:t + 1, :]
                ni = ar * pi + ai * pr + bui[bi, t:t + 1, :]
                hr[bi, t:t + 1, :] = nr
                hi[bi, t:t + 1, :] = ni
                state[bi] = (nr, ni)
        for bi in range(nb):
            u = u_ref[bi, s0:s0 + sub, :]
            ys = [_bdot(hr[bi, s0:s0 + sub, sb * j:sb * (j + 1)], wy_ref[j, :sb, :])
                  + _bdot(hi[bi, s0:s0 + sub, sb * j:sb * (j + 1)], wy_ref[j, sb:, :])
                  for j in range(nblk)]
            y = jax.nn.gelu(jnp.concatenate(ys, axis=1) + d_ref[...] * u)
            y_ref[bi, s0:s0 + sub, :] = y * _sigmoid(_bdot(y, wglu_ref[...]) + bglu_ref[...])
    for bi in range(nb):
        str_[bi] = state[bi][0]
        sti[bi] = state[bi][1]
        hlr_ref[bi] = state[bi][0]
        hli_ref[bi] = state[bi][1]


def _s5_mixer(u, h0r, h0i, prm, layer, tl):
    b, l, w = u.shape
    ns = h0r.shape[-1]
    nblk = w // (S5_GROUP * S5_GROUPS_PER_BLOCK)
    cb = w // nblk
    sb = ns // nblk
    seq = pl.BlockSpec((b, tl, w), lambda i: (0, i, 0))
    st = _full((b, 1, ns))
    names = ("s5_abr", "s5_abi", "s5_wbu", "s5_wy", "s5_d", "s5_wglu", "s5_bglu")
    return pl.pallas_call(
        functools.partial(_s5_body, nb=b, tl=tl, sub=min(S5_SLAB_ROWS, tl), nblk=nblk, cb=cb, sb=sb),
        grid=(l // tl,),
        in_specs=[seq, st, st] + [_of_layer(prm[n], layer) for n in names],
        out_specs=[seq, st, st],
        out_shape=[jax.ShapeDtypeStruct((b, l, w), F32), jax.ShapeDtypeStruct((b, 1, ns), F32),
                   jax.ShapeDtypeStruct((b, 1, ns), F32)],
        scratch_shapes=[pltpu.VMEM((b, tl, ns), F32)] * 4 + [pltpu.VMEM((b, 1, ns), F32)] * 2,
        compiler_params=_params(("arbitrary",)),
        name="s5_mixer",
    )(u, h0r, h0i, *[prm[n] for n in names])


def _s5_discretize(a_re, a_im, log_dt, b_re, b_im, c_re, c_im):
    nl, g, p = a_re.shape
    c = b_re.shape[-1]
    gb = S5_GROUPS_PER_BLOCK
    nblk = g // gb
    dt = jnp.exp(log_dt)[..., None]
    mag = jnp.exp(a_re * dt)
    ab_re = mag * jnp.cos(a_im * dt)
    ab_im = mag * jnp.sin(a_im * dt)
    den = a_re * a_re + a_im * a_im
    n_re = ab_re - 1.0
    k_re = (n_re * a_re + ab_im * a_im) / den
    k_im = (ab_im * a_re - n_re * a_im) / den
    bb_re = k_re[..., None] * b_re - k_im[..., None] * b_im
    bb_im = k_re[..., None] * b_im + k_im[..., None] * b_re
    eye = jnp.eye(gb, dtype=F32)

    def pack_in(bb):
        return jnp.einsum("ljgpc,gh->ljgchp", bb.reshape(nl, nblk, gb, p, c),
                          eye).reshape(nl, nblk, gb * c, gb * p)

    def pack_out(cc):
        return jnp.einsum("ljgcp,gh->ljgphc", cc.reshape(nl, nblk, gb, c, p),
                          eye).reshape(nl, nblk, gb * p, gb * c)

    wbu = jnp.concatenate([pack_in(bb_re), pack_in(bb_im)], axis=-1).astype(BF16)
    wy = jnp.concatenate([pack_out(c_re), -pack_out(c_im)], axis=2).astype(BF16)
    return ab_re.reshape(nl, 1, g * p), ab_im.reshape(nl, 1, g * p), wbu, wy


def _wkv_body(r_ref, k_ref, v_ref, lw_ref, kkn_ref, bv_ref, g_ref, rk_ref, lng_ref, lnb_ref, s0_ref,
              y_ref, sl_ref, s_scr, at_scr, rt_scr, kt_scr, bt_scr, kd_scr, bd_scr, ce_scr, y_scr,
              *, tl, c, cpi, nseq, nh, hd):
    i = pl.program_id(1)

    @pl.when(i == 0)
    def _():
        s_scr[...] = s0_ref[...]

    rowi = lax.broadcasted_iota(jnp.int32, (tl, tl), 0)
    coli = lax.broadcasted_iota(jnp.int32, (tl, tl), 1)
    same_chunk = (rowi // c) == (coli // c)
    lw = lw_ref[...]
    cl = _mdot([(same_chunk & (coli <= rowi)).astype(BF16)], _split(lw, 2))
    ce = jnp.concatenate([jnp.broadcast_to(cl[j * c + c - 1:(j + 1) * c, :], (c, cl.shape[1]))
                          for j in range(tl // c)], axis=0)
    kkn = kkn_ref[...]
    bv = bv_ref[...]
    k = k_ref[...]
    einv = jnp.exp(-cl)
    to_end = jnp.exp(ce - cl)
    at_scr[...] = -kkn * jnp.exp(cl - lw)
    rt_scr[...] = r_ref[...] * jnp.exp(cl)
    kt_scr[...] = k * einv
    bt_scr[...] = bv * einv
    kd_scr[...] = k * to_end
    bd_scr[...] = bv * to_end
    ce_scr[...] = ce

    ri = lax.broadcasted_iota(jnp.int32, (c, c), 0)
    ci_ = lax.broadcasted_iota(jnp.int32, (c, c), 1)
    lower = ci_ <= ri
    strict = ci_ < ri
    levels = int(math.log2(c))
    sp = lambda x, cls: _split(x, WKV_TERMS[cls])

    def chunk_group(gi, carry):
        base = pl.multiple_of(gi * (cpi * c), cpi * c)
        units = [(cc, h) for cc in range(cpi) for h in range(nh)]
        rows = [pl.ds(base + cc * c, c) for cc, _ in units]
        hsl = [slice(h * hd, (h + 1) * hd) for _, h in units]
        un = range(len(units))
        at = [at_scr[rows[u], hsl[u]] for u in un]
        rt = [rt_scr[rows[u], hsl[u]] for u in un]
        lhs = [sp(jnp.concatenate([at[u], rt[u]], axis=0), "nt") for u in un]
        rhs = [sp(jnp.concatenate([bt_scr[rows[u], hsl[u]], kt_scr[rows[u], hsl[u]]], axis=0), "nt")
               for u in un]
        pm = [_mdot(lhs[u], rhs[u], _NT) for u in un]
        n_ab = [jnp.where(strict, pm[u][:c, :c], 0.0) for u in un]
        a_ak = [sp(jnp.where(strict, pm[u][:c, c:], 0.0), "out") for u in un]
        a_rb = [sp(jnp.where(lower, pm[u][c:, :c], 0.0), "out") for u in un]
        a_rk = [sp(jnp.where(lower, pm[u][c:, c:], 0.0), "out") for u in un]
        v_p = [sp(v_ref[rows[u], hsl[u]], "out") for u in un]

        x = [jnp.concatenate([at[u], _mdot(a_ak[u], v_p[u])], axis=1) for u in un]
        rkv = [_mdot(a_rk[u], v_p[u]) for u in un]
        pw = n_ab
        for lv in range(levels):
            pw_p = [sp(pw[u], "inv") for u in un]
            x = [x[u] + _mdot(pw_p[u], sp(x[u], "inv")) for u in un]
            if lv + 1 < levels:
                pw = [_mdot(pw_p[u], pw_p[u]) for u in un]
        x_p = [sp(x[u], "out") for u in un]
        qy = [jnp.concatenate([rt[u], rkv[u]], axis=1) + _mdot(a_rb[u], x_p[u]) for u in un]
        mz = [_mdot(x_p[u], sp(bd_scr[rows[u], hsl[u]], "out"), _TN) for u in un]
        zt = [mz[u][hd:, :] + _mdot(v_p[u], sp(kd_scr[rows[u], hsl[u]], "out"), _TN) for u in un]

        s_cur = [[s_scr[q, h] for h in range(nh)] for q in range(nseq)]
        for cc in range(cpi):
            q = cc // ((tl // c) // nseq) if nseq > 1 else 0
            g_end = jnp.exp(ce_scr[pl.ds(base + cc * c, 1), :])
            s_p = [sp(s_cur[q][h], "state") for h in range(nh)]
            for h in range(nh):
                u = cc * nh + h
                y_scr[rows[u], hsl[u]] = _mdot(sp(qy[u][:, :hd], "state"), s_p[h], _NT) + qy[u][:, hd:]
            s_cur[q] = [s_cur[q][h] * g_end[:, hsl[h]]
                        + _mdot(s_p[h], sp(mz[cc * nh + h][:hd, :], "state")) + zt[cc * nh + h]
                        for h in range(nh)]
        for q in range(nseq):
            for h in range(nh):
                s_scr[q, h] = s_cur[q][h]
        return carry

    lax.fori_loop(0, tl // (cpi * c), chunk_group, 0)

    @pl.when(i == pl.num_programs(1) - 1)
    def _():
        sl_ref[...] = s_scr[...]

    y = y_scr[...]
    yc = y - _head_sum(y, hd) * (1.0 / hd)
    yv = _head_sum(yc * yc, hd) * (1.0 / hd)
    yn = yc * lax.rsqrt(yv + GN_EPS) * lng_ref[...] + lnb_ref[...]
    bonus = _head_sum(r_ref[...] * k_ref[...] * rk_ref[...], hd) * v_ref[...]
    y_ref[...] = (yn + bonus) * g_ref[...]


def _wkv(r, k, v, lw, kkn, bv, g, prm, layer, s0, tl, c):
    b, l, w = r.shape
    nh, hd = s0.shape[1], s0.shape[2]
    nseq = max(1, min(b, WKV_STACK_ROWS // l)) if l == tl else 1
    assert b % nseq == 0, (b, nseq)
    if nseq > 1:
        stack = lambda a: a.reshape(b // nseq, nseq * l, w)
        r, k, v, lw, kkn, bv, g = (stack(a) for a in (r, k, v, lw, kkn, bv, g))
        tl = nseq * l
    nchunk = tl // c
    cpi = min(WKV_CHUNKS_PER_ITER, nchunk) if nseq == 1 else nchunk
    seq = pl.BlockSpec((None, tl, w), lambda bi, i: (bi, i, 0))
    st = pl.BlockSpec((nseq, nh, hd, hd), lambda bi, i: (bi, 0, 0, 0))
    y, s_new = pl.pallas_call(
        functools.partial(_wkv_body, tl=tl, c=c, cpi=cpi, nseq=nseq, nh=nh, hd=hd),
        grid=(r.shape[0], r.shape[1] // tl),
        in_specs=[seq] * 7 + [_of_layer(prm[n], layer) for n in ("r_k", "ln_g", "ln_b")] + [st],
        out_specs=[seq, st],
        out_shape=[jax.ShapeDtypeStruct(r.shape, F32), jax.ShapeDtypeStruct(s0.shape, F32)],
        scratch_shapes=[pltpu.VMEM((nseq, nh, hd, hd), F32)] + [pltpu.VMEM((tl, w), F32)] * 8,
        compiler_params=_params(("parallel", "arbitrary")),
        name="wkv",
    )(r, k, v, lw, kkn, bv, g, prm["r_k"], prm["ln_g"], prm["ln_b"], s0)
    return y.reshape(b, l, w), s_new


def _merge_body(x_ref, ya_ref, yb_ref, gate_ref, p_ref, wpa_ref, wpb_ref, wo_ref, g1_ref, b1_ref,
                wpg_ref, bpg_ref, wple_ref, wr_ref, br_ref, x1_ref, base_ref, cmb_ref,
                *, d, ngroups, nper, nsplit):
    rs = x_ref.shape[0] // nsplit
    slabs = [slice(j * rs, (j + 1) * rs) for j in range(nsplit)]
    pa = [_bdot(ya_ref[s, :], wpa_ref[...]) for s in slabs]
    pb = [_bdot(yb_ref[s, :], wpb_ref[...]) for s in slabs]
    merged = [gate_ref[s, :d] * a + gate_ref[s, d:] * b for s, a, b in zip(slabs, pa, pb)]
    h = [ALPHA * x_ref[s, :] + _bdot(m, wo_ref[...]) for s, m in zip(slabs, merged)]
    x1s = [_layer_norm(v, g1_ref[...], b1_ref[...]) for v in h]
    pg = [_bdot(v, wpg_ref[...]) for v in x1s]
    pe = [_bdot(p_ref[s, :], wple_ref[...]) for s in slabs]
    wr_p = _split(wr_ref[...], 2)
    lg = [_mdot(_split(v, 2), wr_p) + br_ref[...] for v in x1s]
    for s, v, g_, e_, l_ in zip(slabs, x1s, pg, pe, lg):
        x1_ref[s, :] = v.astype(BF16)
        base_ref[s, :] = ALPHA * v + _sigmoid(g_ + bpg_ref[...]) * e_
        cmb_ref[s, :] = _route(l_, ngroups, nper)


def _route(logits, ngroups, nper):
    lane = lax.broadcasted_iota(jnp.int32, logits.shape, 1)
    neg = -jnp.inf
    big = logits.shape[1]
    first_of = lambda hit: jnp.min(jnp.where(hit, lane, big), axis=-1, keepdims=True)
    gl = jnp.where(lane < ngroups, logits, neg)
    gmax = jnp.max(gl, axis=-1, keepdims=True)
    g_sel = first_of(gl == gmax)
    g_w = 1.0 / jnp.sum(jnp.exp(gl - gmax), axis=-1, keepdims=True)
    lo = ngroups + nper * g_sel
    el = jnp.where((lane >= lo) & (lane < lo + nper), logits, neg)
    m1 = jnp.max(el, axis=-1, keepdims=True)
    i1 = first_of(el == m1)
    el2 = jnp.where(lane == i1, neg, el)
    m2 = jnp.max(el2, axis=-1, keepdims=True)
    i2 = first_of(el2 == m2)
    e2 = jnp.exp(m2 - m1)
    w1 = g_w / (1.0 + e2)
    w2 = g_w * e2 / (1.0 + e2)
    return (jnp.where(lane == i1, w1, 0.0) + jnp.where(lane == i2, w2, 0.0)
            + jnp.where(lane == 0, g_sel.astype(F32), 0.0))


def _merge(x, ya, yb, gate, p, layer, prm, tm):
    t, d = x.shape
    row = lambda n: pl.BlockSpec((tm, n), lambda i: (i, 0))
    ws = [prm[n] for n in ("w_pa", "w_pb", "w_o", "ln1_g", "ln1_b", "w_pg", "b_pg", "w_ple", "w_router",
                           "b_router")]
    return pl.pallas_call(
        functools.partial(_merge_body, d=d, ngroups=prm["ngroups"], nper=prm["nper"],
                          nsplit=MERGE_ROW_SLABS),
        grid=(t // tm,),
        in_specs=[row(d), row(ya.shape[1]), row(yb.shape[1]), row(gate.shape[1]),
                  pl.BlockSpec((None, tm, p.shape[2]), lambda i: (layer, i, 0))]
                 + [_of_layer(a, layer) for a in ws],
        out_specs=[row(d), row(d), row(ROUTER_LANES)],
        out_shape=[jax.ShapeDtypeStruct((t, d), BF16), jax.ShapeDtypeStruct((t, d), F32),
                   jax.ShapeDtypeStruct((t, ROUTER_LANES), F32)],
        compiler_params=_params(("parallel",)),
        name="merge_ln1_router",
    )(x, ya, yb, gate, p, *ws)


def _moe_body(start_ref, nblk_ref, x1_ref, base_ref, cmb_ref, drow_ref, dcol_ref, wg_ref, wu_ref, wd_ref,
              g2_ref, b2_ref, o_ref, xs_scr, cs_scr, acc_scr, *, lane0, ngroups, nper, blk):
    i = pl.program_id(0)
    e = pl.program_id(1)
    nslot, tm = xs_scr.shape[0], x1_ref.shape[0]
    eps = wg_ref.shape[0]

    @pl.when(e == 0)
    def _():
        slot = lax.broadcasted_iota(jnp.int32, (nslot, tm), 0)
        pm = (slot == drow_ref[...]).astype(BF16)
        xs_scr[...] = jnp.dot(pm, x1_ref[...], preferred_element_type=F32).astype(BF16)
        cs_scr[...] = _bdot(pm, cmb_ref[...])
        acc_scr[...] = jnp.zeros_like(acc_scr)

    grp = (e * eps) // nper
    row0 = start_ref[i * ngroups + grp]
    lane = lax.broadcasted_iota(jnp.int32, (blk, cs_scr.shape[1]), 1)

    def block(j, carry):
        rows = pl.ds(pl.multiple_of(row0 + j * blk, MOE_ROW_ALIGN), blk)
        xb = xs_scr[rows, :]
        cs = cs_scr[rows, :]
        part = None
        for ee in range(eps):
            ce = jnp.sum(jnp.where(lane == e * eps + ee + lane0, cs, 0.0), axis=-1, keepdims=True)
            hg = jnp.dot(xb, wg_ref[ee], preferred_element_type=F32)
            hu = jnp.dot(xb, wu_ref[ee], preferred_element_type=F32)
            t = _bdot(hg * _sigmoid(hg) * hu * ce, wd_ref[ee])
            part = t if part is None else part + t
        acc_scr[rows, :] += part
        return carry

    lax.fori_loop(0, nblk_ref[i * ngroups + grp], block, 0)

    @pl.when(e == pl.num_programs(1) - 1)
    def _():
        slot_t = lax.broadcasted_iota(jnp.int32, (tm, nslot), 1)
        pmt = (slot_t == dcol_ref[...]).astype(BF16)
        ffn = _bdot(pmt, acc_scr[...])
        o_ref[...] = _layer_norm(base_ref[...] + ffn, g2_ref[...], b2_ref[...])


def _moe_dispatch(cmb, tm, ngroups, blk):
    t = cmb.shape[0]
    nt = t // tm
    g = cmb[:, 0].reshape(nt, tm)
    oh = (g[..., None] == jnp.arange(ngroups, dtype=F32)).astype(F32)
    before = (jnp.arange(tm)[None, :] < jnp.arange(tm)[:, None]).astype(F32)
    rank = jnp.einsum("ts,nsg->ntg", before, oh)
    count = jnp.sum(oh, axis=1).astype(jnp.int32)
    nblk = (count + blk - 1) // blk
    span = (count + MOE_ROW_ALIGN - 1) // MOE_ROW_ALIGN * MOE_ROW_ALIGN
    start = jnp.cumsum(span, axis=-1) - span
    dest = jnp.sum(oh * (rank + start[:, None, :].astype(F32)), axis=-1).astype(jnp.int32)
    return start.reshape(-1), nblk.reshape(-1), dest.reshape(nt, 1, tm), dest.reshape(t, 1)


def _moe(x1, base, cmb, prm, layer, tm):
    t, d = base.shape
    _, ne, _, f = prm["w_gate"].shape
    ngroups, nper = prm["ngroups"], prm["nper"]
    eps = MOE_EXPERTS_PER_STEP
    blk = min(MOE_BLOCK_ROWS, tm)
    nslot = -(-(tm + ngroups * MOE_ROW_ALIGN + blk) // blk) * blk
    start, nblk, drow, dcol = _moe_dispatch(cmb, tm, ngroups, blk)
    row = lambda n: pl.BlockSpec((tm, n), lambda i, e, *_: (i, 0))
    wspec = lambda a, b: pl.BlockSpec((None, eps, a, b), lambda i, e, *_: (layer, e, 0, 0))
    vec = _of_layer(prm["ln2_g"], layer)
    return pl.pallas_call(
        functools.partial(_moe_body, lane0=ngroups, ngroups=ngroups, nper=nper, blk=blk),
        grid_spec=pltpu.PrefetchScalarGridSpec(
            num_scalar_prefetch=2,
            grid=(t // tm, ne // eps),
            in_specs=[row(d), row(d), row(ROUTER_LANES),
                      pl.BlockSpec((None, 1, tm), lambda i, e, *_: (i, 0, 0)), row(1),
                      wspec(d, f), wspec(d, f), wspec(f, d), vec, vec],
            out_specs=row(d),
            scratch_shapes=[pltpu.VMEM((nslot, d), BF16), pltpu.VMEM((nslot, ROUTER_LANES), F32),
                            pltpu.VMEM((nslot, d), F32)]),
        out_shape=jax.ShapeDtypeStruct((t, d), F32),
        compiler_params=_params(("parallel", "arbitrary")),
        name="moe_ln2",
    )(start, nblk, x1, base, cmb, drow, dcol, prm["w_gate"], prm["w_up"], prm["w_down"],
      prm["ln2_g"], prm["ln2_b"])


def _tile(n, target):
    t = min(n, target)
    assert n % t == 0, (n, t)
    return t


def _layer(x, p, shift0, wkv0, s5r0, s5i0, v_first, prm, layer):
    b, l, d = x.shape
    t = b * l
    vf = None if v_first is None else v_first.reshape(t, -1)
    u, gate, r, k, v, lw, kkn, bv, g, shift_new = _inproj(x.reshape(t, d), prm, layer, shift0, vf, l,
                                                          _tile(t, TOKEN_TILE))
    seq3 = lambda a: a.reshape(b, l, -1)
    ya, s5r, s5i = _s5_mixer(seq3(u), s5r0, s5i0, prm, layer, _tile(l, TIME_TILE))
    yb, wkv_new = _wkv(seq3(r), seq3(k), seq3(v), seq3(lw), seq3(kkn), seq3(bv), seq3(g), prm, layer, wkv0,
                       _tile(l, TIME_TILE), min(WKV_CHUNK, l))
    x1, base, cmb = _merge(x.reshape(t, d), ya.reshape(t, -1), yb.reshape(t, -1), gate,
                           p.reshape(p.shape[0], t, -1), layer, prm, _tile(t, TOKEN_TILE))
    out = _moe(x1, base, cmb, prm, layer, _tile(t, MOE_TILE))
    return out.reshape(b, l, d), shift_new, wkv_new, s5r, s5i, (seq3(v) if v_first is None else v_first)


def _trunk(x, p, shift0, wkv0, s50, prm, depth):
    b = x.shape[0]
    shifts, wkvs, s5s = [], [], []
    v_first = None
    for i in range(depth):
        ns = s50.shape[2] * s50.shape[3]
        s5r0 = s50[i, ..., 0].reshape(b, 1, ns)
        s5i0 = s50[i, ..., 1].reshape(b, 1, ns)
        x, sh, wk, s5r, s5i, v_first = _layer(x, p, shift0[i], wkv0[i], s5r0, s5i0, v_first, prm, i)
        shifts.append(sh)
        wkvs.append(wk)
        s5s.append(jnp.stack([s5r.reshape(s50.shape[1:4]), s5i.reshape(s50.shape[1:4])], axis=-1))
    return x, jnp.stack(shifts), jnp.stack(wkvs), jnp.stack(s5s)


def kernel(x_prompt, x_sample, state_shift, state_wkv, state_s5, p_prompt, p_sample, w_in, b_gate, s5_a_re, s5_a_im, s5_log_dt, s5_b_re, s5_b_im, s5_c_re, s5_c_im, s5_d, s5_w_glu, s5_b_glu, rw_mu, rw_w0, rw_w2, rw_a0, rw_a2, rw_g2, rw_v0, rw_v1, rw_v2, rw_k_k, rw_k_a, rw_r_k, rw_ln_g, rw_ln_b, w_pa, w_pb, w_o, ln1_g, ln1_b, w_rg, b_rg, w_re, b_re, w_gate, w_up, w_down, w_ple, w_pg, b_pg, ln2_g, ln2_b):
    depth, d = w_in.shape[0], w_in.shape[1]
    ngroups = w_rg.shape[-1]
    nper = w_re.shape[-1]
    nexp = ngroups * nper
    rowv = lambda a: a.reshape(a.shape[0], 1, -1)
    bf = lambda a: a.astype(BF16)
    abr, abi, wbu, wy = _s5_discretize(s5_a_re, s5_a_im, s5_log_dt, s5_b_re, s5_b_im, s5_c_re, s5_c_im)
    pad = ROUTER_LANES - ngroups - nexp
    w_router = jnp.concatenate([w_rg, w_re.reshape(depth, d, nexp), jnp.zeros((depth, d, pad), F32)], axis=2)
    b_router = jnp.concatenate([b_rg, b_re.reshape(depth, nexp), jnp.zeros((depth, pad), F32)], axis=1)
    prm = dict(
        w_in=bf(w_in), b_gate=rowv(b_gate),
        s5_abr=abr, s5_abi=abi, s5_wbu=wbu, s5_wy=wy, s5_d=rowv(s5_d), s5_wglu=bf(s5_w_glu),
        s5_bglu=rowv(s5_b_glu),
        mu=rowv(rw_mu), w0=rowv(rw_w0), w2=bf(rw_w2), a0=rowv(rw_a0), a2=bf(rw_a2), g2=bf(rw_g2),
        v0=rowv(rw_v0), v1=bf(rw_v1), v2=bf(rw_v2), k_k=rowv(rw_k_k), k_a=rowv(rw_k_a), r_k=rowv(rw_r_k),
        ln_g=rowv(rw_ln_g), ln_b=rowv(rw_ln_b),
        w_pa=bf(w_pa), w_pb=bf(w_pb), w_o=bf(w_o), ln1_g=rowv(ln1_g), ln1_b=rowv(ln1_b), w_pg=bf(w_pg),
        b_pg=rowv(b_pg), w_ple=bf(w_ple), w_router=w_router, b_router=rowv(b_router),
        w_gate=bf(w_gate), w_up=bf(w_up), w_down=bf(w_down), ln2_g=rowv(ln2_g), ln2_b=rowv(ln2_b),
        ngroups=ngroups, nper=nper)

    bp = x_prompt.shape[0]
    dt = x_prompt.dtype
    zero_shift = jnp.zeros((depth, bp) + state_shift.shape[2:], dt)
    zero_wkv = jnp.zeros((depth, bp) + state_wkv.shape[2:], dt)
    zero_s5 = jnp.zeros((depth, bp) + state_s5.shape[2:], dt)
    y_p, sh_p, wkv_p, s5_p = _trunk(x_prompt, p_prompt, zero_shift, zero_wkv, zero_s5, prm, depth)
    y_s, sh_s, wkv_s, s5_s = _trunk(x_sample, p_sample, state_shift, state_wkv, state_s5, prm, depth)
    return (y_p, y_s, sh_p, wkv_p, s5_p, sh_s, wkv_s, s5_s)
```

```python
import functools
import math

import jax
import jax.numpy as jnp
from jax import lax
from jax.experimental import pallas as pl
from jax.experimental.pallas import tpu as pltpu

F32 = jnp.float32
BF16 = jnp.bfloat16

DEPTH = 2
ALPHA = (2 * DEPTH) ** 0.25
LN_EPS = 1e-5
GN_EPS = 64e-5

S5_GROUP = 16
S5_GROUPS_PER_BLOCK = 8
RW_HEAD = 64
DECAY_SCALE = math.exp(-0.5)
ROUTER_LANES = 128
WKV_CHUNK = 64
VMEM_LIMIT = 60 * 1024 * 1024
WKV_TERMS = {"nt": 1, "inv": 1, "out": 1, "state": 1}
WKV_CHUNKS_PER_ITER = 4
WKV_STACK_ROWS = 256
TOKEN_TILE = 512
TIME_TILE = 256
MOE_TILE = 1024
MOE_EXPERTS_PER_STEP = 4
S5_SLAB_ROWS = 256
INPROJ_ROW_SLABS = 2
MERGE_ROW_SLABS = 2
MOE_BLOCK_ROWS = 128
MOE_ROW_ALIGN = 16


def _bdot(a, b):
    return jnp.dot(a.astype(BF16), b.astype(BF16), preferred_element_type=F32)


def _split(x, n):
    parts = []
    rem = x
    for j in range(n):
        p = rem.astype(BF16)
        parts.append(p)
        if j + 1 < n:
            rem = rem - p.astype(F32)
    return parts


_NN = (((1,), (0,)), ((), ()))
_NT = (((1,), (1,)), ((), ()))
_TN = (((0,), (0,)), ((), ()))


def _mdot(a_parts, b_parts, dims=_NN):
    order = max(len(a_parts), len(b_parts))
    acc = None
    for ia, pa in enumerate(a_parts):
        for ib, pb in enumerate(b_parts):
            if ia + ib < order:
                t = lax.dot_general(pa, pb, dims, preferred_element_type=F32)
                acc = t if acc is None else acc + t
    return acc


def _head_sum(x, hd, terms=1):
    n = x.shape[-1]
    same = (lax.broadcasted_iota(jnp.int32, (n, n), 0) // hd
            == lax.broadcasted_iota(jnp.int32, (n, n), 1) // hd)
    return _mdot(_split(x, terms), [same.astype(BF16)])


def _sigmoid(x):
    return 1.0 / (1.0 + jnp.exp(-x))


def _layer_norm(h, g, b):
    mu = jnp.mean(h, axis=-1, keepdims=True)
    d = h - mu
    var = jnp.mean(d * d, axis=-1, keepdims=True)
    return d * lax.rsqrt(var + LN_EPS) * g + b


def _params(sem):
    return pltpu.CompilerParams(dimension_semantics=sem, vmem_limit_bytes=VMEM_LIMIT)


def _full(shape):
    n = len(shape)
    return pl.BlockSpec(shape, lambda *_: (0,) * n)


def _of_layer(arr, layer):
    tail = arr.shape[1:]
    return pl.BlockSpec((None,) + tail, lambda *_: (layer,) + (0,) * len(tail))


def _inproj_body(*refs, s5w, rwc, w, dr, ar_, hd, seq_len, nslab, has_vmix):
    (x_ref, win_ref, bg_ref, sh0_ref, mu_ref, w0_ref, w2_ref, a0_ref, a2_ref, g2_ref, kk_ref,
     ka_ref) = refs[:12]
    refs = refs[12:]
    if has_vmix:
        vf_ref, v0_ref, v1_ref, v2_ref = refs[:4]
        refs = refs[4:]
    u_ref, gate_ref, r_ref, k_ref, v_ref, lw_ref, kkn_ref, bv_ref, g_ref, zlast_ref, carry = refs

    i = pl.program_id(0)
    tm = x_ref.shape[0]
    rs = tm // nslab
    slabs = [slice(j * rs, (j + 1) * rs) for j in range(nslab)]
    xb = [x_ref[s, :].astype(BF16) for s in slabs]
    for s, xv in zip(slabs, xb):
        u_ref[s, :] = jnp.dot(xv, win_ref[:, :s5w], preferred_element_type=F32)
    zr = [jnp.dot(xv, win_ref[:, s5w:s5w + rwc], preferred_element_type=F32) for xv in xb]
    for s, xv in zip(slabs, xb):
        zg = jnp.dot(xv, win_ref[:, s5w + rwc:], preferred_element_type=F32)
        gate_ref[s, :] = _sigmoid(zg + bg_ref[...])

    row = lax.broadcasted_iota(jnp.int32, (rs, rwc), 0)

    @pl.when(i == 0)
    def _():
        carry[...] = jnp.zeros_like(carry)

    zs = []
    for j, z in enumerate(zr):
        if j > 0:
            before = zr[j - 1][rs - 1:rs, :]
        elif tm <= seq_len:
            before = jnp.where((i * tm) % seq_len == 0, sh0_ref[0], carry[...])
        else:
            before = carry[...]
        prev = jnp.where(row == 0, before, pltpu.roll(z, 1, axis=0))
        for bb in range(tm // seq_len):
            if j * rs <= bb * seq_len < (j + 1) * rs:
                prev = jnp.where(row == bb * seq_len - j * rs, sh0_ref[bb], prev)
        zs.append(z + mu_ref[...] * (prev - z))
    carry[...] = zr[-1][rs - 1:rs, :]
    if tm <= seq_len:
        zlast_ref[0] = zr[-1][rs - 1:rs, :]
    else:
        for bb in range(tm // seq_len):
            last = (bb + 1) * seq_len - 1
            zlast_ref[bb] = zr[last // rs][last % rs:last % rs + 1, :]

    c3 = 3 * w
    wl = [w0_ref[...] + _bdot(jnp.tanh(z[:, c3:c3 + dr]), w2_ref[...]) for z in zs]
    a = [_sigmoid(a0_ref[...] + _bdot(z[:, c3 + dr:c3 + dr + ar_], a2_ref[...])) for z in zs]
    g = [_bdot(_sigmoid(z[:, c3 + dr + ar_:]), g2_ref[...]) for z in zs]
    v = [z[:, 2 * w:c3] for z in zs]
    if has_vmix:
        mix = [_sigmoid(v0_ref[...] + _bdot(_bdot(vv, v1_ref[...]), v2_ref[...])) for vv in v]
        v = [vv + (vf_ref[s, :] - vv) * m for s, vv, m in zip(slabs, v, mix)]
    kk = [z[:, w:2 * w] * kk_ref[...] for z in zs]
    ss = [_head_sum(q * q, hd) for q in kk]
    for j, s in enumerate(slabs):
        lw_ref[s, :] = -DECAY_SCALE * _sigmoid(wl[j])
        g_ref[s, :] = g[j]
        r_ref[s, :] = zs[j][:, :w]
        v_ref[s, :] = v[j]
        kkn = kk[j] * lax.rsqrt(jnp.maximum(ss[j], 1e-24))
        kkn_ref[s, :] = kkn
        bv_ref[s, :] = kkn * a[j]
        k_ref[s, :] = zs[j][:, w:2 * w] * (1.0 + (a[j] - 1.0) * ka_ref[...])


def _inproj(x, prm, layer, shift0, v_first, seq_len, tm):
    t, d = x.shape
    n_in = prm["w_in"].shape[2]
    rwc = prm["mu"].shape[-1]
    w = prm["w0"].shape[-1]
    s5w = prm["s5_d"].shape[-1]
    ng = n_in - s5w - rwc
    has_vmix = v_first is not None
    nseq = max(1, tm // seq_len)
    row = lambda n: pl.BlockSpec((tm, n), lambda i: (i, 0))
    names = ["w_in", "b_gate", None, "mu", "w0", "w2", "a0", "a2", "g2", "k_k", "k_a"]
    ins = [x] + [shift0 if n is None else prm[n] for n in names]
    specs = [row(d)] + [pl.BlockSpec((nseq, 1, rwc), lambda i: ((i * tm) // (seq_len * nseq), 0, 0))
                        if n is None else _of_layer(prm[n], layer) for n in names]
    if has_vmix:
        ins += [v_first] + [prm[n] for n in ("v0", "v1", "v2")]
        specs += [row(w)] + [_of_layer(prm[n], layer - 1) for n in ("v0", "v1", "v2")]
    ntile = t // tm
    outs = pl.pallas_call(
        functools.partial(_inproj_body, s5w=s5w, rwc=rwc, w=w, dr=prm["w2"].shape[1],
                          ar_=prm["a2"].shape[1], hd=RW_HEAD, seq_len=seq_len, nslab=INPROJ_ROW_SLABS,
                          has_vmix=has_vmix),
        grid=(ntile,),
        in_specs=specs,
        out_specs=[row(s5w), row(ng)] + [row(w)] * 7
                  + [pl.BlockSpec((nseq, 1, rwc), lambda i: (i, 0, 0))],
        out_shape=[jax.ShapeDtypeStruct((t, s5w), F32), jax.ShapeDtypeStruct((t, ng), F32)]
                  + [jax.ShapeDtypeStruct((t, w), F32)] * 7
                  + [jax.ShapeDtypeStruct((ntile * nseq, 1, rwc), F32)],
        scratch_shapes=[pltpu.VMEM((1, rwc), F32)],
        compiler_params=_params(("arbitrary",)),
        name="inproj_premix",
    )(*ins)
    zlast = outs[-1].reshape(t // seq_len, -1, 1, rwc)[:, -1]
    return list(outs[:-1]) + [zlast]


def _s5_body(u_ref, h0r_ref, h0i_ref, abr_ref, abi_ref, wbu_ref, wy_ref, d_ref, wglu_ref, bglu_ref,
             y_ref, hlr_ref, hli_ref, bur, bui, hr, hi, str_, sti, *, nb, tl, sub, nblk, cb, sb):
    i = pl.program_id(0)

    @pl.when(i == 0)
    def _():
        str_[...] = h0r_ref[...]
        sti[...] = h0i_ref[...]

    slabs = [(bi, s0) for s0 in range(0, tl, sub) for bi in range(nb)]
    for bi, s0 in slabs:
        u = u_ref[bi, s0:s0 + sub, :]
        for j in range(nblk):
            bu = _bdot(u[:, cb * j:cb * (j + 1)], wbu_ref[j])
            bur[bi, s0:s0 + sub, sb * j:sb * (j + 1)] = bu[:, :sb]
            bui[bi, s0:s0 + sub, sb * j:sb * (j + 1)] = bu[:, sb:]

    ar = abr_ref[...]
    ai = abi_ref[...]
    state = [(str_[bi], sti[bi]) for bi in range(nb)]
    for s0 in range(0, tl, sub):
        for t in range(s0, s0 + sub):
            for bi in range(nb):
                pr, pi = state[bi]
                nr = ar * pr - ai * pi + bur[bi, t:t + 1, :]
                ni = ar * pi + ai * pr + bui[bi, t:t + 1, :]
                hr[bi, t:t + 1, :] = nr
                hi[bi, t:t + 1, :] = ni
                state[bi] = (nr, ni)
        for bi in range(nb):
            u = u_ref[bi, s0:s0 + sub, :]
            ys = [_bdot(hr[bi, s0:s0 + sub, sb * j:sb * (j + 1)], wy_ref[j, :sb, :])
                  + _bdot(hi[bi, s0:s0 + sub, sb * j:sb * (j + 1)], wy_ref[j, sb:, :])
                  for j in range(nblk)]
            y = jax.nn.gelu(jnp.concatenate(ys, axis=1) + d_ref[...] * u)
            y_ref[bi, s0:s0 + sub, :] = y * _sigmoid(_bdot(y, wglu_ref[...]) + bglu_ref[...])
    for bi in range(nb):
        str_[bi] = state[bi][0]
        sti[bi] = state[bi][1]
        hlr_ref[bi] = state[bi][0]
        hli_ref[bi] = state[bi][1]


def _s5_mixer(u, h0r, h0i, prm, layer, tl):
    b, l, w = u.shape
    ns = h0r.shape[-1]
    nblk = w // (S5_GROUP * S5_GROUPS_PER_BLOCK)
    cb = w // nblk
    sb = ns // nblk
    seq = pl.BlockSpec((b, tl, w), lambda i: (0, i, 0))
    st = _full((b, 1, ns))
    names = ("s5_abr", "s5_abi", "s5_wbu", "s5_wy", "s5_d", "s5_wglu", "s5_bglu")
    return pl.pallas_call(
        functools.partial(_s5_body, nb=b, tl=tl, sub=min(S5_SLAB_ROWS, tl), nblk=nblk, cb=cb, sb=sb),
        grid=(l // tl,),
        in_specs=[seq, st, st] + [_of_layer(prm[n], layer) for n in names],
        out_specs=[seq, st, st],
        out_shape=[jax.ShapeDtypeStruct((b, l, w), F32), jax.ShapeDtypeStruct((b, 1, ns), F32),
                   jax.ShapeDtypeStruct((b, 1, ns), F32)],
        scratch_shapes=[pltpu.VMEM((b, tl, ns), F32)] * 4 + [pltpu.VMEM((b, 1, ns), F32)] * 2,
        compiler_params=_params(("arbitrary",)),
        name="s5_mixer",
    )(u, h0r, h0i, *[prm[n] for n in names])


def _s5_discretize(a_re, a_im, log_dt, b_re, b_im, c_re, c_im):
    nl, g, p = a_re.shape
    c = b_re.shape[-1]
    gb = S5_GROUPS_PER_BLOCK
    nblk = g // gb
    dt = jnp.exp(log_dt)[..., None]
    mag = jnp.exp(a_re * dt)
    ab_re = mag * jnp.cos(a_im * dt)
    ab_im = mag * jnp.sin(a_im * dt)
    den = a_re * a_re + a_im * a_im
    n_re = ab_re - 1.0
    k_re = (n_re * a_re + ab_im * a_im) / den
    k_im = (ab_im * a_re - n_re * a_im) / den
    bb_re = k_re[..., None] * b_re - k_im[..., None] * b_im
    bb_im = k_re[..., None] * b_im + k_im[..., None] * b_re
    eye = jnp.eye(gb, dtype=F32)

    def pack_in(bb):
        return jnp.einsum("ljgpc,gh->ljgchp", bb.reshape(nl, nblk, gb, p, c),
                          eye).reshape(nl, nblk, gb * c, gb * p)

    def pack_out(cc):
        return jnp.einsum("ljgcp,gh->ljgphc", cc.reshape(nl, nblk, gb, c, p),
                          eye).reshape(nl, nblk, gb * p, gb * c)

    wbu = jnp.concatenate([pack_in(bb_re), pack_in(bb_im)], axis=-1).astype(BF16)
    wy = jnp.concatenate([pack_out(c_re), -pack_out(c_im)], axis=2).astype(BF16)
    return ab_re.reshape(nl, 1, g * p), ab_im.reshape(nl, 1, g * p), wbu, wy


def _wkv_body(r_ref, k_ref, v_ref, lw_ref, kkn_ref, bv_ref, g_ref, rk_ref, lng_ref, lnb_ref, s0_ref,
              y_ref, sl_ref, s_scr, at_scr, rt_scr, kt_scr, bt_scr, kd_scr, bd_scr, ce_scr, y_scr,
              *, tl, c, cpi, nseq, nh, hd):
    i = pl.program_id(1)

    @pl.when(i == 0)
    def _():
        s_scr[...] = s0_ref[...]

    rowi = lax.broadcasted_iota(jnp.int32, (tl, tl), 0)
    coli = lax.broadcasted_iota(jnp.int32, (tl, tl), 1)
    same_chunk = (rowi // c) == (coli // c)
    lw = lw_ref[...]
    cl = _mdot([(same_chunk & (coli <= rowi)).astype(BF16)], _split(lw, 2))
    ce = jnp.concatenate([jnp.broadcast_to(cl[j * c + c - 1:(j + 1) * c, :], (c, cl.shape[1]))
                          for j in range(tl // c)], axis=0)
    kkn = kkn_ref[...]
    bv = bv_ref[...]
    k = k_ref[...]
    einv = jnp.exp(-cl)
    to_end = jnp.exp(ce - cl)
    at_scr[...] = -kkn * jnp.exp(cl - lw)
    rt_scr[...] = r_ref[...] * jnp.exp(cl)
    kt_scr[...] = k * einv
    bt_scr[...] = bv * einv
    kd_scr[...] = k * to_end
    bd_scr[...] = bv * to_end
    ce_scr[...] = ce

    ri = lax.broadcasted_iota(jnp.int32, (c, c), 0)
    ci_ = lax.broadcasted_iota(jnp.int32, (c, c), 1)
    lower = ci_ <= ri
    strict = ci_ < ri
    levels = int(math.log2(c))
    sp = lambda x, cls: _split(x, WKV_TERMS[cls])

    def chunk_group(gi, carry):
        base = pl.multiple_of(gi * (cpi * c), cpi * c)
        units = [(cc, h) for cc in range(cpi) for h in range(nh)]
        rows = [pl.ds(base + cc * c, c) for cc, _ in units]
        hsl = [slice(h * hd, (h + 1) * hd) for _, h in units]
        un = range(len(units))
        at = [at_scr[rows[u], hsl[u]] for u in un]
        rt = [rt_scr[rows[u], hsl[u]] for u in un]
        lhs = [sp(jnp.concatenate([at[u], rt[u]], axis=0), "nt") for u in un]
        rhs = [sp(jnp.concatenate([bt_scr[rows[u], hsl[u]], kt_scr[rows[u], hsl[u]]], axis=0), "nt")
               for u in un]
        pm = [_mdot(lhs[u], rhs[u], _NT) for u in un]
        n_ab = [jnp.where(strict, pm[u][:c, :c], 0.0) for u in un]
        a_ak = [sp(jnp.where(strict, pm[u][:c, c:], 0.0), "out") for u in un]
        a_rb = [sp(jnp.where(lower, pm[u][c:, :c], 0.0), "out") for u in un]
        a_rk = [sp(jnp.where(lower, pm[u][c:, c:], 0.0), "out") for u in un]
        v_p = [sp(v_ref[rows[u], hsl[u]], "out") for u in un]

        x = [jnp.concatenate([at[u], _mdot(a_ak[u], v_p[u])], axis=1) for u in un]
        rkv = [_mdot(a_rk[u], v_p[u]) for u in un]
        pw = n_ab
        for lv in range(levels):
            pw_p = [sp(pw[u], "inv") for u in un]
            x = [x[u] + _mdot(pw_p[u], sp(x[u], "inv")) for u in un]
            if lv + 1 < levels:
                pw = [_mdot(pw_p[u], pw_p[u]) for u in un]
        x_p = [sp(x[u], "out") for u in un]
        qy = [jnp.concatenate([rt[u], rkv[u]], axis=1) + _mdot(a_rb[u], x_p[u]) for u in un]
        mz = [_mdot(x_p[u], sp(bd_scr[rows[u], hsl[u]], "out"), _TN) for u in un]
        zt = [mz[u][hd:, :] + _mdot(v_p[u], sp(kd_scr[rows[u], hsl[u]], "out"), _TN) for u in un]

        s_cur = [[s_scr[q, h] for h in range(nh)] for q in range(nseq)]
        for cc in range(cpi):
            q = cc // ((tl // c) // nseq) if nseq > 1 else 0
            g_end = jnp.exp(ce_scr[pl.ds(base + cc * c, 1), :])
            s_p = [sp(s_cur[q][h], "state") for h in range(nh)]
            for h in range(nh):
                u = cc * nh + h
                y_scr[rows[u], hsl[u]] = _mdot(sp(qy[u][:, :hd], "state"), s_p[h], _NT) + qy[u][:, hd:]
            s_cur[q] = [s_cur[q][h] * g_end[:, hsl[h]]
                        + _mdot(s_p[h], sp(mz[cc * nh + h][:hd, :], "state")) + zt[cc * nh + h]
                        for h in range(nh)]
        for q in range(nseq):
            for h in range(nh):
                s_scr[q, h] = s_cur[q][h]
        return carry

    lax.fori_loop(0, tl // (cpi * c), chunk_group, 0)

    @pl.when(i == pl.num_programs(1) - 1)
    def _():
        sl_ref[...] = s_scr[...]

    y = y_scr[...]
    yc = y - _head_sum(y, hd) * (1.0 / hd)
    yv = _head_sum(yc * yc, hd) * (1.0 / hd)
    yn = yc * lax.rsqrt(yv + GN_EPS) * lng_ref[...] + lnb_ref[...]
    bonus = _head_sum(r_ref[...] * k_ref[...] * rk_ref[...], hd) * v_ref[...]
    y_ref[...] = (yn + bonus) * g_ref[...]


def _wkv(r, k, v, lw, kkn, bv, g, prm, layer, s0, tl, c):
    b, l, w = r.shape
    nh, hd = s0.shape[1], s0.shape[2]
    nseq = max(1, min(b, WKV_STACK_ROWS // l)) if l == tl else 1
    assert b % nseq == 0, (b, nseq)
    if nseq > 1:
        stack = lambda a: a.reshape(b // nseq, nseq * l, w)
        r, k, v, lw, kkn, bv, g = (stack(a) for a in (r, k, v, lw, kkn, bv, g))
        tl = nseq * l
    nchunk = tl // c
    cpi = min(WKV_CHUNKS_PER_ITER, nchunk) if nseq == 1 else nchunk
    seq = pl.BlockSpec((None, tl, w), lambda bi, i: (bi, i, 0))
    st = pl.BlockSpec((nseq, nh, hd, hd), lambda bi, i: (bi, 0, 0, 0))
    y, s_new = pl.pallas_call(
        functools.partial(_wkv_body, tl=tl, c=c, cpi=cpi, nseq=nseq, nh=nh, hd=hd),
        grid=(r.shape[0], r.shape[1] // tl),
        in_specs=[seq] * 7 + [_of_layer(prm[n], layer) for n in ("r_k", "ln_g", "ln_b")] + [st],
        out_specs=[seq, st],
        out_shape=[jax.ShapeDtypeStruct(r.shape, F32), jax.ShapeDtypeStruct(s0.shape, F32)],
        scratch_shapes=[pltpu.VMEM((nseq, nh, hd, hd), F32)] + [pltpu.VMEM((tl, w), F32)] * 8,
        compiler_params=_params(("parallel", "arbitrary")),
        name="wkv",
    )(r, k, v, lw, kkn, bv, g, prm["r_k"], prm["ln_g"], prm["ln_b"], s0)
    return y.reshape(b, l, w), s_new


def _merge_body(x_ref, ya_ref, yb_ref, gate_ref, p_ref, wpa_ref, wpb_ref, wo_ref, g1_ref, b1_ref,
                wpg_ref, bpg_ref, wple_ref, wr_ref, br_ref, x1_ref, base_ref, cmb_ref,
                *, d, ngroups, nper, nsplit):
    rs = x_ref.shape[0] // nsplit
    slabs = [slice(j * rs, (j + 1) * rs) for j in range(nsplit)]
    pa = [_bdot(ya_ref[s, :], wpa_ref[...]) for s in slabs]
    pb = [_bdot(yb_ref[s, :], wpb_ref[...]) for s in slabs]
    merged = [gate_ref[s, :d] * a + gate_ref[s, d:] * b for s, a, b in zip(slabs, pa, pb)]
    h = [ALPHA * x_ref[s, :] + _bdot(m, wo_ref[...]) for s, m in zip(slabs, merged)]
    x1s = [_layer_norm(v, g1_ref[...], b1_ref[...]) for v in h]
    pg = [_bdot(v, wpg_ref[...]) for v in x1s]
    pe = [_bdot(p_ref[s, :], wple_ref[...]) for s in slabs]
    wr_p = _split(wr_ref[...], 2)
    lg = [_mdot(_split(v, 2), wr_p) + br_ref[...] for v in x1s]
    for s, v, g_, e_, l_ in zip(slabs, x1s, pg, pe, lg):
        x1_ref[s, :] = v.astype(BF16)
        base_ref[s, :] = ALPHA * v + _sigmoid(g_ + bpg_ref[...]) * e_
        cmb_ref[s, :] = _route(l_, ngroups, nper)


def _route(logits, ngroups, nper):
    lane = lax.broadcasted_iota(jnp.int32, logits.shape, 1)
    neg = -jnp.inf
    big = logits.shape[1]
    first_of = lambda hit: jnp.min(jnp.where(hit, lane, big), axis=-1, keepdims=True)
    gl = jnp.where(lane < ngroups, logits, neg)
    gmax = jnp.max(gl, axis=-1, keepdims=True)
    g_sel = first_of(gl == gmax)
    g_w = 1.0 / jnp.sum(jnp.exp(gl - gmax), axis=-1, keepdims=True)
    lo = ngroups + nper * g_sel
    el = jnp.where((lane >= lo) & (lane < lo + nper), logits, neg)
    m1 = jnp.max(el, axis=-1, keepdims=True)
    i1 = first_of(el == m1)
    el2 = jnp.where(lane == i1, neg, el)
    m2 = jnp.max(el2, axis=-1, keepdims=True)
    i2 = first_of(el2 == m2)
    e2 = jnp.exp(m2 - m1)
    w1 = g_w / (1.0 + e2)
    w2 = g_w * e2 / (1.0 + e2)
    return (jnp.where(lane == i1, w1, 0.0) + jnp.where(lane == i2, w2, 0.0)
            + jnp.where(lane == 0, g_sel.astype(F32), 0.0))


def _merge(x, ya, yb, gate, p, layer, prm, tm):
    t, d = x.shape
    row = lambda n: pl.BlockSpec((tm, n), lambda i: (i, 0))
    ws = [prm[n] for n in ("w_pa", "w_pb", "w_o", "ln1_g", "ln1_b", "w_pg", "b_pg", "w_ple", "w_router",
                           "b_router")]
    return pl.pallas_call(
        functools.partial(_merge_body, d=d, ngroups=prm["ngroups"], nper=prm["nper"],
                          nsplit=MERGE_ROW_SLABS),
        grid=(t // tm,),
        in_specs=[row(d), row(ya.shape[1]), row(yb.shape[1]), row(gate.shape[1]),
                  pl.BlockSpec((None, tm, p.shape[2]), lambda i: (layer, i, 0))]
                 + [_of_layer(a, layer) for a in ws],
        out_specs=[row(d), row(d), row(ROUTER_LANES)],
        out_shape=[jax.ShapeDtypeStruct((t, d), BF16), jax.ShapeDtypeStruct((t, d), F32),
                   jax.ShapeDtypeStruct((t, ROUTER_LANES), F32)],
        compiler_params=_params(("parallel",)),
        name="merge_ln1_router",
    )(x, ya, yb, gate, p, *ws)


def _moe_body(start_ref, nblk_ref, x1_ref, base_ref, cmb_ref, drow_ref, dcol_ref, wg_ref, wu_ref, wd_ref,
              g2_ref, b2_ref, o_ref, xs_scr, cs_scr, acc_scr, *, lane0, ngroups, nper, blk):
    i = pl.program_id(0)
    e = pl.program_id(1)
    nslot, tm = xs_scr.shape[0], x1_ref.shape[0]
    eps = wg_ref.shape[0]

    @pl.when(e == 0)
    def _():
        slot = lax.broadcasted_iota(jnp.int32, (nslot, tm), 0)
        pm = (slot == drow_ref[...]).astype(BF16)
        xs_scr[...] = jnp.dot(pm, x1_ref[...], preferred_element_type=F32).astype(BF16)
        cs_scr[...] = _bdot(pm, cmb_ref[...])
        acc_scr[...] = jnp.zeros_like(acc_scr)

    grp = (e * eps) // nper
    row0 = start_ref[i * ngroups + grp]
    lane = lax.broadcasted_iota(jnp.int32, (blk, cs_scr.shape[1]), 1)

    def block(j, carry):
        rows = pl.ds(pl.multiple_of(row0 + j * blk, MOE_ROW_ALIGN), blk)
        xb = xs_scr[rows, :]
        cs = cs_scr[rows, :]
        part = None
        for ee in range(eps):
            ce = jnp.sum(jnp.where(lane == e * eps + ee + lane0, cs, 0.0), axis=-1, keepdims=True)
            hg = jnp.dot(xb, wg_ref[ee], preferred_element_type=F32)
            hu = jnp.dot(xb, wu_ref[ee], preferred_element_type=F32)
            t = _bdot(hg * _sigmoid(hg) * hu * ce, wd_ref[ee])
            part = t if part is None else part + t
        acc_scr[rows, :] += part
        return carry

    lax.fori_loop(0, nblk_ref[i * ngroups + grp], block, 0)

    @pl.when(e == pl.num_programs(1) - 1)
    def _():
        slot_t = lax.broadcasted_iota(jnp.int32, (tm, nslot), 1)
        pmt = (slot_t == dcol_ref[...]).astype(BF16)
        ffn = _bdot(pmt, acc_scr[...])
        o_ref[...] = _layer_norm(base_ref[...] + ffn, g2_ref[...], b2_ref[...])


def _moe_dispatch(cmb, tm, ngroups, blk):
    t = cmb.shape[0]
    nt = t // tm
    g = cmb[:, 0].reshape(nt, tm)
    oh = (g[..., None] == jnp.arange(ngroups, dtype=F32)).astype(F32)
    before = (jnp.arange(tm)[None, :] < jnp.arange(tm)[:, None]).astype(F32)
    rank = jnp.einsum("ts,nsg->ntg", before, oh)
    count = jnp.sum(oh, axis=1).astype(jnp.int32)
    nblk = (count + blk - 1) // blk
    span = (count + MOE_ROW_ALIGN - 1) // MOE_ROW_ALIGN * MOE_ROW_ALIGN
    start = jnp.cumsum(span, axis=-1) - span
    dest = jnp.sum(oh * (rank + start[:, None, :].astype(F32)), axis=-1).astype(jnp.int32)
    return start.reshape(-1), nblk.reshape(-1), dest.reshape(nt, 1, tm), dest.reshape(t, 1)


def _moe(x1, base, cmb, prm, layer, tm):
    t, d = base.shape
    _, ne, _, f = prm["w_gate"].shape
    ngroups, nper = prm["ngroups"], prm["nper"]
    eps = MOE_EXPERTS_PER_STEP
    blk = min(MOE_BLOCK_ROWS, tm)
    nslot = -(-(tm + ngroups * MOE_ROW_ALIGN + blk) // blk) * blk
    start, nblk, drow, dcol = _moe_dispatch(cmb, tm, ngroups, blk)
    row = lambda n: pl.BlockSpec((tm, n), lambda i, e, *_: (i, 0))
    wspec = lambda a, b: pl.BlockSpec((None, eps, a, b), lambda i, e, *_: (layer, e, 0, 0))
    vec = _of_layer(prm["ln2_g"], layer)
    return pl.pallas_call(
        functools.partial(_moe_body, lane0=ngroups, ngroups=ngroups, nper=nper, blk=blk),
        grid_spec=pltpu.PrefetchScalarGridSpec(
            num_scalar_prefetch=2,
            grid=(t // tm, ne // eps),
            in_specs=[row(d), row(d), row(ROUTER_LANES),
                      pl.BlockSpec((None, 1, tm), lambda i, e, *_: (i, 0, 0)), row(1),
                      wspec(d, f), wspec(d, f), wspec(f, d), vec, vec],
            out_specs=row(d),
            scratch_shapes=[pltpu.VMEM((nslot, d), BF16), pltpu.VMEM((nslot, ROUTER_LANES), F32),
                            pltpu.VMEM((nslot, d), F32)]),
        out_shape=jax.ShapeDtypeStruct((t, d), F32),
        compiler_params=_params(("parallel", "arbitrary")),
        name="moe_ln2",
    )(start, nblk, x1, base, cmb, drow, dcol, prm["w_gate"], prm["w_up"], prm["w_down"],
      prm["ln2_g"], prm["ln2_b"])


def _tile(n, target):
    t = min(n, target)
    assert n % t == 0, (n, t)
    return t


def _layer(x, p, shift0, wkv0, s5r0, s5i0, v_first, prm, layer):
    b, l, d = x.shape
    t = b * l
    vf = None if v_first is None else v_first.reshape(t, -1)
    u, gate, r, k, v, lw, kkn, bv, g, shift_new = _inproj(x.reshape(t, d), prm, layer, shift0, vf, l,
                                                          _tile(t, TOKEN_TILE))
    seq3 = lambda a: a.reshape(b, l, -1)
    ya, s5r, s5i = _s5_mixer(seq3(u), s5r0, s5i0, prm, layer, _tile(l, TIME_TILE))
    yb, wkv_new = _wkv(seq3(r), seq3(k), seq3(v), seq3(lw), seq3(kkn), seq3(bv), seq3(g), prm, layer, wkv0,
                       _tile(l, TIME_TILE), min(WKV_CHUNK, l))
    x1, base, cmb = _merge(x.reshape(t, d), ya.reshape(t, -1), yb.reshape(t, -1), gate,
                           p.reshape(p.shape[0], t, -1), layer, prm, _tile(t, TOKEN_TILE))
    out = _moe(x1, base, cmb, prm, layer, _tile(t, MOE_TILE))
    return out.reshape(b, l, d), shift_new, wkv_new, s5r, s5i, (seq3(v) if v_first is None else v_first)


def _trunk(x, p, shift0, wkv0, s50, prm, depth):
    b = x.shape[0]
    shifts, wkvs, s5s = [], [], []
    v_first = None
    for i in range(depth):
        ns = s50.shape[2] * s50.shape[3]
        s5r0 = s50[i, ..., 0].reshape(b, 1, ns)
        s5i0 = s50[i, ..., 1].reshape(b, 1, ns)
        x, sh, wk, s5r, s5i, v_first = _layer(x, p, shift0[i], wkv0[i], s5r0, s5i0, v_first, prm, i)
        shifts.append(sh)
        wkvs.append(wk)
        s5s.append(jnp.stack([s5r.reshape(s50.shape[1:4]), s5i.reshape(s50.shape[1:4])], axis=-1))
    return x, jnp.stack(shifts), jnp.stack(wkvs), jnp.stack(s5s)


def kernel(x_prompt, x_sample, state_shift, state_wkv, state_s5, p_prompt, p_sample, w_in, b_gate, s5_a_re, s5_a_im, s5_log_dt, s5_b_re, s5_b_im, s5_c_re, s5_c_im, s5_d, s5_w_glu, s5_b_glu, rw_mu, rw_w0, rw_w2, rw_a0, rw_a2, rw_g2, rw_v0, rw_v1, rw_v2, rw_k_k, rw_k_a, rw_r_k, rw_ln_g, rw_ln_b, w_pa, w_pb, w_o, ln1_g, ln1_b, w_rg, b_rg, w_re, b_re, w_gate, w_up, w_down, w_ple, w_pg, b_pg, ln2_g, ln2_b):
    depth, d = w_in.shape[0], w_in.shape[1]
    ngroups = w_rg.shape[-1]
    nper = w_re.shape[-1]
    nexp = ngroups * nper
    rowv = lambda a: a.reshape(a.shape[0], 1, -1)
    bf = lambda a: a.astype(BF16)
    abr, abi, wbu, wy = _s5_discretize(s5_a_re, s5_a_im, s5_log_dt, s5_b_re, s5_b_im, s5_c_re, s5_c_im)
    pad = ROUTER_LANES - ngroups - nexp
    w_router = jnp.concatenate([w_rg, w_re.reshape(depth, d, nexp), jnp.zeros((depth, d, pad), F32)], axis=2)
    b_router = jnp.concatenate([b_rg, b_re.reshape(depth, nexp), jnp.zeros((depth, pad), F32)], axis=1)
    prm = dict(
        w_in=bf(w_in), b_gate=rowv(b_gate),
        s5_abr=abr, s5_abi=abi, s5_wbu=wbu, s5_wy=wy, s5_d=rowv(s5_d), s5_wglu=bf(s5_w_glu),
        s5_bglu=rowv(s5_b_glu),
        mu=rowv(rw_mu), w0=rowv(rw_w0), w2=bf(rw_w2), a0=rowv(rw_a0), a2=bf(rw_a2), g2=bf(rw_g2),
        v0=rowv(rw_v0), v1=bf(rw_v1), v2=bf(rw_v2), k_k=rowv(rw_k_k), k_a=rowv(rw_k_a), r_k=rowv(rw_r_k),
        ln_g=rowv(rw_ln_g), ln_b=rowv(rw_ln_b),
        w_pa=bf(w_pa), w_pb=bf(w_pb), w_o=bf(w_o), ln1_g=rowv(ln1_g), ln1_b=rowv(ln1_b), w_pg=bf(w_pg),
        b_pg=rowv(b_pg), w_ple=bf(w_ple), w_router=w_router, b_router=rowv(b_router),
        w_gate=bf(w_gate), w_up=bf(w_up), w_down=bf(w_down), ln2_g=rowv(ln2_g), ln2_b=rowv(ln2_b),
        ngroups=ngroups, nper=nper)

    bp = x_prompt.shape[0]
    dt = x_prompt.dtype
    zero_shift = jnp.zeros((depth, bp) + state_shift.shape[2:], dt)
    zero_wkv = jnp.zeros((depth, bp) + state_wkv.shape[2:], dt)
    zero_s5 = jnp.zeros((depth, bp) + state_s5.shape[2:], dt)
    y_p, sh_p, wkv_p, s5_p = _trunk(x_prompt, p_prompt, zero_shift, zero_wkv, zero_s5, prm, depth)
    y_s, sh_s, wkv_s, s5_s = _trunk(x_sample, p_sample, state_shift, state_wkv, state_s5, prm, depth)
    return (y_p, y_s, sh_p, wkv_p, s5_p, sh_s, wkv_s, s5_s)
```
